```python
import math
import jax
import jax.numpy as jnp
from jax import lax
import numpy as np

D_MODEL = 2048
BATCH = 2
SEQ = 16384
DEPTH = 2
DEC_BATCH = 8
DEC_SEQ = 64
PAST_LEN = 4096

CHUNK = 64
N_META = 16
HEAD_DIM = 128
DA_HEADS = 4
DA_DK = HEAD_DIM
DA_DV = 2 * HEAD_DIM
GDN_HEADS = 4
GDN_DK = HEAD_DIM
GDN_DV = HEAD_DIM
GDN_CONV = 4
ML_HEADS = 4
ML_DK = HEAD_DIM
ML_DV = HEAD_DIM
D_MIX = DA_HEADS * DA_DV + GDN_HEADS * GDN_DV + ML_HEADS * ML_DV
FFN_DIM = 5632
FFN_CONV = 3
NUM_BUCKETS = 32
MAX_DISTANCE = 128
Q_BLOCK = 128
RMS_EPS = 1e-6
NEG_INF = -1e30
DA_QK_COLS = DA_HEADS * 2 * DA_DK
DA_V_COLS = DA_HEADS * DA_DV
GDN_QKV_COLS = GDN_HEADS * (2 * GDN_DK + GDN_DV)
ML_QKV_COLS = ML_HEADS * (2 * ML_DK + ML_DV)
IN_SIZES = (DA_QK_COLS, DA_QK_COLS, DA_V_COLS, GDN_QKV_COLS, GDN_HEADS, GDN_HEADS, GDN_HEADS * GDN_DV, ML_QKV_COLS, ML_HEADS, ML_HEADS, ML_HEADS * ML_DV)
IN_COLS = sum(IN_SIZES)

kernel_name = 'hybrid_diffattn_gdn_mlstm_stream_step'


def rmsnorm(x, w):
    xf = x.astype(jnp.float32)
    y = xf * lax.rsqrt(jnp.mean(xf * xf, axis=-1, keepdims=True) + RMS_EPS)
    return (y * w.astype(jnp.float32)).astype(x.dtype)


def l2norm(x):
    xf = x.astype(jnp.float32)
    return xf * lax.rsqrt(jnp.sum(xf * xf, axis=-1, keepdims=True) + RMS_EPS)


def split_cols(z):
    parts, start = [], 0
    for size in IN_SIZES:
        parts.append(z[..., start:start + size])
        start += size
    return parts


def causal_dwconv(x, prev, w):
    width = w.shape[0]
    L = x.shape[1]
    xp = jnp.concatenate([prev.astype(x.dtype), x], axis=1)
    y = sum(xp[:, i:i + L] * w[i].astype(x.dtype) for i in range(width))
    return y, xp[:, xp.shape[1] - (width - 1):]


def prompt_chunk_ids(pos):
    return jnp.where(pos < N_META, 0, 1 + (pos - N_META) // CHUNK)


def t5_bucket(rel):
    half = NUM_BUCKETS // 2
    exact = half // 2
    n = jnp.abs(rel)
    nf = jnp.maximum(n, 1).astype(jnp.float32)
    large = exact + (jnp.log(nf / exact) / math.log(MAX_DISTANCE / exact) * (half - exact)).astype(jnp.int32)
    large = jnp.minimum(large, half - 1)
    return jnp.where(rel > 0, half, 0) + jnp.where(n < exact, n, large)


def diff_attend(q, q_pos, q_chunk, k, v, k_pos, k_chunk, rel_bias, lam):
    s = jnp.einsum('bqhmd,bkhmd->bhmqk', q, k).astype(jnp.float32) * (DA_DK ** -0.5)
    bias = rel_bias.astype(jnp.float32)[t5_bucket(k_pos[None, :] - q_pos[:, None])]
    s = s + jnp.transpose(bias, (2, 0, 1))[None, :, None]
    visible = k_chunk[None, :] <= q_chunk[:, None]
    s = jnp.where(visible[None, None, None], s, NEG_INF)
    p = jax.nn.softmax(s, axis=-1)
    a = p[:, :, 0] - lam * p[:, :, 1]
    return jnp.einsum('bhqk,bkhd->bqhd', a.astype(v.dtype), v)


def gdn_chunk(S, c):
    q, k, v, beta, g = c
    q, k, v = jnp.swapaxes(q, 1, 2), jnp.swapaxes(k, 1, 2), jnp.swapaxes(v, 1, 2)
    beta = jnp.swapaxes(beta, 1, 2)
    G = jnp.cumsum(jnp.swapaxes(g, 1, 2), axis=-1)
    L = q.shape[2]
    incl = jnp.tril(jnp.ones((L, L), dtype=bool))
    strict = jnp.tril(jnp.ones((L, L), dtype=bool), -1)
    diff = G[..., :, None] - G[..., None, :]
    dec = jnp.where(incl, jnp.exp(jnp.where(incl, diff, 0.0)), 0.0)
    kk = jnp.einsum('bhld,bhmd->bhlm', k, k)
    a_mat = jnp.eye(L, dtype=jnp.float32) + jnp.where(strict, beta[..., :, None] * kk * dec, 0.0)
    rhs = beta[..., None] * (v - jnp.exp(G)[..., None] * jnp.einsum('bhld,bhdv->bhlv', k, S))
    u = lax.linalg.triangular_solve(a_mat, rhs, left_side=True, lower=True, unit_diagonal=True)
    qk = jnp.einsum('bhld,bhmd->bhlm', q, k) * dec
    o = jnp.exp(G)[..., None] * jnp.einsum('bhld,bhdv->bhlv', q, S) + jnp.einsum('bhlm,bhmv->bhlv', qk, u)
    g_last = G[..., -1:]
    S_new = jnp.exp(g_last)[..., None] * S + jnp.einsum('bhl,bhld,bhlv->bhdv', jnp.exp(g_last - G), k, u)
    return S_new, jnp.swapaxes(o, 1, 2)


def mlstm_chunk(state, c):
    C, n, m = state
    q, k, v, ig, lf = c
    q, k, v = jnp.swapaxes(q, 1, 2), jnp.swapaxes(k, 1, 2), jnp.swapaxes(v, 1, 2)
    ig, lf = jnp.swapaxes(ig, 1, 2), jnp.swapaxes(lf, 1, 2)
    L = q.shape[2]
    incl = jnp.tril(jnp.ones((L, L), dtype=bool))
    b = jnp.cumsum(lf, axis=-1)
    D = jnp.where(incl, b[..., :, None] - b[..., None, :] + ig[..., None, :], NEG_INF)
    inter = b + m[..., None]
    m_t = jnp.maximum(inter, jnp.max(D, axis=-1))
    w_intra = jnp.exp(D - m_t[..., None])
    w_inter = jnp.exp(inter - m_t)
    qk = jnp.einsum('bhld,bhmd->bhlm', q, k) * w_intra
    num = w_inter[..., None] * jnp.einsum('bhld,bhdv->bhlv', q, C) + jnp.einsum('bhlm,bhmv->bhlv', qk, v)
    den = w_inter * jnp.einsum('bhld,bhd->bhl', q, n) + jnp.sum(qk, axis=-1)
    h = num / jnp.maximum(jnp.abs(den), jnp.exp(-m_t))[..., None]
    m_new = m_t[..., -1]
    w_fin = jnp.exp(b[..., -1:] - b + ig - m_new[..., None])
    dec0 = jnp.exp(b[..., -1] + m - m_new)
    C_new = dec0[..., None, None] * C + jnp.einsum('bhl,bhld,bhlv->bhdv', w_fin, k, v)
    n_new = dec0[..., None] * n + jnp.einsum('bhl,bhld->bhd', w_fin, k)
    return (C_new, n_new, m_new), jnp.swapaxes(h, 1, 2)


def run_chunks(step, state, xs, lead):
    state, out_head = step(state, tuple(a[:, :lead] for a in xs))
    rest = xs[0].shape[1] - lead
    if rest == 0:
        return state, out_head
    n_blk = rest // CHUNK
    blocks = tuple(jnp.swapaxes(a[:, lead:].reshape((a.shape[0], n_blk, CHUNK) + a.shape[2:]), 0, 1) for a in xs)
    state, out_rest = lax.scan(step, state, blocks)
    out_rest = jnp.swapaxes(out_rest, 0, 1)
    out_rest = out_rest.reshape((out_rest.shape[0], rest) + out_rest.shape[3:])
    return state, jnp.concatenate([out_head, out_rest], axis=1)


def mixer(x, lp, rel_bias, lam_init, past):
    f32 = jnp.float32
    B, L, _ = x.shape
    z = rmsnorm(x, lp['norm_mix']) @ lp['w_in']
    da_q, da_k, da_v, g_qkv, g_b, g_a, g_gate, m_qkv, m_i, m_f, m_o = split_cols(z)
    q = rmsnorm(da_q.reshape(B, L, DA_HEADS, 2, DA_DK), lp['da_q_norm'])
    k = rmsnorm(da_k.reshape(B, L, DA_HEADS, 2, DA_DK), lp['da_k_norm'])
    v = da_v.reshape(B, L, DA_HEADS, DA_DV)
    lam = (jnp.exp(jnp.sum(lp['da_lq1'].astype(f32) * lp['da_lk1'].astype(f32)))
           - jnp.exp(jnp.sum(lp['da_lq2'].astype(f32) * lp['da_lk2'].astype(f32))) + lam_init)
    if past is None:
        pos = jnp.arange(L)
        chunk = prompt_chunk_ids(pos)
        n_blk = -(-L // Q_BLOCK)
        L_pad = n_blk * Q_BLOCK
        q_pad = jnp.pad(q, ((0, 0), (0, L_pad - L), (0, 0), (0, 0), (0, 0)))
        q_blocks = jnp.swapaxes(q_pad.reshape(B, n_blk, Q_BLOCK, DA_HEADS, 2, DA_DK), 0, 1)
        pos_q = jnp.arange(L_pad)
        blocks = (q_blocks, pos_q.reshape(n_blk, Q_BLOCK), prompt_chunk_ids(pos_q).reshape(n_blk, Q_BLOCK))
        o_da = lax.map(lambda blk: diff_attend(blk[0], blk[1], blk[2], k, v, pos, chunk, rel_bias, lam), blocks)
        o_da = jnp.swapaxes(o_da, 0, 1).reshape(B, L_pad, DA_HEADS, DA_DV)[:, :L]
        conv_prev = jnp.zeros((B, GDN_CONV - 1, GDN_QKV_COLS), x.dtype)
        S0 = jnp.zeros((B, GDN_HEADS, GDN_DK, GDN_DV), f32)
        C0 = jnp.zeros((B, ML_HEADS, ML_DK, ML_DV), f32)
        n0 = jnp.zeros((B, ML_HEADS, ML_DK), f32)
        m0 = jnp.zeros((B, ML_HEADS), f32)
        lead = N_META
    else:
        cache_k, cache_v, conv_prev, S0, C0, n0, m0 = past
        P = cache_k.shape[1]
        k_all = jnp.concatenate([cache_k.astype(k.dtype), k], axis=1)
        v_all = jnp.concatenate([cache_v.astype(v.dtype), v], axis=1)
        k_pos = jnp.arange(P + L)
        k_chunk = (k_pos >= P).astype(jnp.int32)
        o_da = diff_attend(q, P + jnp.arange(L), jnp.ones((L,), jnp.int32), k_all, v_all, k_pos, k_chunk, rel_bias, lam)
        S0, C0, n0, m0 = S0.astype(f32), C0.astype(f32), n0.astype(f32), m0.astype(f32)
        lead = L
    o_da = rmsnorm(o_da, lp['da_out_norm']).astype(f32) * (1.0 - lam_init)
    gc, conv_state = causal_dwconv(g_qkv, conv_prev, lp['gdn_conv_w'])
    gc = jax.nn.silu(gc.astype(f32))
    nqk = GDN_HEADS * GDN_DK
    gq = l2norm(gc[..., :nqk].reshape(B, L, GDN_HEADS, GDN_DK)) * (GDN_DK ** -0.5)
    gk = l2norm(gc[..., nqk:2 * nqk].reshape(B, L, GDN_HEADS, GDN_DK))
    gv = gc[..., 2 * nqk:].reshape(B, L, GDN_HEADS, GDN_DV)
    beta = jax.nn.sigmoid(g_b.astype(f32))
    g_log = -jnp.exp(lp['gdn_A_log'].astype(f32)) * jax.nn.softplus(g_a.astype(f32) + lp['gdn_dt_bias'].astype(f32))
    S, o_g = run_chunks(gdn_chunk, S0, (gq, gk, gv, beta, g_log), lead)
    o_g = rmsnorm(o_g, lp['gdn_out_norm']) * jax.nn.silu(g_gate.astype(f32).reshape(B, L, GDN_HEADS, GDN_DV))
    mf = m_qkv.astype(f32)
    mqk = ML_HEADS * ML_DK
    mq = mf[..., :mqk].reshape(B, L, ML_HEADS, ML_DK)
    mk = mf[..., mqk:2 * mqk].reshape(B, L, ML_HEADS, ML_DK) * (ML_DK ** -0.5)
    mv = mf[..., 2 * mqk:].reshape(B, L, ML_HEADS, ML_DV)
    i_pre = m_i.astype(f32) + lp['ml_i_bias'].astype(f32)
    log_f = jax.nn.log_sigmoid(m_f.astype(f32) + lp['ml_f_bias'].astype(f32))
    (C, n, m), o_m = run_chunks(mlstm_chunk, (C0, n0, m0), (mq, mk, mv, i_pre, log_f), lead)
    o_m = rmsnorm(o_m, lp['ml_out_norm']) * jax.nn.sigmoid(m_o.astype(f32).reshape(B, L, ML_HEADS, ML_DV))
    mix = jnp.concatenate([o_da.reshape(B, L, DA_HEADS * DA_DV), o_g.reshape(B, L, GDN_HEADS * GDN_DV),
                           o_m.reshape(B, L, ML_HEADS * ML_DV)], axis=-1).astype(x.dtype)
    return x + mix @ lp['w_out'], (k, v, conv_state, S, C, n, m)


def conv_ffn(x, norm_w, w_up, conv_w, w_down, prev):
    up = rmsnorm(x, norm_w) @ w_up
    gate, val = up[..., :FFN_DIM], up[..., FFN_DIM:]
    gate, new_prev = causal_dwconv(gate, prev, conv_w)
    return x + (jax.nn.silu(gate) * val) @ w_down, new_prev


def setup_inputs(seed: int = 0) -> dict:
    key = jax.random.key(seed)
    ks = iter(jax.random.split(key, 48))
    f32 = jnp.float32

    def nrm(shape, scale=1.0):
        return jax.random.normal(next(ks), shape, f32) * scale

    def gain(shape):
        return 1.0 + 0.02 * jax.random.normal(next(ks), shape, f32)

    dt = jnp.exp(jax.random.uniform(next(ks), (DEPTH, GDN_HEADS), f32, math.log(1e-3), math.log(1e-1)))
    return {
        'x_prompt': nrm((BATCH, SEQ, D_MODEL)),
        'x_sample': nrm((DEC_BATCH, DEC_SEQ, D_MODEL)),
        'cache_attn_k': nrm((DEPTH, DEC_BATCH, PAST_LEN, DA_HEADS, 2, DA_DK)),
        'cache_attn_v': nrm((DEPTH, DEC_BATCH, PAST_LEN, DA_HEADS, DA_DV)),
        'state_gdn_conv': nrm((DEPTH, DEC_BATCH, GDN_CONV - 1, GDN_QKV_COLS)),
        'state_gdn_S': nrm((DEPTH, DEC_BATCH, GDN_HEADS, GDN_DK, GDN_DV), GDN_DK ** -0.5),
        'state_mlstm_C': nrm((DEPTH, DEC_BATCH, ML_HEADS, ML_DK, ML_DV), 0.1),
        'state_mlstm_n': nrm((DEPTH, DEC_BATCH, ML_HEADS, ML_DK), 0.1),
        'state_mlstm_m': nrm((DEPTH, DEC_BATCH, ML_HEADS)),
        'state_ffn_conv': nrm((DEPTH, DEC_BATCH, FFN_CONV - 1, FFN_DIM)),
        'meta_tokens': nrm((N_META, D_MODEL)),
        'rel_bias': nrm((NUM_BUCKETS, DA_HEADS), 0.5),
        'norm_mix': gain((DEPTH, D_MODEL)),
        'norm_ffn': gain((DEPTH, D_MODEL)),
        'w_in': nrm((DEPTH, D_MODEL, IN_COLS), D_MODEL ** -0.5),
        'w_out': nrm((DEPTH, D_MIX, D_MODEL), D_MIX ** -0.5),
        'da_q_norm': gain((DEPTH, DA_DK)),
        'da_k_norm': gain((DEPTH, DA_DK)),
        'da_lq1': nrm((DEPTH, DA_DK), 0.1),
        'da_lk1': nrm((DEPTH, DA_DK), 0.1),
        'da_lq2': nrm((DEPTH, DA_DK), 0.1),
        'da_lk2': nrm((DEPTH, DA_DK), 0.1),
        'da_out_norm': gain((DEPTH, DA_DV)),
        'gdn_conv_w': nrm((DEPTH, GDN_CONV, GDN_QKV_COLS), GDN_CONV ** -0.5),
        'gdn_A_log': jnp.log(jax.random.uniform(next(ks), (DEPTH, GDN_HEADS), f32, 1.0, 16.0)),
        'gdn_dt_bias': dt + jnp.log(-jnp.expm1(-dt)),
        'gdn_out_norm': gain((DEPTH, GDN_DV)),
        'ml_i_bias': nrm((DEPTH, ML_HEADS), 0.1),
        'ml_f_bias': 3.0 + 3.0 * jax.random.uniform(next(ks), (DEPTH, ML_HEADS), f32),
        'ml_out_norm': gain((DEPTH, ML_DV)),
        'ffn_w_up': nrm((DEPTH, D_MODEL, 2 * FFN_DIM), D_MODEL ** -0.5),
        'ffn_conv_w': nrm((DEPTH, FFN_CONV, FFN_DIM), FFN_CONV ** -0.5),
        'ffn_w_down': nrm((DEPTH, FFN_DIM, D_MODEL), FFN_DIM ** -0.5),
    }


def reference(x_prompt, x_sample, cache_attn_k, cache_attn_v, state_gdn_conv, state_gdn_S, state_mlstm_C,
              state_mlstm_n, state_mlstm_m, state_ffn_conv, meta_tokens, rel_bias, norm_mix, norm_ffn, w_in, w_out,
              da_q_norm, da_k_norm, da_lq1, da_lk1, da_lq2, da_lk2, da_out_norm, gdn_conv_w, gdn_A_log, gdn_dt_bias,
              gdn_out_norm, ml_i_bias, ml_f_bias, ml_out_norm, ffn_w_up, ffn_conv_w, ffn_w_down):
    B = x_prompt.shape[0]
    meta = jnp.broadcast_to(meta_tokens.astype(x_prompt.dtype)[None], (B, N_META, D_MODEL))
    xp = jnp.concatenate([meta, x_prompt], axis=1)
    xs = x_sample
    p_states = [[] for _ in range(8)]
    s_states = [[] for _ in range(8)]
    for l in range(DEPTH):
        lp = {'norm_mix': norm_mix[l], 'w_in': w_in[l], 'w_out': w_out[l], 'da_q_norm': da_q_norm[l],
              'da_k_norm': da_k_norm[l], 'da_lq1': da_lq1[l], 'da_lk1': da_lk1[l], 'da_lq2': da_lq2[l],
              'da_lk2': da_lk2[l], 'da_out_norm': da_out_norm[l], 'gdn_conv_w': gdn_conv_w[l],
              'gdn_A_log': gdn_A_log[l], 'gdn_dt_bias': gdn_dt_bias[l], 'gdn_out_norm': gdn_out_norm[l],
              'ml_i_bias': ml_i_bias[l], 'ml_f_bias': ml_f_bias[l], 'ml_out_norm': ml_out_norm[l]}
        lam_init = 0.8 - 0.6 * math.exp(-0.3 * l)
        xp, st_p = mixer(xp, lp, rel_bias, lam_init, None)
        xs, st_s = mixer(xs, lp, rel_bias, lam_init,
                         (cache_attn_k[l], cache_attn_v[l], state_gdn_conv[l], state_gdn_S[l],
                          state_mlstm_C[l], state_mlstm_n[l], state_mlstm_m[l]))
        xp, fp = conv_ffn(xp, norm_ffn[l], ffn_w_up[l], ffn_conv_w[l], ffn_w_down[l],
                          jnp.zeros((B, FFN_CONV - 1, FFN_DIM), xp.dtype))
        xs, fs = conv_ffn(xs, norm_ffn[l], ffn_w_up[l], ffn_conv_w[l], ffn_w_down[l], state_ffn_conv[l])
        for i, a in enumerate(st_p + (fp,)):
            p_states[i].append(a)
        for i, a in enumerate(st_s + (fs,)):
            s_states[i].append(a)
    k_p, v_p, gconv_p, gS_p, mC_p, mn_p, mm_p, fconv_p = [jnp.stack(s) for s in p_states]
    k_s, v_s, gconv_s, gS_s, mC_s, mn_s, mm_s, fconv_s = [jnp.stack(s) for s in s_states]
    y_prompt = xp[:, N_META:]
    return (y_prompt, xs, k_p, v_p, gconv_p, gS_p, mC_p, mn_p, mm_p, fconv_p,
            k_s, v_s, gconv_s, gS_s, mC_s, mn_s, mm_s, fconv_s)
```

```python
import functools
import math

import jax
import jax.numpy as jnp
from jax import lax
from jax.experimental import pallas as pl
from jax.experimental.pallas import tpu as pltpu

F32 = jnp.float32
BF16 = jnp.bfloat16

HEADS = 4
HD = 128
DA_DV = 2 * HD
CHUNK = 64
GDN_CONV = 4
FFN_CONV = 3
NUM_BUCKETS = 32
MAX_DISTANCE = 128
RMS_EPS = 1e-6
NEG_INF = -1e30
HALO = 8

QKV_COLS = 3 * HEADS * HD
Z_COLS = 3 * HEADS * DA_DV + 2 * QKV_COLS + 2 * HEADS * HD
SCAL_COLS = 128

VMEM_LIMIT = 56 * 1024 * 1024
HI = lax.Precision.HIGHEST


def _pick(n, cands):
    for c in cands:
        if n % c == 0:
            return c
    raise ValueError(f"no tile for {n} in {cands}")


def _params(sem):
    return pltpu.CompilerParams(dimension_semantics=sem, vmem_limit_bytes=VMEM_LIMIT)


def _sigmoid(x):
    return 1.0 / (1.0 + jnp.exp(-x))


def _softplus(x):
    return jnp.maximum(x, 0.0) + jnp.log(1.0 + jnp.exp(-jnp.abs(x)))


def _hdot(a, b):
    return jnp.dot(a, b, precision=HI, preferred_element_type=F32)


def _hdot_nt(a, b):
    return lax.dot_general(a, b, (((1,), (1,)), ((), ())), precision=HI, preferred_element_type=F32)


def _hdot_tn(a, b):
    return lax.dot_general(a, b, (((0,), (0,)), ((), ())), precision=HI, preferred_element_type=F32)


def _dot_nt(a, b):
    return lax.dot_general(a, b, (((1,), (1,)), ((), ())), preferred_element_type=F32)


def _rms_matmul_kernel(x_ref, g_ref, w_ref, *rest, has_extra):
    if has_extra:
        ws_ref, o_ref, os_ref, xn_ref = rest
    else:
        o_ref, xn_ref = rest

    @pl.when(pl.program_id(1) == 0)
    def _():
        x = x_ref[...]
        ms = jnp.mean(x * x, axis=-1, keepdims=True)
        xn = (x * lax.rsqrt(ms + RMS_EPS) * g_ref[...]).astype(BF16)
        xn_ref[...] = xn
        if has_extra:
            os_ref[...] = jnp.dot(xn, ws_ref[...], preferred_element_type=F32)

    o_ref[...] = jnp.dot(xn_ref[...], w_ref[...], preferred_element_type=F32).astype(o_ref.dtype)


def rms_matmul(x, gain, w, w_extra=None, name="rms_matmul"):
    M, K = x.shape
    N = w.shape[1]
    tm = _pick(M, (1024, 512, 256, 128, 64, 32, 16, 8))
    tn = _pick(N, (512, 256, 128))
    has_extra = w_extra is not None
    in_specs = [pl.BlockSpec((tm, K), lambda i, j: (i, 0)),
                pl.BlockSpec((1, K), lambda i, j: (0, 0)),
                pl.BlockSpec((K, tn), lambda i, j: (0, j))]
    out_specs = [pl.BlockSpec((tm, tn), lambda i, j: (i, j))]
    out_shape = [jax.ShapeDtypeStruct((M, N), F32)]
    args = [x, gain.reshape(1, K), w]
    if has_extra:
        ne = w_extra.shape[1]
        in_specs.append(pl.BlockSpec((K, ne), lambda i, j: (0, 0)))
        out_specs.append(pl.BlockSpec((tm, ne), lambda i, j: (i, 0)))
        out_shape.append(jax.ShapeDtypeStruct((M, ne), F32))
        args.append(w_extra)
    outs = pl.pallas_call(
        functools.partial(_rms_matmul_kernel, has_extra=has_extra),
        grid=(M // tm, N // tn),
        in_specs=in_specs, out_specs=out_specs, out_shape=out_shape,
        scratch_shapes=[pltpu.VMEM((tm, K), BF16)],
        compiler_params=_params(("parallel", "arbitrary")),
        name=name,
    )(*args)
    return outs if has_extra else outs[0]


def _matmul_res_kernel(*refs, n_a):
    a_refs, w_refs = refs[:n_a], refs[n_a:2 * n_a]
    res_ref, o_ref = refs[2 * n_a], refs[2 * n_a + 1]
    acc = res_ref[...]
    for a_ref, w_ref in zip(a_refs, w_refs):
        acc = acc + jnp.dot(a_ref[...], w_ref[...], preferred_element_type=F32)
    o_ref[...] = acc


def matmul_res(a_list, w, res, name="matmul_res"):
    M, N = res.shape
    ktot = sum(a.shape[1] for a in a_list)
    tm = _pick(M, (1024, 512, 256, 128, 64, 32, 16, 8) if ktot <= 4096 else (512, 256, 128, 64, 32, 16, 8))
    tn = _pick(N, (512, 256, 128))
    in_specs, off = [], 0
    for a in a_list:
        in_specs.append(pl.BlockSpec((tm, a.shape[1]), lambda i, j: (i, 0)))
    for a in a_list:
        k = a.shape[1]
        assert off % k == 0
        in_specs.append(pl.BlockSpec((k, tn), lambda i, j, _o=off // k: (_o, j)))
        off += k
    in_specs.append(pl.BlockSpec((tm, tn), lambda i, j: (i, j)))
    return pl.pallas_call(
        functools.partial(_matmul_res_kernel, n_a=len(a_list)),
        grid=(M // tm, N // tn),
        in_specs=in_specs,
        out_specs=pl.BlockSpec((tm, tn), lambda i, j: (i, j)),
        out_shape=jax.ShapeDtypeStruct((M, N), F32),
        compiler_params=_params(("parallel", "arbitrary")),
        name=name,
    )(*a_list, *([w] * len(a_list)), res)


def _qk_prep_kernel(q_ref, k_ref, v_ref, gq_ref, gk_ref, qn_ref, kn32_ref, kn16_ref, v16_ref):
    gq = gq_ref[...]
    gk = gk_ref[...]
    for g in range(2 * HEADS):
        sl = slice(g * HD, (g + 1) * HD)
        q = q_ref[:, sl]
        k = k_ref[:, sl]
        qn = q * lax.rsqrt(jnp.mean(q * q, axis=-1, keepdims=True) + RMS_EPS) * gq
        kn = k * lax.rsqrt(jnp.mean(k * k, axis=-1, keepdims=True) + RMS_EPS) * gk
        qn_ref[:, sl] = (qn * (HD ** -0.5)).astype(BF16)
        kn32_ref[:, sl] = kn
        kn16_ref[:, sl] = kn.astype(BF16)
    v16_ref[...] = v_ref[...].astype(BF16)


def qk_prep(z, gq, gk):
    M = z.shape[0]
    W = 2 * HEADS * HD
    tm = _pick(M, (512, 256, 128, 64, 32, 16, 8))
    col = lambda c: pl.BlockSpec((tm, W), lambda i, _c=c: (i, _c))
    vec = pl.BlockSpec((1, HD), lambda i: (0, 0))
    row = pl.BlockSpec((tm, W), lambda i: (i, 0))
    return pl.pallas_call(
        _qk_prep_kernel,
        grid=(M // tm,),
        in_specs=[col(0), col(1), col(2), vec, vec],
        out_specs=[row, row, row, row],
        out_shape=[jax.ShapeDtypeStruct((M, W), BF16), jax.ShapeDtypeStruct((M, W), F32),
                   jax.ShapeDtypeStruct((M, W), BF16), jax.ShapeDtypeStruct((M, W), BF16)],
        compiler_params=_params(("parallel",)),
        name="qk_prep",
    )(z, z, z, gq.reshape(1, HD), gk.reshape(1, HD))


def _softmax_update(mi, s, v, m_ref, l_ref, acc_ref):
    m_prev = m_ref[mi]
    m_new = jnp.maximum(m_prev, jnp.max(s, axis=-1, keepdims=True))
    alpha = jnp.exp(m_prev - m_new)
    p = jnp.exp(s - m_new)
    l_ref[mi] = alpha * l_ref[mi] + jnp.sum(p, axis=-1, keepdims=True)
    acc_ref[mi] = alpha * acc_ref[mi] + jnp.dot(p.astype(BF16), v, preferred_element_type=F32)
    m_ref[mi] = m_new


def _attn_init(m_ref, l_ref, acc_ref):
    m_ref[...] = jnp.full(m_ref.shape, -jnp.inf, F32)
    l_ref[...] = jnp.zeros(l_ref.shape, F32)
    acc_ref[...] = jnp.zeros(acc_ref.shape, F32)


def _attn_tile(q, k, v, bias, m_ref, l_ref, acc_ref):
    for mi in range(2):
        sl = slice(mi * HD, (mi + 1) * HD)
        s = _dot_nt(q[:, sl], k[:, sl]) + bias
        _softmax_update(mi, s, v, m_ref, l_ref, acc_ref)


def _attn_finish(lam, out_scale, g_ref, o_ref, l_ref, acc_ref):
    o = acc_ref[0] / l_ref[0] - lam * (acc_ref[1] / l_ref[1])
    on = o * lax.rsqrt(jnp.mean(o * o, axis=-1, keepdims=True) + RMS_EPS) * g_ref[...]
    o_ref[...] = (on * out_scale).astype(o_ref.dtype)


def _attn_scratch(tq):
    return [pltpu.VMEM((2, tq, 1), F32), pltpu.VMEM((2, tq, 1), F32), pltpu.VMEM((2, tq, DA_DV), F32)]


def _attn_big_kernel(qi_ref, kj_ref, lam_ref, q_ref, k_ref, v_ref, km_ref, vm_ref, bias_ref, bmeta_ref,
                     g_ref, o_ref, m_ref, l_ref, acc_ref, *, out_scale):
    step = pl.program_id(2)
    i = qi_ref[step]
    j = kj_ref[step]
    q = q_ref[...]

    @pl.when(j == 0)
    def _():
        _attn_init(m_ref, l_ref, acc_ref)
        _attn_tile(q, km_ref[...], vm_ref[...], bmeta_ref[0, 0], m_ref, l_ref, acc_ref)

    _attn_tile(q, k_ref[...], v_ref[...], bias_ref[0, 0], m_ref, l_ref, acc_ref)

    @pl.when(j == i)
    def _():
        _attn_finish(lam_ref[0, 0], out_scale, g_ref, o_ref, l_ref, acc_ref)


def attn_big(qn, kn16, v16, kmeta, vmeta, bias, bmeta, lam, gain, out_scale, B, T, tq):
    nq = T // tq
    mp = kmeta.shape[0] // B
    pairs = [(i, j) for i in range(nq) for j in range(i + 1)]
    qi = jnp.asarray([p[0] for p in pairs], jnp.int32)
    kj = jnp.asarray([p[1] for p in pairs], jnp.int32)
    grid_spec = pltpu.PrefetchScalarGridSpec(
        num_scalar_prefetch=2,
        grid=(B, HEADS, len(pairs)),
        in_specs=[
            pl.BlockSpec(memory_space=pltpu.SMEM),
            pl.BlockSpec((tq, DA_DV), lambda b, h, s, qi, kj: (b * nq + qi[s], h)),
            pl.BlockSpec((tq, DA_DV), lambda b, h, s, qi, kj: (b * nq + kj[s], h)),
            pl.BlockSpec((tq, DA_DV), lambda b, h, s, qi, kj: (b * nq + kj[s], h)),
            pl.BlockSpec((mp, DA_DV), lambda b, h, s, qi, kj: (b, h)),
            pl.BlockSpec((mp, DA_DV), lambda b, h, s, qi, kj: (b, h)),
            pl.BlockSpec((1, 1, tq, tq), lambda b, h, s, qi, kj: (h, jnp.maximum(kj[s] - qi[s] + 2, 0), 0, 0)),
            pl.BlockSpec((1, 1, tq, mp), lambda b, h, s, qi, kj: (h, jnp.minimum(qi[s], 1), 0, 0)),
            pl.BlockSpec((1, DA_DV), lambda b, h, s, qi, kj: (0, 0)),
        ],
        out_specs=pl.BlockSpec((tq, DA_DV), lambda b, h, s, qi, kj: (b * nq + qi[s], h)),
        scratch_shapes=_attn_scratch(tq),
    )
    return pl.pallas_call(
        functools.partial(_attn_big_kernel, out_scale=out_scale),
        grid_spec=grid_spec,
        out_shape=jax.ShapeDtypeStruct((B * T, HEADS * DA_DV), BF16),
        compiler_params=_params(("parallel", "parallel", "arbitrary")),
        name="attn_prompt",
    )(qi, kj, lam, qn, kn16, v16, kmeta, vmeta, bias, bmeta, gain.reshape(1, DA_DV))


def _attn_small_kernel(lam_ref, q_ref, *rest, nc, out_scale):
    if nc > 0:
        ck_ref, cv_ref, bc_ref, k_ref, v_ref, bn_ref, g_ref, o_ref, m_ref, l_ref, acc_ref = rest
    else:
        k_ref, v_ref, bn_ref, g_ref, o_ref, m_ref, l_ref, acc_ref = rest
    j = pl.program_id(2)
    q = q_ref[...]

    @pl.when(j == 0)
    def _():
        _attn_init(m_ref, l_ref, acc_ref)

    if nc > 0:
        @pl.when(j < nc)
        def _():
            _attn_tile(q, ck_ref[...].astype(BF16), cv_ref[...].astype(BF16), bc_ref[0, 0], m_ref, l_ref, acc_ref)

    @pl.when(j == nc)
    def _():
        _attn_tile(q, k_ref[...], v_ref[...], bn_ref[0], m_ref, l_ref, acc_ref)
        _attn_finish(lam_ref[0, 0], out_scale, g_ref, o_ref, l_ref, acc_ref)


def attn_small(qn, kn16, v16, cache, bias_new, lam, gain, out_scale, B, L):
    W = HEADS * DA_DV
    blk = pl.BlockSpec((L, DA_DV), lambda b, h, j: (b, h))
    in_specs = [pl.BlockSpec(memory_space=pltpu.SMEM), blk]
    args = [lam, qn]
    nc = 0
    if cache is not None:
        ck, cv, bc, tk = cache
        nc = ck.shape[0] // B // tk
        cspec = pl.BlockSpec((tk, DA_DV), lambda b, h, j: (b * nc + jnp.minimum(j, nc - 1), h))
        in_specs += [cspec, cspec,
                     pl.BlockSpec((1, 1, L, tk), lambda b, h, j: (h, jnp.where(j >= nc - 1, 1, 0), 0, 0))]
        args += [ck, cv, bc]
    in_specs += [blk, blk, pl.BlockSpec((1, L, L), lambda b, h, j: (h, 0, 0)),
                 pl.BlockSpec((1, DA_DV), lambda b, h, j: (0, 0))]
    args += [kn16, v16, bias_new, gain.reshape(1, DA_DV)]
    return pl.pallas_call(
        functools.partial(_attn_small_kernel, nc=nc, out_scale=out_scale),
        grid=(B, HEADS, nc + 1),
        in_specs=in_specs,
        out_specs=blk,
        out_shape=jax.ShapeDtypeStruct((B * L, W), BF16),
        scratch_shapes=_attn_scratch(L),
        compiler_params=_params(("parallel", "parallel", "arbitrary")),
        name="attn_block",
    )(*args)


def _chunk_masks(L):
    row = lax.broadcasted_iota(jnp.int32, (L, L), 0)
    col = lax.broadcasted_iota(jnp.int32, (L, L), 1)
    return row, col


def _gdn_kernel(nega_ref, dtb_ref, x_ref, gate_ref, sc_ref, prev_ref, cw_ref, s0_ref, gn_ref,
                o_ref, sout_ref, xbuf, s_scr, *, L):
    i = pl.program_id(1)

    @pl.when(i == 0)
    def _():
        xbuf[0:HALO, :] = prev_ref[0]
        s_scr[...] = s0_ref[0]

    xbuf[HALO:HALO + L, :] = x_ref[...]
    cw = cw_ref[...]
    y = xbuf[HALO - 3:HALO - 3 + L, :] * cw[0:1, :]
    for t in range(1, GDN_CONV):
        y = y + xbuf[HALO - 3 + t:HALO - 3 + t + L, :] * cw[t:t + 1, :]
    tail = xbuf[L:L + HALO, :]
    xbuf[0:HALO, :] = tail
    gc = y * _sigmoid(y)

    row, col = _chunk_masks(L)
    incl = col <= row
    strict = col < row
    eye = (col == row).astype(F32)
    tril = incl.astype(F32)
    ones = jnp.ones((L, L), F32)
    sc = sc_ref[...]
    nqk = HEADS * HD

    for h in range(HEADS):
        sl = slice(h * HD, (h + 1) * HD)
        xq = gc[:, h * HD:(h + 1) * HD]
        xk = gc[:, nqk + h * HD:nqk + (h + 1) * HD]
        xv = gc[:, 2 * nqk + h * HD:2 * nqk + (h + 1) * HD]
        qn = xq * lax.rsqrt(jnp.sum(xq * xq, axis=-1, keepdims=True) + RMS_EPS) * (HD ** -0.5)
        kn = xk * lax.rsqrt(jnp.sum(xk * xk, axis=-1, keepdims=True) + RMS_EPS)
        beta = _sigmoid(sc[:, h:h + 1])
        g = nega_ref[h] * _softplus(sc[:, HEADS + h:HEADS + h + 1] + dtb_ref[h])
        gb = jnp.broadcast_to(g, (L, L))
        g_col = _hdot(tril, gb)
        g_row = _hdot(ones, jnp.where(row <= col, gb, 0.0))
        dec = jnp.where(incl, jnp.exp(jnp.where(incl, g_col - g_row, 0.0)), 0.0)
        kk = _hdot_nt(kn, kn)
        nmat = jnp.where(strict, beta * kk * dec, 0.0)
        tinv = eye - nmat
        pw = _hdot(nmat, nmat)
        p = 2
        while True:
            tinv = tinv + _hdot(tinv, pw)
            p *= 2
            if p >= L:
                break
            pw = _hdot(pw, pw)
        gcol = g_col[:, 0:1]
        eg = jnp.exp(gcol)
        s_old = s_scr[h]
        w_mat = _hdot(tinv, beta * eg * kn)
        u0 = _hdot(tinv, beta * xv)
        u = u0 - _hdot(w_mat, s_old)
        qk = _hdot_nt(qn, kn) * dec
        o = eg * _hdot(qn, s_old) + _hdot(qk, u)
        g_last = g_col[L - 1:L, 0:1]
        s_scr[h] = jnp.exp(g_last) * s_old + _hdot_tn(kn * jnp.exp(g_last - gcol), u)
        on = o * lax.rsqrt(jnp.mean(o * o, axis=-1, keepdims=True) + RMS_EPS) * gn_ref[...]
        gt = gate_ref[:, sl]
        o_ref[:, sl] = (on * (gt * _sigmoid(gt))).astype(o_ref.dtype)

    @pl.when(i == pl.num_programs(1) - 1)
    def _():
        sout_ref[0] = s_scr[...]


def gdn(z, sc, prev8, conv_w, neg_a, dt_bias, s0, gn, B, T, L):
    nt = T // L
    smem = pl.BlockSpec(memory_space=pltpu.SMEM)
    st = pl.BlockSpec((1, HEADS, HD, HD), lambda b, i: (b, 0, 0, 0))
    return pl.pallas_call(
        functools.partial(_gdn_kernel, L=L),
        grid=(B, nt),
        in_specs=[smem, smem,
                  pl.BlockSpec((L, QKV_COLS), lambda b, i: (b * nt + i, 2)),
                  pl.BlockSpec((L, HEADS * HD), lambda b, i: (b * nt + i, 12)),
                  pl.BlockSpec((L, SCAL_COLS), lambda b, i: (b * nt + i, 0)),
                  pl.BlockSpec((1, HALO, QKV_COLS), lambda b, i: (b, 0, 0)),
                  pl.BlockSpec((GDN_CONV, QKV_COLS), lambda b, i: (0, 0)),
                  st,
                  pl.BlockSpec((1, HD), lambda b, i: (0, 0))],
        out_specs=[pl.BlockSpec((L, HEADS * HD), lambda b, i: (b * nt + i, 0)), st],
        out_shape=[jax.ShapeDtypeStruct((B * T, HEADS * HD), BF16),
                   jax.ShapeDtypeStruct((B, HEADS, HD, HD), F32)],
        scratch_shapes=[pltpu.VMEM((L + HALO, QKV_COLS), F32), pltpu.VMEM((HEADS, HD, HD), F32)],
        compiler_params=_params(("parallel", "arbitrary")),
        name="gdn_scan",
    )(neg_a, dt_bias, z, z, sc, prev8, conv_w, s0, gn.reshape(1, HD))


def _mlstm_kernel(ib_ref, fb_ref, x_ref, og_ref, sc_ref, c0_ref, n0_ref, m0_ref, gn_ref,
                  o_ref, cout_ref, nout_ref, mout_ref, c_scr, n_scr, m_scr, *, L):
    i = pl.program_id(1)

    @pl.when(i == 0)
    def _():
        c_scr[...] = c0_ref[0]
        n_scr[...] = n0_ref[0]
        m_scr[...] = m0_ref[0]

    row, col = _chunk_masks(L)
    incl = col <= row
    tril = incl.astype(F32)
    ones = jnp.ones((L, L), F32)
    x = x_ref[...]
    sc = sc_ref[...]
    nqk = HEADS * HD

    for h in range(HEADS):
        sl = slice(h * HD, (h + 1) * HD)
        q = x[:, h * HD:(h + 1) * HD]
        k = x[:, nqk + h * HD:nqk + (h + 1) * HD] * (HD ** -0.5)
        v = x[:, 2 * nqk + h * HD:2 * nqk + (h + 1) * HD]
        ig = sc[:, 2 * HEADS + h:2 * HEADS + h + 1] + ib_ref[h]
        lf = -_softplus(-(sc[:, 3 * HEADS + h:3 * HEADS + h + 1] + fb_ref[h]))
        lfb = jnp.broadcast_to(lf, (L, L))
        igb = jnp.broadcast_to(ig, (L, L))
        b_col = _hdot(tril, lfb)
        r_row = _hdot(ones, jnp.where(row == col, igb, 0.0) - jnp.where(row <= col, lfb, 0.0))
        dmat = jnp.where(incl, b_col + r_row, NEG_INF)
        bcol = b_col[:, 0:1]
        m_prev = m_scr[h:h + 1, 0:1]
        inter = bcol + m_prev
        m_t = jnp.maximum(inter, jnp.max(dmat, axis=-1, keepdims=True))
        w_intra = jnp.exp(dmat - m_t)
        w_inter = jnp.exp(inter - m_t)
        qk = _hdot_nt(q, k) * w_intra
        c_old = c_scr[h]
        n_old = n_scr[h:h + 1, :]
        num = w_inter * _hdot(q, c_old) + _hdot(qk, v)
        den = w_inter * jnp.sum(q * n_old, axis=-1, keepdims=True) + jnp.sum(qk, axis=-1, keepdims=True)
        hout = num / jnp.maximum(jnp.abs(den), jnp.exp(-m_t))
        m_new = m_t[L - 1:L, :]
        b_last = bcol[L - 1:L, :]
        w_fin = jnp.exp(b_last - bcol + ig - m_new)
        dec0 = jnp.exp(b_last + m_prev - m_new)
        kw = w_fin * k
        c_scr[h] = dec0 * c_old + _hdot_tn(kw, v)
        n_scr[h:h + 1, :] = dec0 * n_old + jnp.sum(kw, axis=0, keepdims=True)
        m_scr[h:h + 1, :] = jnp.broadcast_to(m_new, (1, HD))
        on = hout * lax.rsqrt(jnp.mean(hout * hout, axis=-1, keepdims=True) + RMS_EPS) * gn_ref[...]
        o_ref[:, sl] = (on * _sigmoid(og_ref[:, sl])).astype(o_ref.dtype)

    @pl.when(i == pl.num_programs(1) - 1)
    def _():
        cout_ref[0] = c_scr[...]
        nout_ref[0] = n_scr[...]
        mout_ref[0] = m_scr[...]


def mlstm(z, sc, i_bias, f_bias, c0, n0p, m0p, gn, B, T, L):
    nt = T // L
    smem = pl.BlockSpec(memory_space=pltpu.SMEM)
    st = pl.BlockSpec((1, HEADS, HD, HD), lambda b, i: (b, 0, 0, 0))
    vec = pl.BlockSpec((1, 8, HD), lambda b, i: (b, 0, 0))
    return pl.pallas_call(
        functools.partial(_mlstm_kernel, L=L),
        grid=(B, nt),
        in_specs=[smem, smem,
                  pl.BlockSpec((L, QKV_COLS), lambda b, i: (b * nt + i, 3)),
                  pl.BlockSpec((L, HEADS * HD), lambda b, i: (b * nt + i, 13)),
                  pl.BlockSpec((L, SCAL_COLS), lambda b, i: (b * nt + i, 0)),
                  st, vec, vec,
                  pl.BlockSpec((1, HD), lambda b, i: (0, 0))],
        out_specs=[pl.BlockSpec((L, HEADS * HD), lambda b, i: (b * nt + i, 0)), st, vec, vec],
        out_shape=[jax.ShapeDtypeStruct((B * T, HEADS * HD), BF16),
                   jax.ShapeDtypeStruct((B, HEADS, HD, HD), F32),
                   jax.ShapeDtypeStruct((B, 8, HD), F32),
                   jax.ShapeDtypeStruct((B, 8, HD), F32)],
        scratch_shapes=[pltpu.VMEM((HEADS, HD, HD), F32), pltpu.VMEM((8, HD), F32), pltpu.VMEM((8, HD), F32)],
        compiler_params=_params(("parallel", "arbitrary")),
        name="mlstm_scan",
    )(i_bias, f_bias, z, z, sc, c0, n0p, m0p, gn.reshape(1, HD))


def _ffn_act_kernel(gate_ref, val_ref, halo_ref, prev_ref, cw_ref, o_ref, buf, *, tm):
    i = pl.program_id(1)
    buf[0:HALO, :] = jnp.where(i == 0, prev_ref[0], halo_ref[...])
    buf[HALO:HALO + tm, :] = gate_ref[...]
    cw = cw_ref[...]
    y = buf[HALO - 2:HALO - 2 + tm, :] * cw[0:1, :]
    for t in range(1, FFN_CONV):
        y = y + buf[HALO - 2 + t:HALO - 2 + t + tm, :] * cw[t:t + 1, :]
    o_ref[...] = (y * _sigmoid(y) * val_ref[...]).astype(o_ref.dtype)


def ffn_act(up, prev8, conv_w, B, T):
    M = B * T
    Fd = conv_w.shape[1]
    tm = _pick(T, (512, 256, 128, 64, 32, 16, 8))
    tf = _pick(Fd, (512, 256, 128))
    nt, nf = T // tm, Fd // tf
    hb = tm // HALO
    return pl.pallas_call(
        functools.partial(_ffn_act_kernel, tm=tm),
        grid=(B, nt, nf),
        in_specs=[pl.BlockSpec((tm, tf), lambda b, i, c: (b * nt + i, c)),
                  pl.BlockSpec((tm, tf), lambda b, i, c: (b * nt + i, nf + c)),
                  pl.BlockSpec((HALO, tf), lambda b, i, c: (jnp.maximum((b * nt + i) * hb - 1, 0), c)),
                  pl.BlockSpec((1, HALO, tf), lambda b, i, c: (b, 0, c)),
                  pl.BlockSpec((FFN_CONV, tf), lambda b, i, c: (0, c))],
        out_specs=pl.BlockSpec((tm, tf), lambda b, i, c: (b * nt + i, c)),
        out_shape=jax.ShapeDtypeStruct((M, Fd), BF16),
        scratch_shapes=[pltpu.VMEM((tm + HALO, tf), F32)],
        compiler_params=_params(("parallel", "parallel", "parallel")),
        name="ffn_act",
    )(up, up, up, prev8, conv_w)


def _t5_bucket(rel):
    half = NUM_BUCKETS // 2
    exact = half // 2
    n = jnp.abs(rel)
    nf = jnp.maximum(n, 1).astype(F32)
    large = exact + (jnp.log(nf / exact) / math.log(MAX_DISTANCE / exact) * (half - exact)).astype(jnp.int32)
    large = jnp.minimum(large, half - 1)
    return jnp.where(rel > 0, half, 0) + jnp.where(n < exact, n, large)


def _rel_bias_tile(rel_bias, rel):
    return jnp.transpose(rel_bias.astype(F32)[_t5_bucket(rel)], (2, 0, 1))


def _pad_rows(a, rows):
    return jnp.pad(a, ((0, 0), (rows - a.shape[1], 0), (0, 0)))


def _layer(x, B, T, L, lw, init, attn_fn):
    z, sc = rms_matmul(x, lw["norm_mix"], lw["w_in"], lw["w_in_scal"], name="in_proj")
    qn, kn32, kn16, v16 = qk_prep(z, lw["da_q_norm"], lw["da_k_norm"])
    o_da = attn_fn(qn, kn16, v16)
    o_g, s_new = gdn(z, sc, _pad_rows(init["gconv"], HALO), lw["gdn_conv_w"], lw["gdn_neg_a"], lw["gdn_dt_bias"],
                     init["S"], lw["gdn_out_norm"], B, T, L)
    n0p = jnp.pad(init["n"], ((0, 0), (0, 8 - HEADS), (0, 0)))
    m0p = jnp.pad(jnp.broadcast_to(init["m"][:, :, None], (B, HEADS, HD)), ((0, 0), (0, 8 - HEADS), (0, 0)))
    o_m, c_new, n_new, m_new = mlstm(z, sc, lw["ml_i_bias"], lw["ml_f_bias"], init["C"], n0p, m0p,
                                     lw["ml_out_norm"], B, T, L)
    x1 = matmul_res([o_da, o_g, o_m], lw["w_out"], x, name="out_proj")
    up = rms_matmul(x1, lw["norm_ffn"], lw["w_up"], name="ffn_up")
    hmid = ffn_act(up, _pad_rows(init["fconv"], HALO), lw["ffn_conv_w"], B, T)
    x2 = matmul_res([hmid], lw["w_down"], x1, name="ffn_down")
    Fd = lw["ffn_conv_w"].shape[1]
    z3 = z.reshape(B, T, Z_COLS)
    st = {
        "k": kn32.reshape(B, T, HEADS, 2, HD),
        "v": z3[:, :, 2 * HEADS * DA_DV:3 * HEADS * DA_DV].reshape(B, T, HEADS, DA_DV),
        "gconv": z3[:, T - (GDN_CONV - 1):, 3 * HEADS * DA_DV:3 * HEADS * DA_DV + QKV_COLS],
        "S": s_new, "C": c_new, "n": n_new[:, :HEADS], "m": m_new[:, :HEADS, 0],
        "fconv": up.reshape(B, T, 2 * Fd)[:, T - (FFN_CONV - 1):, :Fd],
        "k16": kn16, "v16": v16,
    }
    return x2, st


def kernel(x_prompt, x_sample, cache_attn_k, cache_attn_v, state_gdn_conv, state_gdn_S, state_mlstm_C, state_mlstm_n, state_mlstm_m, state_ffn_conv, meta_tokens, rel_bias, norm_mix, norm_ffn, w_in, w_out, da_q_norm, da_k_norm, da_lq1, da_lk1, da_lq2, da_lk2, da_out_norm, gdn_conv_w, gdn_A_log, gdn_dt_bias, gdn_out_norm, ml_i_bias, ml_f_bias, ml_out_norm, ffn_w_up, ffn_conv_w, ffn_w_down):
    B, T, D = x_prompt.shape
    Bs, Ls, _ = x_sample.shape
    depth = w_in.shape[0]
    P = cache_attn_k.shape[2]
    NM = meta_tokens.shape[0]
    Fd = ffn_conv_w.shape[-1]
    W = HEADS * DA_DV
    assert T % CHUNK == 0 and Ls <= CHUNK and NM <= CHUNK and NM % 8 == 0 and Ls % 8 == 0

    tq = _pick(T, (512, 256, 128, 64))
    tkc = _pick(P, (512, 256, 128))
    assert tq >= 96 or T == tq, "far prompt tiles must lie past the last distinct relative-position bucket"
    assert tkc >= 96 or P == tkc
    MP = 128

    r = jnp.arange(tq, dtype=jnp.int32)[:, None]
    c = jnp.arange(tq, dtype=jnp.int32)[None, :]
    diag = jnp.where((c // CHUNK <= r // CHUNK)[None], _rel_bias_tile(rel_bias, c - r), NEG_INF)
    sub = _rel_bias_tile(rel_bias, c - r - tq)
    far = _rel_bias_tile(rel_bias, jnp.full((tq, tq), -2 * tq, jnp.int32))
    bias_big = jnp.stack([far, sub, diag], axis=1)
    cm = jnp.arange(MP, dtype=jnp.int32)[None, :]
    bm0 = _rel_bias_tile(rel_bias, cm - (NM + r))
    bm1 = _rel_bias_tile(rel_bias, jnp.broadcast_to(cm - (NM + tq + r), (tq, MP)))
    bias_meta_big = jnp.where((cm < NM)[None, None], jnp.stack([bm0, bm1], axis=1), NEG_INF)
    rs = jnp.arange(Ls, dtype=jnp.int32)[:, None]
    cs = jnp.arange(Ls, dtype=jnp.int32)[None, :]
    bias_new_s = _rel_bias_tile(rel_bias, cs - rs)
    ck = jnp.arange(tkc, dtype=jnp.int32)[None, :]
    bias_cache = jnp.stack([_rel_bias_tile(rel_bias, jnp.broadcast_to(ck - 2 * tkc - rs, (Ls, tkc))),
                            _rel_bias_tile(rel_bias, ck - tkc - rs)], axis=1)
    rm = jnp.arange(NM, dtype=jnp.int32)
    bias_new_m = _rel_bias_tile(rel_bias, rm[None, :] - rm[:, None])

    sizes = (W, W, W, QKV_COLS, HEADS, HEADS, HEADS * HD, QKV_COLS, HEADS, HEADS, HEADS * HD)
    offs = [0]
    for s_ in sizes:
        offs.append(offs[-1] + s_)
    seg = lambda w, i: w[:, offs[i]:offs[i + 1]]

    xm = jnp.broadcast_to(meta_tokens.astype(F32)[None], (B, NM, D)).reshape(B * NM, D)
    xb = x_prompt.reshape(B * T, D)
    xs = x_sample.reshape(Bs * Ls, D)
    p_states, s_states = [], []
    for l in range(depth):
        wl = w_in[l]
        w_main = jnp.concatenate([seg(wl, 0), seg(wl, 1), seg(wl, 2), seg(wl, 3), seg(wl, 7), seg(wl, 6), seg(wl, 10)],
                                 axis=1).astype(BF16)
        w_scal = jnp.concatenate([seg(wl, 4), seg(wl, 5), seg(wl, 8), seg(wl, 9),
                                  jnp.zeros((D, SCAL_COLS - 4 * HEADS), F32)], axis=1).astype(BF16)
        lw = {
            "norm_mix": norm_mix[l], "norm_ffn": norm_ffn[l], "w_in": w_main, "w_in_scal": w_scal,
            "w_out": w_out[l].astype(BF16), "w_up": ffn_w_up[l].astype(BF16), "w_down": ffn_w_down[l].astype(BF16),
            "da_q_norm": da_q_norm[l], "da_k_norm": da_k_norm[l], "gdn_conv_w": gdn_conv_w[l],
            "gdn_neg_a": -jnp.exp(gdn_A_log[l].astype(F32)), "gdn_dt_bias": gdn_dt_bias[l], "gdn_out_norm": gdn_out_norm[l],
            "ml_i_bias": ml_i_bias[l], "ml_f_bias": ml_f_bias[l], "ml_out_norm": ml_out_norm[l],
            "ffn_conv_w": ffn_conv_w[l],
        }
        lam_init = 0.8 - 0.6 * math.exp(-0.3 * l)
        lam = (jnp.exp(jnp.sum(da_lq1[l].astype(F32) * da_lk1[l].astype(F32)))
               - jnp.exp(jnp.sum(da_lq2[l].astype(F32) * da_lk2[l].astype(F32))) + lam_init).reshape(1, 1)
        out_scale = 1.0 - lam_init
        gain_o = da_out_norm[l]

        zero = {"gconv": jnp.zeros((B, GDN_CONV - 1, QKV_COLS), F32), "S": jnp.zeros((B, HEADS, HD, HD), F32),
                "C": jnp.zeros((B, HEADS, HD, HD), F32), "n": jnp.zeros((B, HEADS, HD), F32),
                "m": jnp.zeros((B, HEADS), F32), "fconv": jnp.zeros((B, FFN_CONV - 1, Fd), F32)}
        xm, st_m = _layer(xm, B, NM, NM, lw, zero,
                          lambda q, k, v: attn_small(q, k, v, None, bias_new_m, lam, gain_o, out_scale, B, NM))

        kmeta = jnp.pad(st_m["k16"].reshape(B, NM, W), ((0, 0), (0, MP - NM), (0, 0))).reshape(B * MP, W)
        vmeta = jnp.pad(st_m["v16"].reshape(B, NM, W), ((0, 0), (0, MP - NM), (0, 0))).reshape(B * MP, W)
        xb, st_b = _layer(xb, B, T, CHUNK, lw, st_m,
                          lambda q, k, v: attn_big(q, k, v, kmeta, vmeta, bias_big, bias_meta_big, lam, gain_o,
                                                   out_scale, B, T, tq))

        init_s = {"gconv": state_gdn_conv[l], "S": state_gdn_S[l].astype(F32), "C": state_mlstm_C[l].astype(F32),
                  "n": state_mlstm_n[l].astype(F32), "m": state_mlstm_m[l].astype(F32), "fconv": state_ffn_conv[l]}
        cache = (cache_attn_k[l].reshape(Bs * P, W), cache_attn_v[l].reshape(Bs * P, W), bias_cache, tkc)
        xs, st_s = _layer(xs, Bs, Ls, Ls, lw, init_s,
                          lambda q, k, v: attn_small(q, k, v, cache, bias_new_s, lam, gain_o, out_scale, Bs, Ls))

        names = ("k", "v", "gconv", "S", "C", "n", "m", "fconv")
        p_states.append([jnp.concatenate([st_m[n_], st_b[n_]], axis=1) if n_ in ("k", "v") else st_b[n_]
                         for n_ in names])
        s_states.append([st_s[n_] for n_ in names])

    p_out = [jnp.stack([p_states[l][i] for l in range(depth)]) for i in range(8)]
    s_out = [jnp.stack([s_states[l][i] for l in range(depth)]) for i in range(8)]
    y_prompt = xb.reshape(B, T, D)
    y_sample = xs.reshape(Bs, Ls, D)
    return (y_prompt, y_sample, *p_out, *s_out)
```

```python
import functools
import math

import jax
import jax.numpy as jnp
from jax import lax
from jax.experimental import pallas as pl
from jax.experimental.pallas import tpu as pltpu

F32 = jnp.float32
BF16 = jnp.bfloat16

HEADS = 4
HD = 128
DA_DV = 2 * HD
CHUNK = 64
GDN_CONV = 4
FFN_CONV = 3
NUM_BUCKETS = 32
MAX_DISTANCE = 128
RMS_EPS = 1e-6
NEG_INF = -1e30
HALO = 8

QKV_COLS = 3 * HEADS * HD
Z_COLS = 3 * HEADS * DA_DV + 2 * QKV_COLS + 2 * HEADS * HD
SCAL_COLS = 128

LOG2E = math.log2(math.e)
Q_SCALE = HD ** -0.5 * LOG2E

VMEM_LIMIT = 56 * 1024 * 1024
HI = lax.Precision.HIGHEST


def _pick(n, cands):
    for c in cands:
        if n % c == 0:
            return c
    raise ValueError(f"no tile for {n} in {cands}")


def _params(sem):
    return pltpu.CompilerParams(dimension_semantics=sem, vmem_limit_bytes=VMEM_LIMIT)


def _sigmoid(x):
    return 1.0 / (1.0 + jnp.exp(-x))


def _softplus(x):
    return jnp.maximum(x, 0.0) + jnp.log(1.0 + jnp.exp(-jnp.abs(x)))


def _hdot(a, b):
    return jnp.dot(a, b, precision=HI, preferred_element_type=F32)


def _hdot_nt(a, b):
    return lax.dot_general(a, b, (((1,), (1,)), ((), ())), precision=HI, preferred_element_type=F32)


def _hdot_tn(a, b):
    return lax.dot_general(a, b, (((0,), (0,)), ((), ())), precision=HI, preferred_element_type=F32)


def _dot_nt(a, b):
    return lax.dot_general(a, b, (((1,), (1,)), ((), ())), preferred_element_type=F32)


def _rms_matmul_kernel(x_ref, g_ref, w_ref, *rest, has_extra):
    if has_extra:
        ws_ref, o_ref, os_ref, xn_ref = rest
    else:
        o_ref, xn_ref = rest

    @pl.when(pl.program_id(1) == 0)
    def _():
        x = x_ref[...]
        ms = jnp.mean(x * x, axis=-1, keepdims=True)
        xn = (x * lax.rsqrt(ms + RMS_EPS) * g_ref[...]).astype(BF16)
        xn_ref[...] = xn
        if has_extra:
            os_ref[...] = jnp.dot(xn, ws_ref[...], preferred_element_type=F32)

    o_ref[...] = jnp.dot(xn_ref[...], w_ref[...], preferred_element_type=F32).astype(o_ref.dtype)


def rms_matmul(x, gain, w, w_extra=None, name="rms_matmul"):
    M, K = x.shape
    N = w.shape[1]
    tm = _pick(M, (1024, 512, 256, 128, 64, 32, 16, 8))
    tn = _pick(N, (512, 256, 128))
    has_extra = w_extra is not None
    in_specs = [pl.BlockSpec((tm, K), lambda i, j: (i, 0)),
                pl.BlockSpec((1, K), lambda i, j: (0, 0)),
                pl.BlockSpec((K, tn), lambda i, j: (0, j))]
    out_specs = [pl.BlockSpec((tm, tn), lambda i, j: (i, j))]
    out_shape = [jax.ShapeDtypeStruct((M, N), F32)]
    args = [x, gain.reshape(1, K), w]
    if has_extra:
        ne = w_extra.shape[1]
        in_specs.append(pl.BlockSpec((K, ne), lambda i, j: (0, 0)))
        out_specs.append(pl.BlockSpec((tm, ne), lambda i, j: (i, 0)))
        out_shape.append(jax.ShapeDtypeStruct((M, ne), F32))
        args.append(w_extra)
    outs = pl.pallas_call(
        functools.partial(_rms_matmul_kernel, has_extra=has_extra),
        grid=(M // tm, N // tn),
        in_specs=in_specs, out_specs=out_specs, out_shape=out_shape,
        scratch_shapes=[pltpu.VMEM((tm, K), BF16)],
        compiler_params=_params(("parallel", "arbitrary")),
        name=name,
    )(*args)
    return outs if has_extra else outs[0]


def _matmul_res_kernel(*refs, n_a):
    a_refs, w_refs = refs[:n_a], refs[n_a:2 * n_a]
    res_ref, o_ref = refs[2 * n_a], refs[2 * n_a + 1]
    acc = res_ref[...]
    for a_ref, w_ref in zip(a_refs, w_refs):
        acc = acc + jnp.dot(a_ref[...], w_ref[...], preferred_element_type=F32)
    o_ref[...] = acc


def matmul_res(a_list, w, res, name="matmul_res"):
    M, N = res.shape
    ktot = sum(a.shape[1] for a in a_list)
    tm = _pick(M, (1024, 512, 256, 128, 64, 32, 16, 8) if ktot <= 4096 else (512, 256, 128, 64, 32, 16, 8))
    tn = _pick(N, (512, 256, 128))
    in_specs, off = [], 0
    for a in a_list:
        in_specs.append(pl.BlockSpec((tm, a.shape[1]), lambda i, j: (i, 0)))
    for a in a_list:
        k = a.shape[1]
        assert off % k == 0
        in_specs.append(pl.BlockSpec((k, tn), lambda i, j, _o=off // k: (_o, j)))
        off += k
    in_specs.append(pl.BlockSpec((tm, tn), lambda i, j: (i, j)))
    return pl.pallas_call(
        functools.partial(_matmul_res_kernel, n_a=len(a_list)),
        grid=(M // tm, N // tn),
        in_specs=in_specs,
        out_specs=pl.BlockSpec((tm, tn), lambda i, j: (i, j)),
        out_shape=jax.ShapeDtypeStruct((M, N), F32),
        compiler_params=_params(("parallel", "arbitrary")),
        name=name,
    )(*a_list, *([w] * len(a_list)), res)


def _qk_prep_kernel(q_ref, k_ref, v_ref, gq_ref, gk_ref, qn_ref, kn32_ref, kn16_ref, v16_ref):
    gq = gq_ref[...]
    gk = gk_ref[...]
    for g in range(2 * HEADS):
        sl = slice(g * HD, (g + 1) * HD)
        q = q_ref[:, sl]
        k = k_ref[:, sl]
        qn = q * lax.rsqrt(jnp.mean(q * q, axis=-1, keepdims=True) + RMS_EPS) * gq
        kn = k * lax.rsqrt(jnp.mean(k * k, axis=-1, keepdims=True) + RMS_EPS) * gk
        qn_ref[:, sl] = (qn * Q_SCALE).astype(BF16)
        kn32_ref[:, sl] = kn
        kn16_ref[:, sl] = kn.astype(BF16)
    v16_ref[...] = v_ref[...].astype(BF16)


def qk_prep(z, gq, gk):
    M = z.shape[0]
    W = 2 * HEADS * HD
    tm = _pick(M, (512, 256, 128, 64, 32, 16, 8))
    col = lambda c: pl.BlockSpec((tm, W), lambda i, _c=c: (i, _c))
    vec = pl.BlockSpec((1, HD), lambda i: (0, 0))
    row = pl.BlockSpec((tm, W), lambda i: (i, 0))
    return pl.pallas_call(
        _qk_prep_kernel,
        grid=(M // tm,),
        in_specs=[col(0), col(1), col(2), vec, vec],
        out_specs=[row, row, row, row],
        out_shape=[jax.ShapeDtypeStruct((M, W), BF16), jax.ShapeDtypeStruct((M, W), F32),
                   jax.ShapeDtypeStruct((M, W), BF16), jax.ShapeDtypeStruct((M, W), BF16)],
        compiler_params=_params(("parallel",)),
        name="qk_prep",
    )(z, z, z, gq.reshape(1, HD), gk.reshape(1, HD))


def _softmax_update(mi, s, v, m_ref, l_ref, acc_ref):
    m_prev = m_ref[mi]
    m_new = jnp.maximum(m_prev, jnp.max(s, axis=-1, keepdims=True))
    alpha = jnp.exp2(m_prev - m_new)
    p = jnp.exp2(s - m_new)
    l_ref[mi] = alpha * l_ref[mi] + jnp.sum(p, axis=-1, keepdims=True)
    acc_ref[mi] = alpha * acc_ref[mi] + jnp.dot(p.astype(BF16), v, preferred_element_type=F32)
    m_ref[mi] = m_new


def _attn_init(m_ref, l_ref, acc_ref):
    m_ref[...] = jnp.full(m_ref.shape, -jnp.inf, F32)
    l_ref[...] = jnp.zeros(l_ref.shape, F32)
    acc_ref[...] = jnp.zeros(acc_ref.shape, F32)


def _attn_tile(q, k, v, bias, m_ref, l_ref, acc_ref):
    for mi in range(2):
        sl = slice(mi * HD, (mi + 1) * HD)
        s = _dot_nt(q[:, sl], k[:, sl]) + bias
        _softmax_update(mi, s, v, m_ref, l_ref, acc_ref)


def _attn_finish(lam, out_scale, g_ref, o_ref, l_ref, acc_ref):
    o = acc_ref[0] / l_ref[0] - lam * (acc_ref[1] / l_ref[1])
    on = o * lax.rsqrt(jnp.mean(o * o, axis=-1, keepdims=True) + RMS_EPS) * g_ref[...]
    o_ref[...] = (on * out_scale).astype(o_ref.dtype)


def _attn_scratch(tq):
    return [pltpu.VMEM((2, tq, 1), F32), pltpu.VMEM((2, tq, 1), F32), pltpu.VMEM((2, tq, DA_DV), F32)]


ATT_RB = 128
LANES = 128


def _rowblock_tile(q_ref, k_ref, v_ref, bias_ref, m_ref, l_ref, acc_ref):
    tq = q_ref.shape[0]
    tk = k_ref.shape[0]
    nb = tk // LANES
    nrb = tq // ATT_RB
    v = v_ref[...]

    def scores(r):
        rows = pl.ds(r * ATT_RB, ATT_RB)
        bias = bias_ref[0, 0, rows, :]
        return [_dot_nt(q_ref[rows, mi * HD:(mi + 1) * HD], k_ref[:, mi * HD:(mi + 1) * HD]) + bias
                for mi in range(2)]

    def update(r, s_pair):
        rows = pl.ds(r * ATT_RB, ATT_RB)
        for mi in range(2):
            s = s_pair[mi]
            blocks = [s[:, c * LANES:(c + 1) * LANES] for c in range(nb)]
            smax = blocks[0]
            for blk in blocks[1:]:
                smax = jnp.maximum(smax, blk)
            m_prev = m_ref[mi, rows, :]
            m_new = jnp.maximum(m_prev, jnp.max(smax, axis=-1, keepdims=True))
            alpha = jnp.exp2(m_prev - m_new)
            ps = [jnp.exp2(blk - m_new) for blk in blocks]
            psum = ps[0]
            for pb in ps[1:]:
                psum = psum + pb
            l_ref[mi, rows, :] = alpha * l_ref[mi, rows, :] + psum
            m_ref[mi, rows, :] = m_new
            p = jnp.concatenate(ps, axis=1).astype(BF16) if nb > 1 else ps[0].astype(BF16)
            pv = jnp.dot(p, v, preferred_element_type=F32)
            acc_ref[mi, rows, :] = jnp.concatenate([alpha, alpha], axis=1) * acc_ref[mi, rows, :] + pv

    s_next = scores(0)
    for r in range(nrb):
        s_cur = s_next
        if r + 1 < nrb:
            s_next = scores(r + 1)
        update(r, s_cur)


def _attn_big_kernel(qi_ref, kj_ref, lam_ref, q_ref, k_ref, v_ref, km_ref, vm_ref, bias_ref, bmeta_ref,
                     g_ref, o_ref, m_ref, l_ref, acc_ref, *, out_scale):
    step = pl.program_id(2)
    i = qi_ref[step]
    j = kj_ref[step]

    @pl.when(j == 0)
    def _():
        _attn_init(m_ref, l_ref, acc_ref)
        _rowblock_tile(q_ref, km_ref, vm_ref, bmeta_ref, m_ref, l_ref, acc_ref)

    _rowblock_tile(q_ref, k_ref, v_ref, bias_ref, m_ref, l_ref, acc_ref)

    @pl.when(j == i)
    def _():
        l0 = jnp.sum(l_ref[0], axis=-1, keepdims=True)
        l1 = jnp.sum(l_ref[1], axis=-1, keepdims=True)
        o = acc_ref[0] / l0 - lam_ref[0, 0] * (acc_ref[1] / l1)
        on = o * lax.rsqrt(jnp.mean(o * o, axis=-1, keepdims=True) + RMS_EPS) * g_ref[...]
        o_ref[...] = (on * out_scale).astype(o_ref.dtype)


def attn_big(qn, kn16, v16, kmeta, vmeta, bias, bmeta, lam, gain, out_scale, B, T, tq):
    nq = T // tq
    mp = kmeta.shape[0] // B
    pairs = [(i, j) for i in range(nq) for j in range(i + 1)]
    qi = jnp.asarray([p[0] for p in pairs], jnp.int32)
    kj = jnp.asarray([p[1] for p in pairs], jnp.int32)
    grid_spec = pltpu.PrefetchScalarGridSpec(
        num_scalar_prefetch=2,
        grid=(B, HEADS, len(pairs)),
        in_specs=[
            pl.BlockSpec(memory_space=pltpu.SMEM),
            pl.BlockSpec((tq, DA_DV), lambda b, h, s, qi, kj: (b * nq + qi[s], h)),
            pl.BlockSpec((tq, DA_DV), lambda b, h, s, qi, kj: (b * nq + kj[s], h)),
            pl.BlockSpec((tq, DA_DV), lambda b, h, s, qi, kj: (b * nq + kj[s], h)),
            pl.BlockSpec((mp, DA_DV), lambda b, h, s, qi, kj: (b, h)),
            pl.BlockSpec((mp, DA_DV), lambda b, h, s, qi, kj: (b, h)),
            pl.BlockSpec((1, 1, tq, tq), lambda b, h, s, qi, kj: (h, jnp.maximum(kj[s] - qi[s] + 2, 0), 0, 0)),
            pl.BlockSpec((1, 1, tq, mp), lambda b, h, s, qi, kj: (h, jnp.minimum(qi[s], 1), 0, 0)),
            pl.BlockSpec((1, DA_DV), lambda b, h, s, qi, kj: (0, 0)),
        ],
        out_specs=pl.BlockSpec((tq, DA_DV), lambda b, h, s, qi, kj: (b * nq + qi[s], h)),
        scratch_shapes=[pltpu.VMEM((2, tq, LANES), F32), pltpu.VMEM((2, tq, LANES), F32),
                        pltpu.VMEM((2, tq, DA_DV), F32)],
    )
    return pl.pallas_call(
        functools.partial(_attn_big_kernel, out_scale=out_scale),
        grid_spec=grid_spec,
        out_shape=jax.ShapeDtypeStruct((B * T, HEADS * DA_DV), BF16),
        compiler_params=_params(("parallel", "parallel", "arbitrary")),
        name="attn_prompt",
    )(qi, kj, lam, qn, kn16, v16, kmeta, vmeta, bias, bmeta, gain.reshape(1, DA_DV))


def _attn_small_kernel(lam_ref, q_ref, *rest, nc, out_scale):
    if nc > 0:
        ck_ref, cv_ref, bc_ref, k_ref, v_ref, bn_ref, g_ref, o_ref, m_ref, l_ref, acc_ref = rest
    else:
        k_ref, v_ref, bn_ref, g_ref, o_ref, m_ref, l_ref, acc_ref = rest
    j = pl.program_id(2)
    q = q_ref[...]

    @pl.when(j == 0)
    def _():
        _attn_init(m_ref, l_ref, acc_ref)

    if nc > 0:
        @pl.when(j < nc)
        def _():
            _attn_tile(q, ck_ref[...].astype(BF16), cv_ref[...].astype(BF16), bc_ref[0, 0], m_ref, l_ref, acc_ref)

    @pl.when(j == nc)
    def _():
        _attn_tile(q, k_ref[...], v_ref[...], bn_ref[0], m_ref, l_ref, acc_ref)
        _attn_finish(lam_ref[0, 0], out_scale, g_ref, o_ref, l_ref, acc_ref)


def attn_small(qn, kn16, v16, cache, bias_new, lam, gain, out_scale, B, L):
    W = HEADS * DA_DV
    blk = pl.BlockSpec((L, DA_DV), lambda b, h, j: (b, h))
    in_specs = [pl.BlockSpec(memory_space=pltpu.SMEM), blk]
    args = [lam, qn]
    nc = 0
    if cache is not None:
        ck, cv, bc, tk = cache
        nc = ck.shape[0] // B // tk
        cspec = pl.BlockSpec((tk, DA_DV), lambda b, h, j: (b * nc + jnp.minimum(j, nc - 1), h))
        in_specs += [cspec, cspec,
                     pl.BlockSpec((1, 1, L, tk), lambda b, h, j: (h, jnp.where(j >= nc - 1, 1, 0), 0, 0))]
        args += [ck, cv, bc]
    in_specs += [blk, blk, pl.BlockSpec((1, L, L), lambda b, h, j: (h, 0, 0)),
                 pl.BlockSpec((1, DA_DV), lambda b, h, j: (0, 0))]
    args += [kn16, v16, bias_new, gain.reshape(1, DA_DV)]
    return pl.pallas_call(
        functools.partial(_attn_small_kernel, nc=nc, out_scale=out_scale),
        grid=(B, HEADS, nc + 1),
        in_specs=in_specs,
        out_specs=blk,
        out_shape=jax.ShapeDtypeStruct((B * L, W), BF16),
        scratch_shapes=_attn_scratch(L),
        compiler_params=_params(("parallel", "parallel", "arbitrary")),
        name="attn_block",
    )(*args)


def _chunk_masks(L):
    row = lax.broadcasted_iota(jnp.int32, (L, L), 0)
    col = lax.broadcasted_iota(jnp.int32, (L, L), 1)
    return row, col


def _gdn_kernel(nega_ref, dtb_ref, x_ref, gate_ref, sc_ref, prev_ref, cw_ref, s0_ref, gn_ref,
                o_ref, sout_ref, xbuf, s_scr, *, L):
    i = pl.program_id(1)

    @pl.when(i == 0)
    def _():
        xbuf[0:HALO, :] = prev_ref[0]
        s_scr[...] = s0_ref[0]

    xbuf[HALO:HALO + L, :] = x_ref[...]
    cw = cw_ref[...]
    y = xbuf[HALO - 3:HALO - 3 + L, :] * cw[0:1, :]
    for t in range(1, GDN_CONV):
        y = y + xbuf[HALO - 3 + t:HALO - 3 + t + L, :] * cw[t:t + 1, :]
    tail = xbuf[L:L + HALO, :]
    xbuf[0:HALO, :] = tail
    gc = y * _sigmoid(y)

    row, col = _chunk_masks(L)
    incl = col <= row
    strict = col < row
    eye = (col == row).astype(F32)
    tril = incl.astype(F32)
    ones = jnp.ones((L, L), F32)
    sc = sc_ref[...]
    nqk = HEADS * HD

    for h in range(HEADS):
        sl = slice(h * HD, (h + 1) * HD)
        xq = gc[:, h * HD:(h + 1) * HD]
        xk = gc[:, nqk + h * HD:nqk + (h + 1) * HD]
        xv = gc[:, 2 * nqk + h * HD:2 * nqk + (h + 1) * HD]
        qn = xq * lax.rsqrt(jnp.sum(xq * xq, axis=-1, keepdims=True) + RMS_EPS) * (HD ** -0.5)
        kn = xk * lax.rsqrt(jnp.sum(xk * xk, axis=-1, keepdims=True) + RMS_EPS)
        beta = _sigmoid(sc[:, h:h + 1])
        g = nega_ref[h] * _softplus(sc[:, HEADS + h:HEADS + h + 1] + dtb_ref[h])
        gb = jnp.broadcast_to(g, (L, L))
        g_col = _hdot(tril, gb)
        g_row = _hdot(ones, jnp.where(row <= col, gb, 0.0))
        dec = jnp.where(incl, jnp.exp(jnp.where(incl, g_col - g_row, 0.0)), 0.0)
        kk = _hdot_nt(kn, kn)
        nmat = jnp.where(strict, beta * kk * dec, 0.0)
        tinv = eye - nmat
        pw = _hdot(nmat, nmat)
        p = 2
        while True:
            tinv = tinv + _hdot(tinv, pw)
            p *= 2
            if p >= L:
                break
            pw = _hdot(pw, pw)
        gcol = g_col[:, 0:1]
        eg = jnp.exp(gcol)
        s_old = s_scr[h]
        w_mat = _hdot(tinv, beta * eg * kn)
        u0 = _hdot(tinv, beta * xv)
        u = u0 - _hdot(w_mat, s_old)
        qk = _hdot_nt(qn, kn) * dec
        o = eg * _hdot(qn, s_old) + _hdot(qk, u)
        g_last = g_col[L - 1:L, 0:1]
        s_scr[h] = jnp.exp(g_last) * s_old + _hdot_tn(kn * jnp.exp(g_last - gcol), u)
        on = o * lax.rsqrt(jnp.mean(o * o, axis=-1, keepdims=True) + RMS_EPS) * gn_ref[...]
        gt = gate_ref[:, sl]
        o_ref[:, sl] = (on * (gt * _sigmoid(gt))).astype(o_ref.dtype)

    @pl.when(i == pl.num_programs(1) - 1)
    def _():
        sout_ref[0] = s_scr[...]


def gdn(z, sc, prev8, conv_w, neg_a, dt_bias, s0, gn, B, T, L):
    nt = T // L
    smem = pl.BlockSpec(memory_space=pltpu.SMEM)
    st = pl.BlockSpec((1, HEADS, HD, HD), lambda b, i: (b, 0, 0, 0))
    return pl.pallas_call(
        functools.partial(_gdn_kernel, L=L),
        grid=(B, nt),
        in_specs=[smem, smem,
                  pl.BlockSpec((L, QKV_COLS), lambda b, i: (b * nt + i, 2)),
                  pl.BlockSpec((L, HEADS * HD), lambda b, i: (b * nt + i, 12)),
                  pl.BlockSpec((L, SCAL_COLS), lambda b, i: (b * nt + i, 0)),
                  pl.BlockSpec((1, HALO, QKV_COLS), lambda b, i: (b, 0, 0)),
                  pl.BlockSpec((GDN_CONV, QKV_COLS), lambda b, i: (0, 0)),
                  st,
                  pl.BlockSpec((1, HD), lambda b, i: (0, 0))],
        out_specs=[pl.BlockSpec((L, HEADS * HD), lambda b, i: (b * nt + i, 0)), st],
        out_shape=[jax.ShapeDtypeStruct((B * T, HEADS * HD), BF16),
                   jax.ShapeDtypeStruct((B, HEADS, HD, HD), F32)],
        scratch_shapes=[pltpu.VMEM((L + HALO, QKV_COLS), F32), pltpu.VMEM((HEADS, HD, HD), F32)],
        compiler_params=_params(("parallel", "arbitrary")),
        name="gdn_scan",
    )(neg_a, dt_bias, z, z, sc, prev8, conv_w, s0, gn.reshape(1, HD))


def _mlstm_kernel(ib_ref, fb_ref, x_ref, og_ref, sc_ref, c0_ref, n0_ref, m0_ref, gn_ref,
                  o_ref, cout_ref, nout_ref, mout_ref, c_scr, n_scr, m_scr, *, L):
    i = pl.program_id(1)

    @pl.when(i == 0)
    def _():
        c_scr[...] = c0_ref[0]
        n_scr[...] = n0_ref[0]
        m_scr[...] = m0_ref[0]

    row, col = _chunk_masks(L)
    incl = col <= row
    tril = incl.astype(F32)
    ones = jnp.ones((L, L), F32)
    x = x_ref[...]
    sc = sc_ref[...]
    nqk = HEADS * HD

    for h in range(HEADS):
        sl = slice(h * HD, (h + 1) * HD)
        q = x[:, h * HD:(h + 1) * HD]
        k = x[:, nqk + h * HD:nqk + (h + 1) * HD] * (HD ** -0.5)
        v = x[:, 2 * nqk + h * HD:2 * nqk + (h + 1) * HD]
        ig = sc[:, 2 * HEADS + h:2 * HEADS + h + 1] + ib_ref[h]
        lf = -_softplus(-(sc[:, 3 * HEADS + h:3 * HEADS + h + 1] + fb_ref[h]))
        lfb = jnp.broadcast_to(lf, (L, L))
        igb = jnp.broadcast_to(ig, (L, L))
        b_col = _hdot(tril, lfb)
        r_row = _hdot(ones, jnp.where(row == col, igb, 0.0) - jnp.where(row <= col, lfb, 0.0))
        dmat = jnp.where(incl, b_col + r_row, NEG_INF)
        bcol = b_col[:, 0:1]
        m_prev = m_scr[h:h + 1, 0:1]
        inter = bcol + m_prev
        m_t = jnp.maximum(inter, jnp.max(dmat, axis=-1, keepdims=True))
        w_intra = jnp.exp(dmat - m_t)
        w_inter = jnp.exp(inter - m_t)
        qk = _hdot_nt(q, k) * w_intra
        c_old = c_scr[h]
        n_old = n_scr[h:h + 1, :]
        num = w_inter * _hdot(q, c_old) + _hdot(qk, v)
        den = w_inter * jnp.sum(q * n_old, axis=-1, keepdims=True) + jnp.sum(qk, axis=-1, keepdims=True)
        hout = num / jnp.maximum(jnp.abs(den), jnp.exp(-m_t))
        m_new = m_t[L - 1:L, :]
        b_last = bcol[L - 1:L, :]
        w_fin = jnp.exp(b_last - bcol + ig - m_new)
        dec0 = jnp.exp(b_last + m_prev - m_new)
        kw = w_fin * k
        c_scr[h] = dec0 * c_old + _hdot_tn(kw, v)
        n_scr[h:h + 1, :] = dec0 * n_old + jnp.sum(kw, axis=0, keepdims=True)
        m_scr[h:h + 1, :] = jnp.broadcast_to(m_new, (1, HD))
        on = hout * lax.rsqrt(jnp.mean(hout * hout, axis=-1, keepdims=True) + RMS_EPS) * gn_ref[...]
        o_ref[:, sl] = (on * _sigmoid(og_ref[:, sl])).astype(o_ref.dtype)

    @pl.when(i == pl.num_programs(1) - 1)
    def _():
        cout_ref[0] = c_scr[...]
        nout_ref[0] = n_scr[...]
        mout_ref[0] = m_scr[...]


def mlstm(z, sc, i_bias, f_bias, c0, n0p, m0p, gn, B, T, L):
    nt = T // L
    smem = pl.BlockSpec(memory_space=pltpu.SMEM)
    st = pl.BlockSpec((1, HEADS, HD, HD), lambda b, i: (b, 0, 0, 0))
    vec = pl.BlockSpec((1, 8, HD), lambda b, i: (b, 0, 0))
    return pl.pallas_call(
        functools.partial(_mlstm_kernel, L=L),
        grid=(B, nt),
        in_specs=[smem, smem,
                  pl.BlockSpec((L, QKV_COLS), lambda b, i: (b * nt + i, 3)),
                  pl.BlockSpec((L, HEADS * HD), lambda b, i: (b * nt + i, 13)),
                  pl.BlockSpec((L, SCAL_COLS), lambda b, i: (b * nt + i, 0)),
                  st, vec, vec,
                  pl.BlockSpec((1, HD), lambda b, i: (0, 0))],
        out_specs=[pl.BlockSpec((L, HEADS * HD), lambda b, i: (b * nt + i, 0)), st, vec, vec],
        out_shape=[jax.ShapeDtypeStruct((B * T, HEADS * HD), BF16),
                   jax.ShapeDtypeStruct((B, HEADS, HD, HD), F32),
                   jax.ShapeDtypeStruct((B, 8, HD), F32),
                   jax.ShapeDtypeStruct((B, 8, HD), F32)],
        scratch_shapes=[pltpu.VMEM((HEADS, HD, HD), F32), pltpu.VMEM((8, HD), F32), pltpu.VMEM((8, HD), F32)],
        compiler_params=_params(("parallel", "arbitrary")),
        name="mlstm_scan",
    )(i_bias, f_bias, z, z, sc, c0, n0p, m0p, gn.reshape(1, HD))


def _ffn_act_kernel(gate_ref, val_ref, halo_ref, prev_ref, cw_ref, o_ref, buf, *, tm):
    i = pl.program_id(1)
    buf[0:HALO, :] = jnp.where(i == 0, prev_ref[0], halo_ref[...])
    buf[HALO:HALO + tm, :] = gate_ref[...]
    cw = cw_ref[...]
    y = buf[HALO - 2:HALO - 2 + tm, :] * cw[0:1, :]
    for t in range(1, FFN_CONV):
        y = y + buf[HALO - 2 + t:HALO - 2 + t + tm, :] * cw[t:t + 1, :]
    o_ref[...] = (y * _sigmoid(y) * val_ref[...]).astype(o_ref.dtype)


def ffn_act(up, prev8, conv_w, B, T):
    M = B * T
    Fd = conv_w.shape[1]
    tm = _pick(T, (512, 256, 128, 64, 32, 16, 8))
    tf = _pick(Fd, (512, 256, 128))
    nt, nf = T // tm, Fd // tf
    hb = tm // HALO
    return pl.pallas_call(
        functools.partial(_ffn_act_kernel, tm=tm),
        grid=(B, nt, nf),
        in_specs=[pl.BlockSpec((tm, tf), lambda b, i, c: (b * nt + i, c)),
                  pl.BlockSpec((tm, tf), lambda b, i, c: (b * nt + i, nf + c)),
                  pl.BlockSpec((HALO, tf), lambda b, i, c: (jnp.maximum((b * nt + i) * hb - 1, 0), c)),
                  pl.BlockSpec((1, HALO, tf), lambda b, i, c: (b, 0, c)),
                  pl.BlockSpec((FFN_CONV, tf), lambda b, i, c: (0, c))],
        out_specs=pl.BlockSpec((tm, tf), lambda b, i, c: (b * nt + i, c)),
        out_shape=jax.ShapeDtypeStruct((M, Fd), BF16),
        scratch_shapes=[pltpu.VMEM((tm + HALO, tf), F32)],
        compiler_params=_params(("parallel", "parallel", "parallel")),
        name="ffn_act",
    )(up, up, up, prev8, conv_w)


def _t5_bucket(rel):
    half = NUM_BUCKETS // 2
    exact = half // 2
    n = jnp.abs(rel)
    nf = jnp.maximum(n, 1).astype(F32)
    large = exact + (jnp.log(nf / exact) / math.log(MAX_DISTANCE / exact) * (half - exact)).astype(jnp.int32)
    large = jnp.minimum(large, half - 1)
    return jnp.where(rel > 0, half, 0) + jnp.where(n < exact, n, large)


def _rel_bias_tile(rel_bias, rel):
    bucket = _t5_bucket(rel)
    rb = rel_bias.astype(F32) * LOG2E
    out = jnp.zeros((rb.shape[1],) + rel.shape, F32)
    for kb in range(NUM_BUCKETS):
        out = jnp.where((bucket == kb)[None], rb[kb][:, None, None], out)
    return out


def _pad_rows(a, rows):
    return jnp.pad(a, ((0, 0), (rows - a.shape[1], 0), (0, 0)))


def _layer(x, B, T, L, lw, init, attn_fn):
    z, sc = rms_matmul(x, lw["norm_mix"], lw["w_in"], lw["w_in_scal"], name="in_proj")
    qn, kn32, kn16, v16 = qk_prep(z, lw["da_q_norm"], lw["da_k_norm"])
    o_da = attn_fn(qn, kn16, v16)
    o_g, s_new = gdn(z, sc, _pad_rows(init["gconv"], HALO), lw["gdn_conv_w"], lw["gdn_neg_a"], lw["gdn_dt_bias"],
                     init["S"], lw["gdn_out_norm"], B, T, L)
    n0p = jnp.pad(init["n"], ((0, 0), (0, 8 - HEADS), (0, 0)))
    m0p = jnp.pad(jnp.broadcast_to(init["m"][:, :, None], (B, HEADS, HD)), ((0, 0), (0, 8 - HEADS), (0, 0)))
    o_m, c_new, n_new, m_new = mlstm(z, sc, lw["ml_i_bias"], lw["ml_f_bias"], init["C"], n0p, m0p,
                                     lw["ml_out_norm"], B, T, L)
    x1 = matmul_res([o_da, o_g, o_m], lw["w_out"], x, name="out_proj")
    up = rms_matmul(x1, lw["norm_ffn"], lw["w_up"], name="ffn_up")
    hmid = ffn_act(up, _pad_rows(init["fconv"], HALO), lw["ffn_conv_w"], B, T)
    x2 = matmul_res([hmid], lw["w_down"], x1, name="ffn_down")
    Fd = lw["ffn_conv_w"].shape[1]
    z3 = z.reshape(B, T, Z_COLS)
    st = {
        "k": kn32.reshape(B, T, HEADS, 2, HD),
        "v": z3[:, :, 2 * HEADS * DA_DV:3 * HEADS * DA_DV].reshape(B, T, HEADS, DA_DV),
        "gconv": z3[:, T - (GDN_CONV - 1):, 3 * HEADS * DA_DV:3 * HEADS * DA_DV + QKV_COLS],
        "S": s_new, "C": c_new, "n": n_new[:, :HEADS], "m": m_new[:, :HEADS, 0],
        "fconv": up.reshape(B, T, 2 * Fd)[:, T - (FFN_CONV - 1):, :Fd],
        "k16": kn16, "v16": v16,
    }
    return x2, st


def kernel(x_prompt, x_sample, cache_attn_k, cache_attn_v, state_gdn_conv, state_gdn_S, state_mlstm_C, state_mlstm_n, state_mlstm_m, state_ffn_conv, meta_tokens, rel_bias, norm_mix, norm_ffn, w_in, w_out, da_q_norm, da_k_norm, da_lq1, da_lk1, da_lq2, da_lk2, da_out_norm, gdn_conv_w, gdn_A_log, gdn_dt_bias, gdn_out_norm, ml_i_bias, ml_f_bias, ml_out_norm, ffn_w_up, ffn_conv_w, ffn_w_down):
    B, T, D = x_prompt.shape
    Bs, Ls, _ = x_sample.shape
    depth = w_in.shape[0]
    P = cache_attn_k.shape[2]
    NM = meta_tokens.shape[0]
    Fd = ffn_conv_w.shape[-1]
    W = HEADS * DA_DV
    assert T % CHUNK == 0 and Ls <= CHUNK and NM <= CHUNK and NM % 8 == 0 and Ls % 8 == 0

    tq = _pick(T, (512, 256, 128, 64))
    tkc = _pick(P, (512, 256, 128))
    assert tq >= 96 or T == tq, "far prompt tiles must lie past the last distinct relative-position bucket"
    assert tkc >= 96 or P == tkc
    MP = 128

    r = jnp.arange(tq, dtype=jnp.int32)[:, None]
    c = jnp.arange(tq, dtype=jnp.int32)[None, :]
    diag = jnp.where((c // CHUNK <= r // CHUNK)[None], _rel_bias_tile(rel_bias, c - r), NEG_INF)
    sub = _rel_bias_tile(rel_bias, c - r - tq)
    far = _rel_bias_tile(rel_bias, jnp.full((tq, tq), -2 * tq, jnp.int32))
    bias_big = jnp.stack([far, sub, diag], axis=1)
    cm = jnp.arange(MP, dtype=jnp.int32)[None, :]
    bm0 = _rel_bias_tile(rel_bias, cm - (NM + r))
    bm1 = _rel_bias_tile(rel_bias, jnp.broadcast_to(cm - (NM + tq + r), (tq, MP)))
    bias_meta_big = jnp.where((cm < NM)[None, None], jnp.stack([bm0, bm1], axis=1), NEG_INF)
    rs = jnp.arange(Ls, dtype=jnp.int32)[:, None]
    cs = jnp.arange(Ls, dtype=jnp.int32)[None, :]
    bias_new_s = _rel_bias_tile(rel_bias, cs - rs)
    ck = jnp.arange(tkc, dtype=jnp.int32)[None, :]
    bias_cache = jnp.stack([_rel_bias_tile(rel_bias, jnp.broadcast_to(ck - 2 * tkc - rs, (Ls, tkc))),
                            _rel_bias_tile(rel_bias, ck - tkc - rs)], axis=1)
    rm = jnp.arange(NM, dtype=jnp.int32)
    bias_new_m = _rel_bias_tile(rel_bias, rm[None, :] - rm[:, None])

    sizes = (W, W, W, QKV_COLS, HEADS, HEADS, HEADS * HD, QKV_COLS, HEADS, HEADS, HEADS * HD)
    offs = [0]
    for s_ in sizes:
        offs.append(offs[-1] + s_)
    seg = lambda w, i: w[:, offs[i]:offs[i + 1]]

    xm = jnp.broadcast_to(meta_tokens.astype(F32)[None], (B, NM, D)).reshape(B * NM, D)
    xb = x_prompt.reshape(B * T, D)
    xs = x_sample.reshape(Bs * Ls, D)
    p_states, s_states = [], []
    for l in range(depth):
        wl = w_in[l]
        w_main = jnp.concatenate([seg(wl, 0), seg(wl, 1), seg(wl, 2), seg(wl, 3), seg(wl, 7), seg(wl, 6), seg(wl, 10)],
                                 axis=1).astype(BF16)
        w_scal = jnp.concatenate([seg(wl, 4), seg(wl, 5), seg(wl, 8), seg(wl, 9),
                                  jnp.zeros((D, SCAL_COLS - 4 * HEADS), F32)], axis=1).astype(BF16)
        lw = {
            "norm_mix": norm_mix[l], "norm_ffn": norm_ffn[l], "w_in": w_main, "w_in_scal": w_scal,
            "w_out": w_out[l].astype(BF16), "w_up": ffn_w_up[l].astype(BF16), "w_down": ffn_w_down[l].astype(BF16),
            "da_q_norm": da_q_norm[l], "da_k_norm": da_k_norm[l], "gdn_conv_w": gdn_conv_w[l],
            "gdn_neg_a": -jnp.exp(gdn_A_log[l].astype(F32)), "gdn_dt_bias": gdn_dt_bias[l], "gdn_out_norm": gdn_out_norm[l],
            "ml_i_bias": ml_i_bias[l], "ml_f_bias": ml_f_bias[l], "ml_out_norm": ml_out_norm[l],
            "ffn_conv_w": ffn_conv_w[l],
        }
        lam_init = 0.8 - 0.6 * math.exp(-0.3 * l)
        lam = (jnp.exp(jnp.sum(da_lq1[l].astype(F32) * da_lk1[l].astype(F32)))
               - jnp.exp(jnp.sum(da_lq2[l].astype(F32) * da_lk2[l].astype(F32))) + lam_init).reshape(1, 1)
        out_scale = 1.0 - lam_init
        gain_o = da_out_norm[l]

        zero = {"gconv": jnp.zeros((B, GDN_CONV - 1, QKV_COLS), F32), "S": jnp.zeros((B, HEADS, HD, HD), F32),
                "C": jnp.zeros((B, HEADS, HD, HD), F32), "n": jnp.zeros((B, HEADS, HD), F32),
                "m": jnp.zeros((B, HEADS), F32), "fconv": jnp.zeros((B, FFN_CONV - 1, Fd), F32)}
        xm, st_m = _layer(xm, B, NM, NM, lw, zero,
                          lambda q, k, v: attn_small(q, k, v, None, bias_new_m, lam, gain_o, out_scale, B, NM))

        kmeta = jnp.pad(st_m["k16"].reshape(B, NM, W), ((0, 0), (0, MP - NM), (0, 0))).reshape(B * MP, W)
        vmeta = jnp.pad(st_m["v16"].reshape(B, NM, W), ((0, 0), (0, MP - NM), (0, 0))).reshape(B * MP, W)
        xb, st_b = _layer(xb, B, T, CHUNK, lw, st_m,
                          lambda q, k, v: attn_big(q, k, v, kmeta, vmeta, bias_big, bias_meta_big, lam, gain_o,
                                                   out_scale, B, T, tq))

        init_s = {"gconv": state_gdn_conv[l], "S": state_gdn_S[l].astype(F32), "C": state_mlstm_C[l].astype(F32),
                  "n": state_mlstm_n[l].astype(F32), "m": state_mlstm_m[l].astype(F32), "fconv": state_ffn_conv[l]}
        cache = (cache_attn_k[l].reshape(Bs * P, W), cache_attn_v[l].reshape(Bs * P, W), bias_cache, tkc)
        xs, st_s = _layer(xs, Bs, Ls, Ls, lw, init_s,
                          lambda q, k, v: attn_small(q, k, v, cache, bias_new_s, lam, gain_o, out_scale, Bs, Ls))

        names = ("k", "v", "gconv", "S", "C", "n", "m", "fconv")
        p_states.append([jnp.concatenate([st_m[n_], st_b[n_]], axis=1) if n_ in ("k", "v") else st_b[n_]
                         for n_ in names])
        s_states.append([st_s[n_] for n_ in names])

    p_out = [jnp.stack([p_states[l][i] for l in range(depth)]) for i in range(8)]
    s_out = [jnp.stack([s_states[l][i] for l in range(depth)]) for i in range(8)]
    y_prompt = xb.reshape(B, T, D)
    y_sample = xs.reshape(Bs, Ls, D)
    return (y_prompt, y_sample, *p_out, *s_out)
```

```python
import functools
import math

import jax
import jax.numpy as jnp
from jax import lax
from jax.experimental import pallas as pl
from jax.experimental.pallas import tpu as pltpu

F32 = jnp.float32
BF16 = jnp.bfloat16

HEADS = 4
HD = 128
DA_DV = 2 * HD
CHUNK = 64
GDN_CONV = 4
FFN_CONV = 3
NUM_BUCKETS = 32
MAX_DISTANCE = 128
RMS_EPS = 1e-6
NEG_INF = -1e30
HALO = 8

QKV_COLS = 3 * HEADS * HD
Z_COLS = 3 * HEADS * DA_DV + 2 * QKV_COLS + 2 * HEADS * HD
SCAL_COLS = 128

LOG2E = math.log2(math.e)
Q_SCALE = HD ** -0.5 * LOG2E

VMEM_LIMIT = 56 * 1024 * 1024


def _pick(n, cands):
    for c in cands:
        if n % c == 0:
            return c
    raise ValueError(f"no tile for {n} in {cands}")


def _params(sem):
    return pltpu.CompilerParams(dimension_semantics=sem, vmem_limit_bytes=VMEM_LIMIT)


def _sigmoid(x):
    return 1.0 / (1.0 + jnp.exp(-x))


def _softplus(x):
    return jnp.maximum(x, 0.0) + jnp.log(1.0 + jnp.exp(-jnp.abs(x)))


def _dot_nt(a, b):
    return lax.dot_general(a, b, (((1,), (1,)), ((), ())), preferred_element_type=F32)


def _rms_matmul_kernel(x_ref, g_ref, w_ref, *rest, has_extra):
    if has_extra:
        ws_ref, o_ref, os_ref, xn_ref = rest
    else:
        o_ref, xn_ref = rest

    @pl.when(pl.program_id(1) == 0)
    def _():
        x = x_ref[...]
        ms = jnp.mean(x * x, axis=-1, keepdims=True)
        xn = (x * lax.rsqrt(ms + RMS_EPS) * g_ref[...]).astype(BF16)
        xn_ref[...] = xn
        if has_extra:
            os_ref[...] = jnp.dot(xn, ws_ref[...], preferred_element_type=F32)

    o_ref[...] = jnp.dot(xn_ref[...], w_ref[...], preferred_element_type=F32).astype(o_ref.dtype)


def rms_matmul(x, gain, w, w_extra=None, name="rms_matmul"):
    M, K = x.shape
    N = w.shape[1]
    tm = _pick(M, (1024, 512, 256, 128, 64, 32, 16, 8))
    tn = _pick(N, (512, 256, 128))
    has_extra = w_extra is not None
    in_specs = [pl.BlockSpec((tm, K), lambda i, j: (i, 0)),
                pl.BlockSpec((1, K), lambda i, j: (0, 0)),
                pl.BlockSpec((K, tn), lambda i, j: (0, j))]
    out_specs = [pl.BlockSpec((tm, tn), lambda i, j: (i, j))]
    out_shape = [jax.ShapeDtypeStruct((M, N), F32)]
    args = [x, gain.reshape(1, K), w]
    if has_extra:
        ne = w_extra.shape[1]
        in_specs.append(pl.BlockSpec((K, ne), lambda i, j: (0, 0)))
        out_specs.append(pl.BlockSpec((tm, ne), lambda i, j: (i, 0)))
        out_shape.append(jax.ShapeDtypeStruct((M, ne), F32))
        args.append(w_extra)
    outs = pl.pallas_call(
        functools.partial(_rms_matmul_kernel, has_extra=has_extra),
        grid=(M // tm, N // tn),
        in_specs=in_specs, out_specs=out_specs, out_shape=out_shape,
        scratch_shapes=[pltpu.VMEM((tm, K), BF16)],
        compiler_params=_params(("parallel", "arbitrary")),
        name=name,
    )(*args)
    return outs if has_extra else outs[0]


def _matmul_res_kernel(*refs, n_a):
    a_refs, w_refs = refs[:n_a], refs[n_a:2 * n_a]
    res_ref, o_ref = refs[2 * n_a], refs[2 * n_a + 1]
    acc = res_ref[...]
    for a_ref, w_ref in zip(a_refs, w_refs):
        acc = acc + jnp.dot(a_ref[...], w_ref[...], preferred_element_type=F32)
    o_ref[...] = acc


def matmul_res(a_list, w, res, name="matmul_res"):
    M, N = res.shape
    ktot = sum(a.shape[1] for a in a_list)
    tm = _pick(M, (1024, 512, 256, 128, 64, 32, 16, 8) if ktot <= 4096 else (512, 256, 128, 64, 32, 16, 8))
    tn = _pick(N, (512, 256, 128))
    in_specs, off = [], 0
    for a in a_list:
        in_specs.append(pl.BlockSpec((tm, a.shape[1]), lambda i, j: (i, 0)))
    for a in a_list:
        k = a.shape[1]
        assert off % k == 0
        in_specs.append(pl.BlockSpec((k, tn), lambda i, j, _o=off // k: (_o, j)))
        off += k
    in_specs.append(pl.BlockSpec((tm, tn), lambda i, j: (i, j)))
    return pl.pallas_call(
        functools.partial(_matmul_res_kernel, n_a=len(a_list)),
        grid=(M // tm, N // tn),
        in_specs=in_specs,
        out_specs=pl.BlockSpec((tm, tn), lambda i, j: (i, j)),
        out_shape=jax.ShapeDtypeStruct((M, N), F32),
        compiler_params=_params(("parallel", "arbitrary")),
        name=name,
    )(*a_list, *([w] * len(a_list)), res)


def _qk_prep_kernel(q_ref, k_ref, v_ref, gq_ref, gk_ref, qn_ref, kn32_ref, kn16_ref, v16_ref):
    gq = gq_ref[...]
    gk = gk_ref[...]
    for g in range(2 * HEADS):
        sl = slice(g * HD, (g + 1) * HD)
        q = q_ref[:, sl]
        k = k_ref[:, sl]
        qn = q * lax.rsqrt(jnp.mean(q * q, axis=-1, keepdims=True) + RMS_EPS) * gq
        kn = k * lax.rsqrt(jnp.mean(k * k, axis=-1, keepdims=True) + RMS_EPS) * gk
        qn_ref[:, sl] = (qn * Q_SCALE).astype(BF16)
        kn32_ref[:, sl] = kn
        kn16_ref[:, sl] = kn.astype(BF16)
    v16_ref[...] = v_ref[...].astype(BF16)


def qk_prep(z, gq, gk):
    M = z.shape[0]
    W = 2 * HEADS * HD
    tm = _pick(M, (512, 256, 128, 64, 32, 16, 8))
    col = lambda c: pl.BlockSpec((tm, W), lambda i, _c=c: (i, _c))
    vec = pl.BlockSpec((1, HD), lambda i: (0, 0))
    row = pl.BlockSpec((tm, W), lambda i: (i, 0))
    return pl.pallas_call(
        _qk_prep_kernel,
        grid=(M // tm,),
        in_specs=[col(0), col(1), col(2), vec, vec],
        out_specs=[row, row, row, row],
        out_shape=[jax.ShapeDtypeStruct((M, W), BF16), jax.ShapeDtypeStruct((M, W), F32),
                   jax.ShapeDtypeStruct((M, W), BF16), jax.ShapeDtypeStruct((M, W), BF16)],
        compiler_params=_params(("parallel",)),
        name="qk_prep",
    )(z, z, z, gq.reshape(1, HD), gk.reshape(1, HD))


def _softmax_update(mi, s, v, m_ref, l_ref, acc_ref):
    m_prev = m_ref[mi]
    m_new = jnp.maximum(m_prev, jnp.max(s, axis=-1, keepdims=True))
    alpha = jnp.exp2(m_prev - m_new)
    p = jnp.exp2(s - m_new)
    l_ref[mi] = alpha * l_ref[mi] + jnp.sum(p, axis=-1, keepdims=True)
    acc_ref[mi] = alpha * acc_ref[mi] + jnp.dot(p.astype(BF16), v, preferred_element_type=F32)
    m_ref[mi] = m_new


def _attn_init(m_ref, l_ref, acc_ref):
    m_ref[...] = jnp.full(m_ref.shape, -jnp.inf, F32)
    l_ref[...] = jnp.zeros(l_ref.shape, F32)
    acc_ref[...] = jnp.zeros(acc_ref.shape, F32)


def _attn_tile(q, k, v, bias, m_ref, l_ref, acc_ref):
    for mi in range(2):
        sl = slice(mi * HD, (mi + 1) * HD)
        s = _dot_nt(q[:, sl], k[:, sl]) + bias
        _softmax_update(mi, s, v, m_ref, l_ref, acc_ref)


def _attn_finish(lam, out_scale, g_ref, o_ref, l_ref, acc_ref):
    o = acc_ref[0] / l_ref[0] - lam * (acc_ref[1] / l_ref[1])
    on = o * lax.rsqrt(jnp.mean(o * o, axis=-1, keepdims=True) + RMS_EPS) * g_ref[...]
    o_ref[...] = (on * out_scale).astype(o_ref.dtype)


def _attn_scratch(tq):
    return [pltpu.VMEM((2, tq, 1), F32), pltpu.VMEM((2, tq, 1), F32), pltpu.VMEM((2, tq, DA_DV), F32)]


ATT_RB = 128
LANES = 128


def _rowblock_tile(q_ref, k_ref, v_ref, bias_ref, m_ref, l_ref, acc_ref):
    tq = q_ref.shape[0]
    tk = k_ref.shape[0]
    nb = tk // LANES
    nrb = tq // ATT_RB
    v = v_ref[...]

    def scores(r):
        rows = pl.ds(r * ATT_RB, ATT_RB)
        bias = bias_ref[0, 0, rows, :]
        return [_dot_nt(q_ref[rows, mi * HD:(mi + 1) * HD], k_ref[:, mi * HD:(mi + 1) * HD]) + bias
                for mi in range(2)]

    def update(r, s_pair):
        rows = pl.ds(r * ATT_RB, ATT_RB)
        for mi in range(2):
            s = s_pair[mi]
            blocks = [s[:, c * LANES:(c + 1) * LANES] for c in range(nb)]
            smax = blocks[0]
            for blk in blocks[1:]:
                smax = jnp.maximum(smax, blk)
            m_prev = m_ref[mi, rows, :]
            m_new = jnp.maximum(m_prev, jnp.max(smax, axis=-1, keepdims=True))
            alpha = jnp.exp2(m_prev - m_new)
            ps = [jnp.exp2(blk - m_new) for blk in blocks]
            psum = ps[0]
            for pb in ps[1:]:
                psum = psum + pb
            l_ref[mi, rows, :] = alpha * l_ref[mi, rows, :] + psum
            m_ref[mi, rows, :] = m_new
            p = jnp.concatenate(ps, axis=1).astype(BF16) if nb > 1 else ps[0].astype(BF16)
            pv = jnp.dot(p, v, preferred_element_type=F32)
            acc_ref[mi, rows, :] = jnp.concatenate([alpha, alpha], axis=1) * acc_ref[mi, rows, :] + pv

    s_next = scores(0)
    for r in range(nrb):
        s_cur = s_next
        if r + 1 < nrb:
            s_next = scores(r + 1)
        update(r, s_cur)


def _attn_big_kernel(qi_ref, kj_ref, lam_ref, q_ref, k_ref, v_ref, km_ref, vm_ref, bias_ref, bmeta_ref,
                     g_ref, o_ref, m_ref, l_ref, acc_ref, *, out_scale):
    step = pl.program_id(2)
    i = qi_ref[step]
    j = kj_ref[step]

    @pl.when(j == 0)
    def _():
        _attn_init(m_ref, l_ref, acc_ref)
        _rowblock_tile(q_ref, km_ref, vm_ref, bmeta_ref, m_ref, l_ref, acc_ref)

    _rowblock_tile(q_ref, k_ref, v_ref, bias_ref, m_ref, l_ref, acc_ref)

    @pl.when(j == i)
    def _():
        l0 = jnp.sum(l_ref[0], axis=-1, keepdims=True)
        l1 = jnp.sum(l_ref[1], axis=-1, keepdims=True)
        o = acc_ref[0] / l0 - lam_ref[0, 0] * (acc_ref[1] / l1)
        on = o * lax.rsqrt(jnp.mean(o * o, axis=-1, keepdims=True) + RMS_EPS) * g_ref[...]
        o_ref[...] = (on * out_scale).astype(o_ref.dtype)


def attn_big(qn, kn16, v16, kmeta, vmeta, bias, bmeta, lam, gain, out_scale, B, T, tq):
    nq = T // tq
    mp = kmeta.shape[0] // B
    pairs = [(i, j) for i in range(nq) for j in range(i + 1)]
    qi = jnp.asarray([p[0] for p in pairs], jnp.int32)
    kj = jnp.asarray([p[1] for p in pairs], jnp.int32)
    grid_spec = pltpu.PrefetchScalarGridSpec(
        num_scalar_prefetch=2,
        grid=(B, HEADS, len(pairs)),
        in_specs=[
            pl.BlockSpec(memory_space=pltpu.SMEM),
            pl.BlockSpec((tq, DA_DV), lambda b, h, s, qi, kj: (b * nq + qi[s], h)),
            pl.BlockSpec((tq, DA_DV), lambda b, h, s, qi, kj: (b * nq + kj[s], h)),
            pl.BlockSpec((tq, DA_DV), lambda b, h, s, qi, kj: (b * nq + kj[s], h)),
            pl.BlockSpec((mp, DA_DV), lambda b, h, s, qi, kj: (b, h)),
            pl.BlockSpec((mp, DA_DV), lambda b, h, s, qi, kj: (b, h)),
            pl.BlockSpec((1, 1, tq, tq), lambda b, h, s, qi, kj: (h, jnp.maximum(kj[s] - qi[s] + 2, 0), 0, 0)),
            pl.BlockSpec((1, 1, tq, mp), lambda b, h, s, qi, kj: (h, jnp.minimum(qi[s], 1), 0, 0)),
            pl.BlockSpec((1, DA_DV), lambda b, h, s, qi, kj: (0, 0)),
        ],
        out_specs=pl.BlockSpec((tq, DA_DV), lambda b, h, s, qi, kj: (b * nq + qi[s], h)),
        scratch_shapes=[pltpu.VMEM((2, tq, LANES), F32), pltpu.VMEM((2, tq, LANES), F32),
                        pltpu.VMEM((2, tq, DA_DV), F32)],
    )
    return pl.pallas_call(
        functools.partial(_attn_big_kernel, out_scale=out_scale),
        grid_spec=grid_spec,
        out_shape=jax.ShapeDtypeStruct((B * T, HEADS * DA_DV), BF16),
        compiler_params=_params(("parallel", "parallel", "arbitrary")),
        name="attn_prompt",
    )(qi, kj, lam, qn, kn16, v16, kmeta, vmeta, bias, bmeta, gain.reshape(1, DA_DV))


def _attn_small_kernel(lam_ref, q_ref, *rest, nc, out_scale):
    if nc > 0:
        ck_ref, cv_ref, bc_ref, k_ref, v_ref, bn_ref, g_ref, o_ref, m_ref, l_ref, acc_ref = rest
    else:
        k_ref, v_ref, bn_ref, g_ref, o_ref, m_ref, l_ref, acc_ref = rest
    j = pl.program_id(2)
    q = q_ref[...]

    @pl.when(j == 0)
    def _():
        _attn_init(m_ref, l_ref, acc_ref)

    if nc > 0:
        @pl.when(j < nc)
        def _():
            _attn_tile(q, ck_ref[...].astype(BF16), cv_ref[...].astype(BF16), bc_ref[0, 0], m_ref, l_ref, acc_ref)

    @pl.when(j == nc)
    def _():
        _attn_tile(q, k_ref[...], v_ref[...], bn_ref[0], m_ref, l_ref, acc_ref)
        _attn_finish(lam_ref[0, 0], out_scale, g_ref, o_ref, l_ref, acc_ref)


def attn_small(qn, kn16, v16, cache, bias_new, lam, gain, out_scale, B, L):
    W = HEADS * DA_DV
    blk = pl.BlockSpec((L, DA_DV), lambda b, h, j: (b, h))
    in_specs = [pl.BlockSpec(memory_space=pltpu.SMEM), blk]
    args = [lam, qn]
    nc = 0
    if cache is not None:
        ck, cv, bc, tk = cache
        nc = ck.shape[0] // B // tk
        cspec = pl.BlockSpec((tk, DA_DV), lambda b, h, j: (b * nc + jnp.minimum(j, nc - 1), h))
        in_specs += [cspec, cspec,
                     pl.BlockSpec((1, 1, L, tk), lambda b, h, j: (h, jnp.where(j >= nc - 1, 1, 0), 0, 0))]
        args += [ck, cv, bc]
    in_specs += [blk, blk, pl.BlockSpec((1, L, L), lambda b, h, j: (h, 0, 0)),
                 pl.BlockSpec((1, DA_DV), lambda b, h, j: (0, 0))]
    args += [kn16, v16, bias_new, gain.reshape(1, DA_DV)]
    return pl.pallas_call(
        functools.partial(_attn_small_kernel, nc=nc, out_scale=out_scale),
        grid=(B, HEADS, nc + 1),
        in_specs=in_specs,
        out_specs=blk,
        out_shape=jax.ShapeDtypeStruct((B * L, W), BF16),
        scratch_shapes=_attn_scratch(L),
        compiler_params=_params(("parallel", "parallel", "arbitrary")),
        name="attn_block",
    )(*args)


_NN = (((1,), (0,)), ((), ()))
_NT = (((1,), (1,)), ((), ()))
_TN = (((0,), (0,)), ((), ()))


def _dg(a, b, dn):
    return lax.dot_general(a, b, dn, preferred_element_type=F32)


def _bf16r(a):
    return a.astype(BF16).astype(F32)


def _mm3(a, b, dn=_NN):
    ah = _bf16r(a)
    al = a - ah
    bh = _bf16r(b)
    bl = b - bh
    return _dg(ah, bh, dn) + (_dg(ah, bl, dn) + _dg(al, bh, dn))


def _split3(a):
    hi = _bf16r(a)
    r1 = a - hi
    mid = _bf16r(r1)
    return hi, mid, r1 - mid


def _chunk_masks(L):
    row = lax.broadcasted_iota(jnp.int32, (L, L), 0)
    col = lax.broadcasted_iota(jnp.int32, (L, L), 1)
    return row, col


def _cumsum_cols_rows(x, tril, triu):
    hi, mid, lo = _split3(x)
    cols = _dg(tril, hi, _NN) + (_dg(tril, mid, _NN) + _dg(tril, lo, _NN))
    rows = _dg(hi, triu, _TN) + (_dg(mid, triu, _TN) + _dg(lo, triu, _TN))
    return cols, rows


def _gdn_prep_kernel(x_ref, halo_ref, prev_ref, sc_ref, cw_ref, nega_ref, dtb_ref,
                     w_ref, u_ref, qg_ref, kd_ref, qk_ref, gc_ref, xbuf, *, L, G):
    i = pl.program_id(1)
    tile = L * G
    xbuf[0:HALO, :] = jnp.where(i == 0, prev_ref[0], halo_ref[...])
    xbuf[HALO:HALO + tile, :] = x_ref[...]
    cw = cw_ref[...]
    y = xbuf[HALO - 3:HALO - 3 + tile, :] * cw[0:1, :]
    for t in range(1, GDN_CONV):
        y = y + xbuf[HALO - 3 + t:HALO - 3 + t + tile, :] * cw[t:t + 1, :]
    gcv = y * _sigmoid(y)

    row, col = _chunk_masks(L)
    incl = col <= row
    strict = col < row
    eye = (col == row).astype(F32)
    tril = incl.astype(F32)
    triu = (row <= col).astype(F32)
    sc = sc_ref[...]
    beta_all = _sigmoid(sc)
    g_all = nega_ref[...] * _softplus(sc + dtb_ref[...])
    nqk = HEADS * HD

    st = []
    for c in range(G):
        rows = slice(c * L, (c + 1) * L)
        g_cols, g_rows = _cumsum_cols_rows(g_all[rows], tril, triu)
        gc_ref[rows, :] = g_cols
        for h in range(HEADS):
            xq = gcv[rows, h * HD:(h + 1) * HD]
            xk = gcv[rows, nqk + h * HD:nqk + (h + 1) * HD]
            xv = gcv[rows, 2 * nqk + h * HD:2 * nqk + (h + 1) * HD]
            qn = xq * lax.rsqrt(jnp.sum(xq * xq, axis=-1, keepdims=True) + RMS_EPS) * (HD ** -0.5)
            kn = xk * lax.rsqrt(jnp.sum(xk * xk, axis=-1, keepdims=True) + RMS_EPS)
            beta = beta_all[rows, h:h + 1]
            gcol = g_cols[:, HEADS + h:HEADS + h + 1]
            grow = g_rows[HEADS + h:HEADS + h + 1, :]
            dec = jnp.where(incl, jnp.exp(jnp.where(incl, gcol - grow, 0.0)), 0.0)
            eg = jnp.exp(gcol)
            g_last = gcol[L - 1:L, :]
            st.append(dict(rows=rows, h=h, qn=qn, kn=kn, bv=beta * xv, bek=beta * eg * kn, beta=beta, dec=dec, eg=eg,
                           kd=kn * jnp.exp(g_last - gcol)))

    kk = [_mm3(s["kn"], s["kn"], _NT) for s in st]
    nmat = [jnp.where(strict, s["beta"] * k_ * s["dec"], 0.0) for s, k_ in zip(st, kk)]
    tinv = [eye - n for n in nmat]
    pw = [_mm3(n, n) for n in nmat]
    p = 2
    while True:
        tinv = [t + _mm3(t, q) for t, q in zip(tinv, pw)]
        p *= 2
        if p >= L:
            break
        pw = [_mm3(q, q) for q in pw]
    w_mat = [_mm3(t, s["bek"]) for t, s in zip(tinv, st)]
    u_mat = [_mm3(t, s["bv"]) for t, s in zip(tinv, st)]
    qk = [_mm3(s["qn"], s["kn"], _NT) * s["dec"] for s in st]
    for s, w_, u_, qk_ in zip(st, w_mat, u_mat, qk):
        rows, h = s["rows"], s["h"]
        sl = slice(h * HD, (h + 1) * HD)
        w_ref[rows, sl] = w_
        u_ref[rows, sl] = u_
        qg_ref[rows, sl] = s["eg"] * s["qn"]
        kd_ref[rows, sl] = s["kd"]
        qk_ref[rows, h * L:(h + 1) * L] = qk_


def _gdn_scan_kernel(w_ref, u_ref, qg_ref, kd_ref, qk_ref, gc_ref, gate_ref, s0_ref, gn_ref,
                     o_ref, sout_ref, s_scr, *, L, cps, B):
    i = pl.program_id(0)

    @pl.when(i == 0)
    def _():
        s_scr[...] = s0_ref[...]

    chains = [(b, h) for b in range(B) for h in range(HEADS)]
    gn = gn_ref[...]
    for c in range(cps):
        rows = slice(c * L, (c + 1) * L)
        last = (c + 1) * L - 1
        s_old = [s_scr[b, h] for b, h in chains]
        ws = [_mm3(w_ref[b, rows, h * HD:(h + 1) * HD], s_) for (b, h), s_ in zip(chains, s_old)]
        u = [u_ref[b, rows, h * HD:(h + 1) * HD] - w_ for (b, h), w_ in zip(chains, ws)]
        upd = [_mm3(kd_ref[b, rows, h * HD:(h + 1) * HD], u_, _TN) for (b, h), u_ in zip(chains, u)]
        for (b, h), s_, d_ in zip(chains, s_old, upd):
            ds = jnp.exp(gc_ref[b, last:last + 1, HEADS + h:HEADS + h + 1])
            s_scr[b, h] = ds * s_ + d_
        o1 = [_mm3(qg_ref[b, rows, h * HD:(h + 1) * HD], s_) for (b, h), s_ in zip(chains, s_old)]
        o2 = [_mm3(qk_ref[b, rows, h * L:(h + 1) * L], u_) for (b, h), u_ in zip(chains, u)]
        for (b, h), a_, b_ in zip(chains, o1, o2):
            sl = slice(h * HD, (h + 1) * HD)
            o = a_ + b_
            on = o * lax.rsqrt(jnp.mean(o * o, axis=-1, keepdims=True) + RMS_EPS) * gn
            gt = gate_ref[b, rows, sl]
            o_ref[b, rows, sl] = (on * (gt * _sigmoid(gt))).astype(o_ref.dtype)

    @pl.when(i == pl.num_programs(0) - 1)
    def _():
        sout_ref[...] = s_scr[...]


def gdn(z, sc, prev8, conv_w, neg_a, dt_bias, s0, gn, B, T, L):
    M = B * T
    nchunks = T // L
    G = _pick(nchunks, (2, 1))
    tile = L * G
    nt = T // tile
    hb = tile // HALO
    nega_vec = jnp.zeros((1, SCAL_COLS), F32).at[0, HEADS:2 * HEADS].set(neg_a)
    dtb_vec = jnp.zeros((1, SCAL_COLS), F32).at[0, HEADS:2 * HEADS].set(dt_bias.astype(F32))
    rowblk = lambda w: pl.BlockSpec((tile, w), lambda b, i: (b * nt + i, 0))
    vec = pl.BlockSpec((1, SCAL_COLS), lambda b, i: (0, 0))
    w_mat, u_mat, qg, kd, qk, gc = pl.pallas_call(
        functools.partial(_gdn_prep_kernel, L=L, G=G),
        grid=(B, nt),
        in_specs=[pl.BlockSpec((tile, QKV_COLS), lambda b, i: (b * nt + i, 2)),
                  pl.BlockSpec((HALO, QKV_COLS), lambda b, i: (jnp.maximum((b * nt + i) * hb - 1, 0), 2)),
                  pl.BlockSpec((1, HALO, QKV_COLS), lambda b, i: (b, 0, 0)),
                  rowblk(SCAL_COLS),
                  pl.BlockSpec((GDN_CONV, QKV_COLS), lambda b, i: (0, 0)),
                  vec, vec],
        out_specs=[rowblk(HEADS * HD)] * 4 + [rowblk(HEADS * L), rowblk(SCAL_COLS)],
        out_shape=[jax.ShapeDtypeStruct((M, HEADS * HD), F32)] * 4
                  + [jax.ShapeDtypeStruct((M, HEADS * L), F32), jax.ShapeDtypeStruct((M, SCAL_COLS), F32)],
        scratch_shapes=[pltpu.VMEM((tile + HALO, QKV_COLS), F32)],
        compiler_params=_params(("parallel", "parallel")),
        name="gdn_prep",
    )(z, z, prev8, sc, conv_w, nega_vec, dtb_vec)

    cps = _pick(nchunks, (4, 2, 1))
    rows = cps * L
    r3 = lambda a: a.reshape(B, T, a.shape[-1])
    blk = lambda w, cb=0: pl.BlockSpec((B, rows, w), lambda i, _c=cb: (0, i, _c))
    st = pl.BlockSpec((B, HEADS, HD, HD), lambda i: (0, 0, 0, 0))
    o, s_new = pl.pallas_call(
        functools.partial(_gdn_scan_kernel, L=L, cps=cps, B=B),
        grid=(nchunks // cps,),
        in_specs=[blk(HEADS * HD)] * 4 + [blk(HEADS * L), blk(SCAL_COLS), blk(HEADS * HD, 12), st,
                                          pl.BlockSpec((1, HD), lambda i: (0, 0))],
        out_specs=[blk(HEADS * HD), st],
        out_shape=[jax.ShapeDtypeStruct((B, T, HEADS * HD), BF16), jax.ShapeDtypeStruct((B, HEADS, HD, HD), F32)],
        scratch_shapes=[pltpu.VMEM((B, HEADS, HD, HD), F32)],
        compiler_params=_params(("arbitrary",)),
        name="gdn_scan",
    )(r3(w_mat), r3(u_mat), r3(qg), r3(kd), r3(qk), r3(gc), r3(z), s0, gn.reshape(1, HD))
    return o.reshape(M, HEADS * HD), s_new


def _mlstm_prep_kernel(x_ref, sc_ref, ib_ref, fb_ref, qk_ref, d_ref, st_ref, *, L, G):
    row, col = _chunk_masks(L)
    incl = col <= row
    eye = (col == row).astype(F32)
    tril = incl.astype(F32)
    triu = (row <= col).astype(F32)
    sc = sc_ref[...]
    ig_all = sc + ib_ref[...]
    lf_all = -_softplus(-(sc + fb_ref[...]))
    lane = lax.broadcasted_iota(jnp.int32, (L, SCAL_COLS), 1)
    nqk = HEADS * HD
    for c in range(G):
        rows = slice(c * L, (c + 1) * L)
        b_cols, b_rows = _cumsum_cols_rows(lf_all[rows], tril, triu)
        ih, im, il = _split3(ig_all[rows])
        ig_rows = _dg(ih, eye, _TN) + (_dg(im, eye, _TN) + _dg(il, eye, _TN))
        stats = jnp.where(lane >= 3 * HEADS, b_cols, ig_all[rows])
        for h in range(HEADS):
            bcol = b_cols[:, 3 * HEADS + h:3 * HEADS + h + 1]
            brow = b_rows[3 * HEADS + h:3 * HEADS + h + 1, :]
            igrow = ig_rows[2 * HEADS + h:2 * HEADS + h + 1, :]
            dmat = jnp.where(incl, bcol - brow + igrow, NEG_INF)
            stats = jnp.where(lane == h, jnp.max(dmat, axis=-1, keepdims=True), stats)
            q = x_ref[rows, h * HD:(h + 1) * HD]
            k = x_ref[rows, nqk + h * HD:nqk + (h + 1) * HD] * (HD ** -0.5)
            qk_ref[rows, h * L:(h + 1) * L] = _mm3(q, k, _NT)
            d_ref[rows, h * L:(h + 1) * L] = dmat
        st_ref[rows, :] = stats


def _mlstm_scan_kernel(x_ref, og_ref, qk_ref, d_ref, st_ref, c0_ref, n0_ref, m0_ref, gn_ref,
                       o_ref, cout_ref, nout_ref, mout_ref, c_scr, n_scr, m_scr, *, L, cps, B):
    i = pl.program_id(0)

    @pl.when(i == 0)
    def _():
        c_scr[...] = c0_ref[...]
        n_scr[...] = n0_ref[...]
        m_scr[...] = m0_ref[...]

    chains = [(b, h) for b in range(B) for h in range(HEADS)]
    gn = gn_ref[...]
    nqk = HEADS * HD
    for c in range(cps):
        rows = slice(c * L, (c + 1) * L)
        st = []
        for b, h in chains:
            q = x_ref[b, rows, h * HD:(h + 1) * HD]
            k = x_ref[b, rows, nqk + h * HD:nqk + (h + 1) * HD] * (HD ** -0.5)
            v = x_ref[b, rows, 2 * nqk + h * HD:2 * nqk + (h + 1) * HD]
            dmax = st_ref[b, rows, h:h + 1]
            ig = st_ref[b, rows, 2 * HEADS + h:2 * HEADS + h + 1]
            bcol = st_ref[b, rows, 3 * HEADS + h:3 * HEADS + h + 1]
            m_prev = m_scr[b, h:h + 1, 0:1]
            inter = bcol + m_prev
            m_t = jnp.maximum(inter, dmax)
            w_inter = jnp.exp(inter - m_t)
            qkw = qk_ref[b, rows, h * L:(h + 1) * L] * jnp.exp(d_ref[b, rows, h * L:(h + 1) * L] - m_t)
            m_new = m_t[L - 1:L, :]
            b_last = bcol[L - 1:L, :]
            kw = jnp.exp(b_last - bcol + ig - m_new) * k
            dec0 = jnp.exp(b_last + m_prev - m_new)
            st.append(dict(q=q, v=v, m_t=m_t, w_inter=w_inter, qkw=qkw, kw=kw, dec0=dec0, m_new=m_new,
                           c_old=c_scr[b, h], n_old=n_scr[b, h:h + 1, :]))
        upd = [_mm3(s["kw"], s["v"], _TN) for s in st]
        for (b, h), s, d_ in zip(chains, st, upd):
            c_scr[b, h] = s["dec0"] * s["c_old"] + d_
            n_scr[b, h:h + 1, :] = s["dec0"] * s["n_old"] + jnp.sum(s["kw"], axis=0, keepdims=True)
            m_scr[b, h:h + 1, :] = jnp.broadcast_to(s["m_new"], (1, HD))
        qc = [_mm3(s["q"], s["c_old"]) for s in st]
        pv = [_mm3(s["qkw"], s["v"]) for s in st]
        for (b, h), s, qc_, pv_ in zip(chains, st, qc, pv):
            sl = slice(h * HD, (h + 1) * HD)
            num = s["w_inter"] * qc_ + pv_
            den = (s["w_inter"] * jnp.sum(s["q"] * s["n_old"], axis=-1, keepdims=True)
                   + jnp.sum(s["qkw"], axis=-1, keepdims=True))
            hout = num / jnp.maximum(jnp.abs(den), jnp.exp(-s["m_t"]))
            on = hout * lax.rsqrt(jnp.mean(hout * hout, axis=-1, keepdims=True) + RMS_EPS) * gn
            o_ref[b, rows, sl] = (on * _sigmoid(og_ref[b, rows, sl])).astype(o_ref.dtype)

    @pl.when(i == pl.num_programs(0) - 1)
    def _():
        cout_ref[...] = c_scr[...]
        nout_ref[...] = n_scr[...]
        mout_ref[...] = m_scr[...]


def mlstm(z, sc, i_bias, f_bias, c0, n0p, m0p, gn, B, T, L):
    M = B * T
    nchunks = T // L
    G = _pick(nchunks, (4, 2, 1))
    tile = L * G
    nt = T // tile
    ib_vec = jnp.zeros((1, SCAL_COLS), F32).at[0, 2 * HEADS:3 * HEADS].set(i_bias.astype(F32))
    fb_vec = jnp.zeros((1, SCAL_COLS), F32).at[0, 3 * HEADS:4 * HEADS].set(f_bias.astype(F32))
    rowblk = lambda w, cb=0: pl.BlockSpec((tile, w), lambda b, i, _c=cb: (b * nt + i, _c))
    vec = pl.BlockSpec((1, SCAL_COLS), lambda b, i: (0, 0))
    qk, dm, stats = pl.pallas_call(
        functools.partial(_mlstm_prep_kernel, L=L, G=G),
        grid=(B, nt),
        in_specs=[rowblk(QKV_COLS, 3), rowblk(SCAL_COLS), vec, vec],
        out_specs=[rowblk(HEADS * L), rowblk(HEADS * L), rowblk(SCAL_COLS)],
        out_shape=[jax.ShapeDtypeStruct((M, HEADS * L), F32), jax.ShapeDtypeStruct((M, HEADS * L), F32),
                   jax.ShapeDtypeStruct((M, SCAL_COLS), F32)],
        compiler_params=_params(("parallel", "parallel")),
        name="mlstm_prep",
    )(z, sc, ib_vec, fb_vec)

    cps = _pick(nchunks, (4, 2, 1))
    rows = cps * L
    r3 = lambda a: a.reshape(B, T, a.shape[-1])
    blk = lambda w, cb=0: pl.BlockSpec((B, rows, w), lambda i, _c=cb: (0, i, _c))
    st = pl.BlockSpec((B, HEADS, HD, HD), lambda i: (0, 0, 0, 0))
    vec8 = pl.BlockSpec((B, 8, HD), lambda i: (0, 0, 0))
    o, c_new, n_new, m_new = pl.pallas_call(
        functools.partial(_mlstm_scan_kernel, L=L, cps=cps, B=B),
        grid=(nchunks // cps,),
        in_specs=[blk(QKV_COLS, 3), blk(HEADS * HD, 13), blk(HEADS * L), blk(HEADS * L), blk(SCAL_COLS),
                  st, vec8, vec8, pl.BlockSpec((1, HD), lambda i: (0, 0))],
        out_specs=[blk(HEADS * HD), st, vec8, vec8],
        out_shape=[jax.ShapeDtypeStruct((B, T, HEADS * HD), BF16), jax.ShapeDtypeStruct((B, HEADS, HD, HD), F32),
                   jax.ShapeDtypeStruct((B, 8, HD), F32), jax.ShapeDtypeStruct((B, 8, HD), F32)],
        scratch_shapes=[pltpu.VMEM((B, HEADS, HD, HD), F32), pltpu.VMEM((B, 8, HD), F32),
                        pltpu.VMEM((B, 8, HD), F32)],
        compiler_params=_params(("arbitrary",)),
        name="mlstm_scan",
    )(r3(z), r3(z), r3(qk), r3(dm), r3(stats), c0, n0p, m0p, gn.reshape(1, HD))
    return o.reshape(M, HEADS * HD), c_new, n_new, m_new


def _ffn_act_kernel(gate_ref, val_ref, halo_ref, prev_ref, cw_ref, o_ref, buf, *, tm):
    i = pl.program_id(1)
    buf[0:HALO, :] = jnp.where(i == 0, prev_ref[0], halo_ref[...])
    buf[HALO:HALO + tm, :] = gate_ref[...]
    cw = cw_ref[...]
    y = buf[HALO - 2:HALO - 2 + tm, :] * cw[0:1, :]
    for t in range(1, FFN_CONV):
        y = y + buf[HALO - 2 + t:HALO - 2 + t + tm, :] * cw[t:t + 1, :]
    o_ref[...] = (y * _sigmoid(y) * val_ref[...]).astype(o_ref.dtype)


def ffn_act(up, prev8, conv_w, B, T):
    M = B * T
    Fd = conv_w.shape[1]
    tm = _pick(T, (512, 256, 128, 64, 32, 16, 8))
    tf = _pick(Fd, (512, 256, 128))
    nt, nf = T // tm, Fd // tf
    hb = tm // HALO
    return pl.pallas_call(
        functools.partial(_ffn_act_kernel, tm=tm),
        grid=(B, nt, nf),
        in_specs=[pl.BlockSpec((tm, tf), lambda b, i, c: (b * nt + i, c)),
                  pl.BlockSpec((tm, tf), lambda b, i, c: (b * nt + i, nf + c)),
                  pl.BlockSpec((HALO, tf), lambda b, i, c: (jnp.maximum((b * nt + i) * hb - 1, 0), c)),
                  pl.BlockSpec((1, HALO, tf), lambda b, i, c: (b, 0, c)),
                  pl.BlockSpec((FFN_CONV, tf), lambda b, i, c: (0, c))],
        out_specs=pl.BlockSpec((tm, tf), lambda b, i, c: (b * nt + i, c)),
        out_shape=jax.ShapeDtypeStruct((M, Fd), BF16),
        scratch_shapes=[pltpu.VMEM((tm + HALO, tf), F32)],
        compiler_params=_params(("parallel", "parallel", "parallel")),
        name="ffn_act",
    )(up, up, up, prev8, conv_w)


def _t5_bucket(rel):
    half = NUM_BUCKETS // 2
    exact = half // 2
    n = jnp.abs(rel)
    nf = jnp.maximum(n, 1).astype(F32)
    large = exact + (jnp.log(nf / exact) / math.log(MAX_DISTANCE / exact) * (half - exact)).astype(jnp.int32)
    large = jnp.minimum(large, half - 1)
    return jnp.where(rel > 0, half, 0) + jnp.where(n < exact, n, large)


def _rel_bias_vec(rel_bias, rel):
    bucket = _t5_bucket(rel)
    rb = rel_bias.astype(F32) * LOG2E
    out = jnp.zeros((rb.shape[1], rel.shape[0]), F32)
    for kb in range(NUM_BUCKETS):
        out = jnp.where((bucket == kb)[None], rb[kb][:, None], out)
    return out


def _rel_bias_const(rel_bias, rel, R, C):
    v = _rel_bias_vec(rel_bias, jnp.full((1,), rel, jnp.int32))
    return jnp.broadcast_to(v[:, :, None], (v.shape[0], R, C))


def _rel_bias_toeplitz(rel_bias, R, C, off):
    n = R + C
    d = jnp.concatenate([jnp.arange(0, C + 1, dtype=jnp.int32), jnp.arange(-(R - 1), 0, dtype=jnp.int32)])
    vec = _rel_bias_vec(rel_bias, d + off)
    H = vec.shape[0]
    flat = jnp.broadcast_to(vec[:, None, :], (H, R, n)).reshape(H, R * n)
    return flat[:, :R * (n - 1)].reshape(H, R, n - 1)[:, :, :C]


def _pad_rows(a, rows):
    return jnp.pad(a, ((0, 0), (rows - a.shape[1], 0), (0, 0)))


def _layer(x, B, T, L, lw, init, attn_fn):
    z, sc = rms_matmul(x, lw["norm_mix"], lw["w_in"], lw["w_in_scal"], name="in_proj")
    qn, kn32, kn16, v16 = qk_prep(z, lw["da_q_norm"], lw["da_k_norm"])
    o_da = attn_fn(qn, kn16, v16)
    o_g, s_new = gdn(z, sc, _pad_rows(init["gconv"], HALO), lw["gdn_conv_w"], lw["gdn_neg_a"], lw["gdn_dt_bias"],
                     init["S"], lw["gdn_out_norm"], B, T, L)
    n0p = jnp.pad(init["n"], ((0, 0), (0, 8 - HEADS), (0, 0)))
    m0p = jnp.pad(jnp.broadcast_to(init["m"][:, :, None], (B, HEADS, HD)), ((0, 0), (0, 8 - HEADS), (0, 0)))
    o_m, c_new, n_new, m_new = mlstm(z, sc, lw["ml_i_bias"], lw["ml_f_bias"], init["C"], n0p, m0p,
                                     lw["ml_out_norm"], B, T, L)
    x1 = matmul_res([o_da, o_g, o_m], lw["w_out"], x, name="out_proj")
    up = rms_matmul(x1, lw["norm_ffn"], lw["w_up"], name="ffn_up")
    hmid = ffn_act(up, _pad_rows(init["fconv"], HALO), lw["ffn_conv_w"], B, T)
    x2 = matmul_res([hmid], lw["w_down"], x1, name="ffn_down")
    Fd = lw["ffn_conv_w"].shape[1]
    z3 = z.reshape(B, T, Z_COLS)
    st = {
        "k": kn32.reshape(B, T, HEADS, 2, HD),
        "v": z3[:, :, 2 * HEADS * DA_DV:3 * HEADS * DA_DV].reshape(B, T, HEADS, DA_DV),
        "gconv": z3[:, T - (GDN_CONV - 1):, 3 * HEADS * DA_DV:3 * HEADS * DA_DV + QKV_COLS],
        "S": s_new, "C": c_new, "n": n_new[:, :HEADS], "m": m_new[:, :HEADS, 0],
        "fconv": up.reshape(B, T, 2 * Fd)[:, T - (FFN_CONV - 1):, :Fd],
        "k16": kn16, "v16": v16,
    }
    return x2, st


def kernel(x_prompt, x_sample, cache_attn_k, cache_attn_v, state_gdn_conv, state_gdn_S, state_mlstm_C, state_mlstm_n, state_mlstm_m, state_ffn_conv, meta_tokens, rel_bias, norm_mix, norm_ffn, w_in, w_out, da_q_norm, da_k_norm, da_lq1, da_lk1, da_lq2, da_lk2, da_out_norm, gdn_conv_w, gdn_A_log, gdn_dt_bias, gdn_out_norm, ml_i_bias, ml_f_bias, ml_out_norm, ffn_w_up, ffn_conv_w, ffn_w_down):
    B, T, D = x_prompt.shape
    Bs, Ls, _ = x_sample.shape
    depth = w_in.shape[0]
    P = cache_attn_k.shape[2]
    NM = meta_tokens.shape[0]
    Fd = ffn_conv_w.shape[-1]
    W = HEADS * DA_DV
    assert T % CHUNK == 0 and Ls <= CHUNK and NM <= CHUNK and NM % 8 == 0 and Ls % 8 == 0

    tq = _pick(T, (512, 256, 128, 64))
    tkc = _pick(P, (512, 256, 128))
    assert tq >= 96 or T == tq, "far prompt tiles must lie past the last distinct relative-position bucket"
    assert tkc >= 96 or P == tkc
    MP = 128

    r = jnp.arange(tq, dtype=jnp.int32)[:, None]
    c = jnp.arange(tq, dtype=jnp.int32)[None, :]
    diag = jnp.where((c // CHUNK <= r // CHUNK)[None], _rel_bias_toeplitz(rel_bias, tq, tq, 0), NEG_INF)
    sub = _rel_bias_toeplitz(rel_bias, tq, tq, -tq)
    far = _rel_bias_const(rel_bias, -2 * tq, tq, tq)
    bias_big = jnp.stack([far, sub, diag], axis=1)
    cm = jnp.arange(MP, dtype=jnp.int32)[None, :]
    bm0 = _rel_bias_toeplitz(rel_bias, tq, MP, -NM)
    bm1 = _rel_bias_const(rel_bias, -2 * tq, tq, MP)
    bias_meta_big = jnp.where((cm < NM)[None, None], jnp.stack([bm0, bm1], axis=1), NEG_INF)
    bias_new_s = _rel_bias_toeplitz(rel_bias, Ls, Ls, 0)
    bias_cache = jnp.stack([_rel_bias_const(rel_bias, -2 * tkc, Ls, tkc),
                            _rel_bias_toeplitz(rel_bias, Ls, tkc, -tkc)], axis=1)
    bias_new_m = _rel_bias_toeplitz(rel_bias, NM, NM, 0)

    sizes = (W, W, W, QKV_COLS, HEADS, HEADS, HEADS * HD, QKV_COLS, HEADS, HEADS, HEADS * HD)
    offs = [0]
    for s_ in sizes:
        offs.append(offs[-1] + s_)
    seg = lambda w, i: w[:, offs[i]:offs[i + 1]]

    xm = jnp.broadcast_to(meta_tokens.astype(F32)[None], (B, NM, D)).reshape(B * NM, D)
    xb = x_prompt.reshape(B * T, D)
    xs = x_sample.reshape(Bs * Ls, D)
    p_states, s_states = [], []
    for l in range(depth):
        wl = w_in[l]
        w_main = jnp.concatenate([seg(wl, 0), seg(wl, 1), seg(wl, 2), seg(wl, 3), seg(wl, 7), seg(wl, 6), seg(wl, 10)],
                                 axis=1).astype(BF16)
        w_scal = jnp.concatenate([seg(wl, 4), seg(wl, 5), seg(wl, 8), seg(wl, 9),
                                  jnp.zeros((D, SCAL_COLS - 4 * HEADS), F32)], axis=1).astype(BF16)
        lw = {
            "norm_mix": norm_mix[l], "norm_ffn": norm_ffn[l], "w_in": w_main, "w_in_scal": w_scal,
            "w_out": w_out[l].astype(BF16), "w_up": ffn_w_up[l].astype(BF16), "w_down": ffn_w_down[l].astype(BF16),
            "da_q_norm": da_q_norm[l], "da_k_norm": da_k_norm[l], "gdn_conv_w": gdn_conv_w[l],
            "gdn_neg_a": -jnp.exp(gdn_A_log[l].astype(F32)), "gdn_dt_bias": gdn_dt_bias[l], "gdn_out_norm": gdn_out_norm[l],
            "ml_i_bias": ml_i_bias[l], "ml_f_bias": ml_f_bias[l], "ml_out_norm": ml_out_norm[l],
            "ffn_conv_w": ffn_conv_w[l],
        }
        lam_init = 0.8 - 0.6 * math.exp(-0.3 * l)
        lam = (jnp.exp(jnp.sum(da_lq1[l].astype(F32) * da_lk1[l].astype(F32)))
               - jnp.exp(jnp.sum(da_lq2[l].astype(F32) * da_lk2[l].astype(F32))) + lam_init).reshape(1, 1)
        out_scale = 1.0 - lam_init
        gain_o = da_out_norm[l]

        zero = {"gconv": jnp.zeros((B, GDN_CONV - 1, QKV_COLS), F32), "S": jnp.zeros((B, HEADS, HD, HD), F32),
                "C": jnp.zeros((B, HEADS, HD, HD), F32), "n": jnp.zeros((B, HEADS, HD), F32),
                "m": jnp.zeros((B, HEADS), F32), "fconv": jnp.zeros((B, FFN_CONV - 1, Fd), F32)}
        xm, st_m = _layer(xm, B, NM, NM, lw, zero,
                          lambda q, k, v: attn_small(q, k, v, None, bias_new_m, lam, gain_o, out_scale, B, NM))

        kmeta = jnp.pad(st_m["k16"].reshape(B, NM, W), ((0, 0), (0, MP - NM), (0, 0))).reshape(B * MP, W)
        vmeta = jnp.pad(st_m["v16"].reshape(B, NM, W), ((0, 0), (0, MP - NM), (0, 0))).reshape(B * MP, W)
        xb, st_b = _layer(xb, B, T, CHUNK, lw, st_m,
                          lambda q, k, v: attn_big(q, k, v, kmeta, vmeta, bias_big, bias_meta_big, lam, gain_o,
                                                   out_scale, B, T, tq))

        init_s = {"gconv": state_gdn_conv[l], "S": state_gdn_S[l].astype(F32), "C": state_mlstm_C[l].astype(F32),
                  "n": state_mlstm_n[l].astype(F32), "m": state_mlstm_m[l].astype(F32), "fconv": state_ffn_conv[l]}
        cache = (cache_attn_k[l].reshape(Bs * P, W), cache_attn_v[l].reshape(Bs * P, W), bias_cache, tkc)
        xs, st_s = _layer(xs, Bs, Ls, Ls, lw, init_s,
                          lambda q, k, v: attn_small(q, k, v, cache, bias_new_s, lam, gain_o, out_scale, Bs, Ls))

        names = ("k", "v", "gconv", "S", "C", "n", "m", "fconv")
        p_states.append([jnp.concatenate([st_m[n_], st_b[n_]], axis=1) if n_ in ("k", "v") else st_b[n_]
                         for n_ in names])
        s_states.append([st_s[n_] for n_ in names])

    p_out = [jnp.stack([p_states[l][i] for l in range(depth)]) for i in range(8)]
    s_out = [jnp.stack([s_states[l][i] for l in range(depth)]) for i in range(8)]
    y_prompt = xb.reshape(B, T, D)
    y_sample = xs.reshape(Bs, Ls, D)
    return (y_prompt, y_sample, *p_out, *s_out)
```

```python
import functools
import math

import jax
import jax.numpy as jnp
from jax import lax
from jax.experimental import pallas as pl
from jax.experimental.pallas import tpu as pltpu

F32 = jnp.float32
BF16 = jnp.bfloat16

HEADS = 4
HD = 128
DA_DV = 2 * HD
CHUNK = 64
GDN_CONV = 4
FFN_CONV = 3
NUM_BUCKETS = 32
MAX_DISTANCE = 128
RMS_EPS = 1e-6
NEG_INF = -1e30
HALO = 8

QKV_COLS = 3 * HEADS * HD
Z_COLS = 3 * HEADS * DA_DV + 2 * QKV_COLS + 2 * HEADS * HD
SCAL_COLS = 128

LOG2E = math.log2(math.e)
Q_SCALE = HD ** -0.5 * LOG2E

VMEM_LIMIT = 56 * 1024 * 1024


def _pick(n, cands):
    for c in cands:
        if n % c == 0:
            return c
    raise ValueError(f"no tile for {n} in {cands}")


def _params(sem):
    return pltpu.CompilerParams(dimension_semantics=sem, vmem_limit_bytes=VMEM_LIMIT)


def _sigmoid(x):
    return 1.0 / (1.0 + jnp.exp(-x))


def _softplus(x):
    return jnp.maximum(x, 0.0) + jnp.log(1.0 + jnp.exp(-jnp.abs(x)))


def _dot_nt(a, b):
    return lax.dot_general(a, b, (((1,), (1,)), ((), ())), preferred_element_type=F32)


def _rms_matmul_kernel(x_ref, g_ref, w_ref, *rest, has_extra):
    if has_extra:
        ws_ref, o_ref, os_ref, xn_ref = rest
    else:
        o_ref, xn_ref = rest

    @pl.when(pl.program_id(1) == 0)
    def _():
        x = x_ref[...]
        ms = jnp.mean(x * x, axis=-1, keepdims=True)
        xn = (x * lax.rsqrt(ms + RMS_EPS) * g_ref[...]).astype(BF16)
        xn_ref[...] = xn
        if has_extra:
            os_ref[...] = jnp.dot(xn, ws_ref[...], preferred_element_type=F32)

    o_ref[...] = jnp.dot(xn_ref[...], w_ref[...], preferred_element_type=F32).astype(o_ref.dtype)


def rms_matmul(x, gain, w, w_extra=None, name="rms_matmul"):
    M, K = x.shape
    N = w.shape[1]
    tm = _pick(M, (1024, 512, 256, 128, 64, 32, 16, 8))
    tn = _pick(N, (512, 256, 128))
    has_extra = w_extra is not None
    in_specs = [pl.BlockSpec((tm, K), lambda i, j: (i, 0)),
                pl.BlockSpec((1, K), lambda i, j: (0, 0)),
                pl.BlockSpec((K, tn), lambda i, j: (0, j))]
    out_specs = [pl.BlockSpec((tm, tn), lambda i, j: (i, j))]
    out_shape = [jax.ShapeDtypeStruct((M, N), F32)]
    args = [x, gain.reshape(1, K), w]
    if has_extra:
        ne = w_extra.shape[1]
        in_specs.append(pl.BlockSpec((K, ne), lambda i, j: (0, 0)))
        out_specs.append(pl.BlockSpec((tm, ne), lambda i, j: (i, 0)))
        out_shape.append(jax.ShapeDtypeStruct((M, ne), F32))
        args.append(w_extra)
    outs = pl.pallas_call(
        functools.partial(_rms_matmul_kernel, has_extra=has_extra),
        grid=(M // tm, N // tn),
        in_specs=in_specs, out_specs=out_specs, out_shape=out_shape,
        scratch_shapes=[pltpu.VMEM((tm, K), BF16)],
        compiler_params=_params(("parallel", "arbitrary")),
        name=name,
    )(*args)
    return outs if has_extra else outs[0]


def _matmul_res_kernel(*refs, n_a):
    a_refs, w_refs = refs[:n_a], refs[n_a:2 * n_a]
    res_ref, o_ref = refs[2 * n_a], refs[2 * n_a + 1]
    acc = res_ref[...]
    for a_ref, w_ref in zip(a_refs, w_refs):
        acc = acc + jnp.dot(a_ref[...], w_ref[...], preferred_element_type=F32)
    o_ref[...] = acc


def matmul_res(a_list, w, res, name="matmul_res"):
    M, N = res.shape
    ktot = sum(a.shape[1] for a in a_list)
    tm = _pick(M, (1024, 512, 256, 128, 64, 32, 16, 8) if ktot <= 4096 else (512, 256, 128, 64, 32, 16, 8))
    tn = _pick(N, (512, 256, 128))
    in_specs, off = [], 0
    for a in a_list:
        in_specs.append(pl.BlockSpec((tm, a.shape[1]), lambda i, j: (i, 0)))
    for a in a_list:
        k = a.shape[1]
        assert off % k == 0
        in_specs.append(pl.BlockSpec((k, tn), lambda i, j, _o=off // k: (_o, j)))
        off += k
    in_specs.append(pl.BlockSpec((tm, tn), lambda i, j: (i, j)))
    return pl.pallas_call(
        functools.partial(_matmul_res_kernel, n_a=len(a_list)),
        grid=(M // tm, N // tn),
        in_specs=in_specs,
        out_specs=pl.BlockSpec((tm, tn), lambda i, j: (i, j)),
        out_shape=jax.ShapeDtypeStruct((M, N), F32),
        compiler_params=_params(("parallel", "arbitrary")),
        name=name,
    )(*a_list, *([w] * len(a_list)), res)


def _qk_prep_kernel(q_ref, k_ref, v_ref, gq_ref, gk_ref, qn_ref, kn32_ref, kn16_ref, v16_ref, *rest):
    gq = gq_ref[...]
    gk = gk_ref[...]
    for g in range(2 * HEADS):
        sl = slice(g * HD, (g + 1) * HD)
        q = q_ref[:, sl]
        k = k_ref[:, sl]
        qn = q * lax.rsqrt(jnp.mean(q * q, axis=-1, keepdims=True) + RMS_EPS) * gq
        kn = k * lax.rsqrt(jnp.mean(k * k, axis=-1, keepdims=True) + RMS_EPS) * gk
        qn_ref[:, sl] = (qn * Q_SCALE).astype(BF16)
        kn32_ref[:, sl] = kn
        kn16_ref[:, sl] = kn.astype(BF16)
        if rest:
            rest[0][sl, :] = kn.T.astype(BF16)
    v16_ref[...] = v_ref[...].astype(BF16)


def qk_prep(z, gq, gk, want_kt=False):
    M = z.shape[0]
    W = 2 * HEADS * HD
    tm = _pick(M, (512, 256, 128, 64, 32, 16, 8))
    col = lambda c: pl.BlockSpec((tm, W), lambda i, _c=c: (i, _c))
    vec = pl.BlockSpec((1, HD), lambda i: (0, 0))
    row = pl.BlockSpec((tm, W), lambda i: (i, 0))
    out_specs = [row, row, row, row]
    out_shape = [jax.ShapeDtypeStruct((M, W), BF16), jax.ShapeDtypeStruct((M, W), F32),
                 jax.ShapeDtypeStruct((M, W), BF16), jax.ShapeDtypeStruct((M, W), BF16)]
    if want_kt:
        out_specs.append(pl.BlockSpec((W, tm), lambda i: (0, i)))
        out_shape.append(jax.ShapeDtypeStruct((W, M), BF16))
    return pl.pallas_call(
        _qk_prep_kernel,
        grid=(M // tm,),
        in_specs=[col(0), col(1), col(2), vec, vec],
        out_specs=out_specs,
        out_shape=out_shape,
        compiler_params=_params(("parallel",)),
        name="qk_prep",
    )(z, z, z, gq.reshape(1, HD), gk.reshape(1, HD))


def _softmax_update(mi, s, v, m_ref, l_ref, acc_ref):
    m_prev = m_ref[mi]
    m_new = jnp.maximum(m_prev, jnp.max(s, axis=-1, keepdims=True))
    alpha = jnp.exp2(m_prev - m_new)
    p = jnp.exp2(s - m_new)
    l_ref[mi] = alpha * l_ref[mi] + jnp.sum(p, axis=-1, keepdims=True)
    acc_ref[mi] = alpha * acc_ref[mi] + jnp.dot(p.astype(BF16), v, preferred_element_type=F32)
    m_ref[mi] = m_new


def _attn_init(m_ref, l_ref, acc_ref):
    m_ref[...] = jnp.full(m_ref.shape, -jnp.inf, F32)
    l_ref[...] = jnp.zeros(l_ref.shape, F32)
    acc_ref[...] = jnp.zeros(acc_ref.shape, F32)


def _attn_tile(q, k, v, bias, m_ref, l_ref, acc_ref):
    for mi in range(2):
        sl = slice(mi * HD, (mi + 1) * HD)
        s = _dot_nt(q[:, sl], k[:, sl]) + bias
        _softmax_update(mi, s, v, m_ref, l_ref, acc_ref)


def _attn_finish(lam, out_scale, g_ref, o_ref, l_ref, acc_ref):
    o = acc_ref[0] / l_ref[0] - lam * (acc_ref[1] / l_ref[1])
    on = o * lax.rsqrt(jnp.mean(o * o, axis=-1, keepdims=True) + RMS_EPS) * g_ref[...]
    o_ref[...] = (on * out_scale).astype(o_ref.dtype)


def _attn_scratch(tq):
    return [pltpu.VMEM((2, tq, 1), F32), pltpu.VMEM((2, tq, 1), F32), pltpu.VMEM((2, tq, DA_DV), F32)]


LANES = 128
ATT_TQ = (1024, 512, 256, 128, 64)
ATT_TK = (1024, 512, 256, 128, 64)
ATT_RB = 256


def _rowblock_tile(q_ref, kt_ref, v_ref, bias_ref, shift, m_ref, l_ref, acc_ref, rb):
    tq = q_ref.shape[0]
    tk = kt_ref.shape[1]
    nb = tk // LANES
    rb = min(rb, tq)
    nrb = tq // rb
    v = v_ref[...]

    def scores(r):
        rows = pl.ds(r * rb, rb)
        out = [jnp.dot(q_ref[rows, mi * HD:(mi + 1) * HD], kt_ref[mi * HD:(mi + 1) * HD, :],
                       preferred_element_type=F32) for mi in range(2)]
        if bias_ref is not None:
            bias = bias_ref[0, 0, rows, :]
            out = [s + bias for s in out]
        return out

    def update(r, s_pair):
        rows = pl.ds(r * rb, rb)
        for mi in range(2):
            s = s_pair[mi]
            blocks = [s[:, c * LANES:(c + 1) * LANES] for c in range(nb)]
            smax = blocks[0]
            for blk in blocks[1:]:
                smax = jnp.maximum(smax, blk)
            m_prev = m_ref[mi, rows, :]
            if shift is not None:
                m_prev = m_prev - shift
            m_new = jnp.maximum(m_prev, jnp.max(smax, axis=-1, keepdims=True))
            alpha = jnp.exp2(m_prev - m_new)
            ps = [jnp.exp2(blk - m_new) for blk in blocks]
            psum = ps[0]
            for pb in ps[1:]:
                psum = psum + pb
            l_ref[mi, rows, :] = alpha * l_ref[mi, rows, :] + psum
            m_ref[mi, rows, :] = m_new if shift is None else m_new + shift
            p = jnp.concatenate(ps, axis=1).astype(BF16) if nb > 1 else ps[0].astype(BF16)
            pv = jnp.dot(p, v, preferred_element_type=F32)
            acc_ref[mi, rows, :] = jnp.concatenate([alpha, alpha], axis=1) * acc_ref[mi, rows, :] + pv

    s_next = scores(0)
    for r in range(nrb):
        s_cur = s_next
        if r + 1 < nrb:
            s_next = scores(r + 1)
        update(r, s_cur)


def _attn_big_kernel(qi_ref, kj_ref, lam_ref, far_ref, q_ref, k_ref, v_ref, km_ref, vm_ref, bias_ref, bmeta_ref,
                     g_ref, o_ref, m_ref, l_ref, acc_ref, *, out_scale, ratio, rb):
    step = pl.program_id(2)
    i = qi_ref[step]
    j = kj_ref[step]
    is_far = j - ratio * i < -1

    @pl.when(j == 0)
    def _():
        _attn_init(m_ref, l_ref, acc_ref)
        _rowblock_tile(q_ref, km_ref, vm_ref, bmeta_ref, None, m_ref, l_ref, acc_ref, rb)

    @pl.when(is_far)
    def _():
        _rowblock_tile(q_ref, k_ref, v_ref, None, far_ref[pl.program_id(1)], m_ref, l_ref, acc_ref, rb)

    @pl.when(jnp.logical_not(is_far))
    def _():
        _rowblock_tile(q_ref, k_ref, v_ref, bias_ref, None, m_ref, l_ref, acc_ref, rb)

    @pl.when(j == ratio * (i + 1) - 1)
    def _():
        l0 = jnp.sum(l_ref[0], axis=-1, keepdims=True)
        l1 = jnp.sum(l_ref[1], axis=-1, keepdims=True)
        o = acc_ref[0] / l0 - lam_ref[0, 0] * (acc_ref[1] / l1)
        on = o * lax.rsqrt(jnp.mean(o * o, axis=-1, keepdims=True) + RMS_EPS) * g_ref[...]
        o_ref[...] = (on * out_scale).astype(o_ref.dtype)


def attn_big(qn, knt16, v16, kmeta_t, vmeta, bias, far, bmeta, lam, gain, out_scale, B, T, tq, tk, rb):
    nq, nk = T // tq, T // tk
    ratio = tq // tk
    mp = vmeta.shape[0] // B
    pairs = [(i, j) for i in range(nq) for j in range(ratio * (i + 1))]
    qi = jnp.asarray([p[0] for p in pairs], jnp.int32)
    kj = jnp.asarray([p[1] for p in pairs], jnp.int32)
    grid_spec = pltpu.PrefetchScalarGridSpec(
        num_scalar_prefetch=2,
        grid=(B, HEADS, len(pairs)),
        in_specs=[
            pl.BlockSpec(memory_space=pltpu.SMEM),
            pl.BlockSpec(memory_space=pltpu.SMEM),
            pl.BlockSpec((tq, DA_DV), lambda b, h, s, qi, kj: (b * nq + qi[s], h)),
            pl.BlockSpec((DA_DV, tk), lambda b, h, s, qi, kj: (h, b * nk + kj[s])),
            pl.BlockSpec((tk, DA_DV), lambda b, h, s, qi, kj: (b * nk + kj[s], h)),
            pl.BlockSpec((DA_DV, mp), lambda b, h, s, qi, kj: (b * HEADS + h, 0)),
            pl.BlockSpec((mp, DA_DV), lambda b, h, s, qi, kj: (b, h)),
            pl.BlockSpec((1, 1, tq, tk),
                         lambda b, h, s, qi, kj: (h, jnp.maximum(kj[s] - ratio * qi[s] + 1, 0), 0, 0)),
            pl.BlockSpec((1, 1, tq, mp), lambda b, h, s, qi, kj: (h, jnp.minimum(qi[s], 1), 0, 0)),
            pl.BlockSpec((1, DA_DV), lambda b, h, s, qi, kj: (0, 0)),
        ],
        out_specs=pl.BlockSpec((tq, DA_DV), lambda b, h, s, qi, kj: (b * nq + qi[s], h)),
        scratch_shapes=[pltpu.VMEM((2, tq, LANES), F32), pltpu.VMEM((2, tq, LANES), F32),
                        pltpu.VMEM((2, tq, DA_DV), F32)],
    )
    return pl.pallas_call(
        functools.partial(_attn_big_kernel, out_scale=out_scale, ratio=ratio, rb=rb),
        grid_spec=grid_spec,
        out_shape=jax.ShapeDtypeStruct((B * T, HEADS * DA_DV), BF16),
        compiler_params=_params(("parallel", "parallel", "arbitrary")),
        name="attn_prompt",
    )(qi, kj, lam, far, qn, knt16, v16, kmeta_t, vmeta, bias, bmeta, gain.reshape(1, DA_DV))


def _attn_small_kernel(lam_ref, q_ref, *rest, nc, out_scale):
    if nc > 0:
        ck_ref, cv_ref, bc_ref, k_ref, v_ref, bn_ref, g_ref, o_ref, m_ref, l_ref, acc_ref = rest
    else:
        k_ref, v_ref, bn_ref, g_ref, o_ref, m_ref, l_ref, acc_ref = rest
    j = pl.program_id(2)
    q = q_ref[...]

    @pl.when(j == 0)
    def _():
        _attn_init(m_ref, l_ref, acc_ref)

    if nc > 0:
        @pl.when(j < nc)
        def _():
            _attn_tile(q, ck_ref[...].astype(BF16), cv_ref[...].astype(BF16), bc_ref[0, 0], m_ref, l_ref, acc_ref)

    @pl.when(j == nc)
    def _():
        _attn_tile(q, k_ref[...], v_ref[...], bn_ref[0], m_ref, l_ref, acc_ref)
        _attn_finish(lam_ref[0, 0], out_scale, g_ref, o_ref, l_ref, acc_ref)


def attn_small(qn, kn16, v16, cache, bias_new, lam, gain, out_scale, B, L):
    W = HEADS * DA_DV
    blk = pl.BlockSpec((L, DA_DV), lambda b, h, j: (b, h))
    in_specs = [pl.BlockSpec(memory_space=pltpu.SMEM), blk]
    args = [lam, qn]
    nc = 0
    if cache is not None:
        ck, cv, bc, tk = cache
        nc = ck.shape[0] // B // tk
        cspec = pl.BlockSpec((tk, DA_DV), lambda b, h, j: (b * nc + jnp.minimum(j, nc - 1), h))
        in_specs += [cspec, cspec,
                     pl.BlockSpec((1, 1, L, tk), lambda b, h, j: (h, jnp.where(j >= nc - 1, 1, 0), 0, 0))]
        args += [ck, cv, bc]
    in_specs += [blk, blk, pl.BlockSpec((1, L, L), lambda b, h, j: (h, 0, 0)),
                 pl.BlockSpec((1, DA_DV), lambda b, h, j: (0, 0))]
    args += [kn16, v16, bias_new, gain.reshape(1, DA_DV)]
    return pl.pallas_call(
        functools.partial(_attn_small_kernel, nc=nc, out_scale=out_scale),
        grid=(B, HEADS, nc + 1),
        in_specs=in_specs,
        out_specs=blk,
        out_shape=jax.ShapeDtypeStruct((B * L, W), BF16),
        scratch_shapes=_attn_scratch(L),
        compiler_params=_params(("parallel", "parallel", "arbitrary")),
        name="attn_block",
    )(*args)


_NN = (((1,), (0,)), ((), ()))
_NT = (((1,), (1,)), ((), ()))
_TN = (((0,), (0,)), ((), ()))


def _dg(a, b, dn):
    return lax.dot_general(a, b, dn, preferred_element_type=F32)


def _bf16r(a):
    return a.astype(BF16).astype(F32)


def _mm3(a, b, dn=_NN):
    ah = _bf16r(a)
    al = a - ah
    bh = _bf16r(b)
    bl = b - bh
    return _dg(ah, bh, dn) + (_dg(ah, bl, dn) + _dg(al, bh, dn))


def _split3(a):
    hi = _bf16r(a)
    r1 = a - hi
    mid = _bf16r(r1)
    return hi, mid, r1 - mid


def _chunk_masks(L):
    row = lax.broadcasted_iota(jnp.int32, (L, L), 0)
    col = lax.broadcasted_iota(jnp.int32, (L, L), 1)
    return row, col


def _cumsum_cols_rows(x, tril, triu):
    hi, mid, lo = _split3(x)
    cols = _dg(tril, hi, _NN) + (_dg(tril, mid, _NN) + _dg(tril, lo, _NN))
    rows = _dg(hi, triu, _TN) + (_dg(mid, triu, _TN) + _dg(lo, triu, _TN))
    return cols, rows


def _gdn_prep_kernel(x_ref, halo_ref, prev_ref, sc_ref, cw_ref, nega_ref, dtb_ref,
                     w_ref, u_ref, qg_ref, kd_ref, qk_ref, gc_ref, xbuf, *, L, G):
    i = pl.program_id(1)
    tile = L * G
    xbuf[0:HALO, :] = jnp.where(i == 0, prev_ref[0], halo_ref[...])
    xbuf[HALO:HALO + tile, :] = x_ref[...]
    cw = cw_ref[...]
    y = xbuf[HALO - 3:HALO - 3 + tile, :] * cw[0:1, :]
    for t in range(1, GDN_CONV):
        y = y + xbuf[HALO - 3 + t:HALO - 3 + t + tile, :] * cw[t:t + 1, :]
    gcv = y * _sigmoid(y)

    row, col = _chunk_masks(L)
    incl = col <= row
    strict = col < row
    eye = (col == row).astype(F32)
    tril = incl.astype(F32)
    triu = (row <= col).astype(F32)
    sc = sc_ref[...]
    beta_all = _sigmoid(sc)
    g_all = nega_ref[...] * _softplus(sc + dtb_ref[...])
    nqk = HEADS * HD

    st = []
    for c in range(G):
        rows = slice(c * L, (c + 1) * L)
        g_cols, g_rows = _cumsum_cols_rows(g_all[rows], tril, triu)
        gc_ref[rows, :] = g_cols
        for h in range(HEADS):
            xq = gcv[rows, h * HD:(h + 1) * HD]
            xk = gcv[rows, nqk + h * HD:nqk + (h + 1) * HD]
            xv = gcv[rows, 2 * nqk + h * HD:2 * nqk + (h + 1) * HD]
            qn = xq * lax.rsqrt(jnp.sum(xq * xq, axis=-1, keepdims=True) + RMS_EPS) * (HD ** -0.5)
            kn = xk * lax.rsqrt(jnp.sum(xk * xk, axis=-1, keepdims=True) + RMS_EPS)
            beta = beta_all[rows, h:h + 1]
            gcol = g_cols[:, HEADS + h:HEADS + h + 1]
            grow = g_rows[HEADS + h:HEADS + h + 1, :]
            dec = jnp.where(incl, jnp.exp(jnp.where(incl, gcol - grow, 0.0)), 0.0)
            eg = jnp.exp(gcol)
            g_last = gcol[L - 1:L, :]
            st.append(dict(rows=rows, h=h, qn=qn, kn=kn, bv=beta * xv, bek=beta * eg * kn, beta=beta, dec=dec, eg=eg,
                           kd=kn * jnp.exp(g_last - gcol)))

    kk = [_mm3(s["kn"], s["kn"], _NT) for s in st]
    nmat = [jnp.where(strict, s["beta"] * k_ * s["dec"], 0.0) for s, k_ in zip(st, kk)]
    tinv = [eye - n for n in nmat]
    pw = [_mm3(n, n) for n in nmat]
    p = 2
    while True:
        tinv = [t + _mm3(t, q) for t, q in zip(tinv, pw)]
        p *= 2
        if p >= L:
            break
        pw = [_mm3(q, q) for q in pw]
    w_mat = [_mm3(t, s["bek"]) for t, s in zip(tinv, st)]
    u_mat = [_mm3(t, s["bv"]) for t, s in zip(tinv, st)]
    qk = [_mm3(s["qn"], s["kn"], _NT) * s["dec"] for s in st]
    for s, w_, u_, qk_ in zip(st, w_mat, u_mat, qk):
        rows, h = s["rows"], s["h"]
        sl = slice(h * HD, (h + 1) * HD)
        w_ref[rows, sl] = w_
        u_ref[rows, sl] = u_
        qg_ref[rows, sl] = s["eg"] * s["qn"]
        kd_ref[rows, sl] = s["kd"]
        qk_ref[rows, h * L:(h + 1) * L] = qk_


def _gdn_scan_kernel(w_ref, u_ref, qg_ref, kd_ref, qk_ref, gc_ref, gate_ref, s0_ref, gn_ref,
                     o_ref, sout_ref, s_scr, *, L, cps, B):
    i = pl.program_id(0)

    @pl.when(i == 0)
    def _():
        s_scr[...] = s0_ref[...]

    chains = [(b, h) for b in range(B) for h in range(HEADS)]
    gn = gn_ref[...]
    for c in range(cps):
        rows = slice(c * L, (c + 1) * L)
        last = (c + 1) * L - 1
        s_old = [s_scr[b, h] for b, h in chains]
        ws = [_mm3(w_ref[b, rows, h * HD:(h + 1) * HD], s_) for (b, h), s_ in zip(chains, s_old)]
        u = [u_ref[b, rows, h * HD:(h + 1) * HD] - w_ for (b, h), w_ in zip(chains, ws)]
        upd = [_mm3(kd_ref[b, rows, h * HD:(h + 1) * HD], u_, _TN) for (b, h), u_ in zip(chains, u)]
        for (b, h), s_, d_ in zip(chains, s_old, upd):
            ds = jnp.exp(gc_ref[b, last:last + 1, HEADS + h:HEADS + h + 1])
            s_scr[b, h] = ds * s_ + d_
        o1 = [_mm3(qg_ref[b, rows, h * HD:(h + 1) * HD], s_) for (b, h), s_ in zip(chains, s_old)]
        o2 = [_mm3(qk_ref[b, rows, h * L:(h + 1) * L], u_) for (b, h), u_ in zip(chains, u)]
        for (b, h), a_, b_ in zip(chains, o1, o2):
            sl = slice(h * HD, (h + 1) * HD)
            o = a_ + b_
            on = o * lax.rsqrt(jnp.mean(o * o, axis=-1, keepdims=True) + RMS_EPS) * gn
            gt = gate_ref[b, rows, sl]
            o_ref[b, rows, sl] = (on * (gt * _sigmoid(gt))).astype(o_ref.dtype)

    @pl.when(i == pl.num_programs(0) - 1)
    def _():
        sout_ref[...] = s_scr[...]


def gdn(z, sc, prev8, conv_w, neg_a, dt_bias, s0, gn, B, T, L):
    M = B * T
    nchunks = T // L
    G = _pick(nchunks, (2, 1))
    tile = L * G
    nt = T // tile
    hb = tile // HALO
    nega_vec = jnp.zeros((1, SCAL_COLS), F32).at[0, HEADS:2 * HEADS].set(neg_a)
    dtb_vec = jnp.zeros((1, SCAL_COLS), F32).at[0, HEADS:2 * HEADS].set(dt_bias.astype(F32))
    rowblk = lambda w: pl.BlockSpec((tile, w), lambda b, i: (b * nt + i, 0))
    vec = pl.BlockSpec((1, SCAL_COLS), lambda b, i: (0, 0))
    w_mat, u_mat, qg, kd, qk, gc = pl.pallas_call(
        functools.partial(_gdn_prep_kernel, L=L, G=G),
        grid=(B, nt),
        in_specs=[pl.BlockSpec((tile, QKV_COLS), lambda b, i: (b * nt + i, 2)),
                  pl.BlockSpec((HALO, QKV_COLS), lambda b, i: (jnp.maximum((b * nt + i) * hb - 1, 0), 2)),
                  pl.BlockSpec((1, HALO, QKV_COLS), lambda b, i: (b, 0, 0)),
                  rowblk(SCAL_COLS),
                  pl.BlockSpec((GDN_CONV, QKV_COLS), lambda b, i: (0, 0)),
                  vec, vec],
        out_specs=[rowblk(HEADS * HD)] * 4 + [rowblk(HEADS * L), rowblk(SCAL_COLS)],
        out_shape=[jax.ShapeDtypeStruct((M, HEADS * HD), F32)] * 4
                  + [jax.ShapeDtypeStruct((M, HEADS * L), F32), jax.ShapeDtypeStruct((M, SCAL_COLS), F32)],
        scratch_shapes=[pltpu.VMEM((tile + HALO, QKV_COLS), F32)],
        compiler_params=_params(("parallel", "parallel")),
        name="gdn_prep",
    )(z, z, prev8, sc, conv_w, nega_vec, dtb_vec)

    cps = _pick(nchunks, (4, 2, 1))
    rows = cps * L
    r3 = lambda a: a.reshape(B, T, a.shape[-1])
    blk = lambda w, cb=0: pl.BlockSpec((B, rows, w), lambda i, _c=cb: (0, i, _c))
    st = pl.BlockSpec((B, HEADS, HD, HD), lambda i: (0, 0, 0, 0))
    o, s_new = pl.pallas_call(
        functools.partial(_gdn_scan_kernel, L=L, cps=cps, B=B),
        grid=(nchunks // cps,),
        in_specs=[blk(HEADS * HD)] * 4 + [blk(HEADS * L), blk(SCAL_COLS), blk(HEADS * HD, 12), st,
                                          pl.BlockSpec((1, HD), lambda i: (0, 0))],
        out_specs=[blk(HEADS * HD), st],
        out_shape=[jax.ShapeDtypeStruct((B, T, HEADS * HD), BF16), jax.ShapeDtypeStruct((B, HEADS, HD, HD), F32)],
        scratch_shapes=[pltpu.VMEM((B, HEADS, HD, HD), F32)],
        compiler_params=_params(("arbitrary",)),
        name="gdn_scan",
    )(r3(w_mat), r3(u_mat), r3(qg), r3(kd), r3(qk), r3(gc), r3(z), s0, gn.reshape(1, HD))
    return o.reshape(M, HEADS * HD), s_new


def _mlstm_prep_kernel(x_ref, sc_ref, ib_ref, fb_ref, qk_ref, d_ref, st_ref, *, L, G):
    row, col = _chunk_masks(L)
    incl = col <= row
    eye = (col == row).astype(F32)
    tril = incl.astype(F32)
    triu = (row <= col).astype(F32)
    sc = sc_ref[...]
    ig_all = sc + ib_ref[...]
    lf_all = -_softplus(-(sc + fb_ref[...]))
    lane = lax.broadcasted_iota(jnp.int32, (L, SCAL_COLS), 1)
    nqk = HEADS * HD
    for c in range(G):
        rows = slice(c * L, (c + 1) * L)
        b_cols, b_rows = _cumsum_cols_rows(lf_all[rows], tril, triu)
        ih, im, il = _split3(ig_all[rows])
        ig_rows = _dg(ih, eye, _TN) + (_dg(im, eye, _TN) + _dg(il, eye, _TN))
        stats = jnp.where(lane >= 3 * HEADS, b_cols, ig_all[rows])
        for h in range(HEADS):
            bcol = b_cols[:, 3 * HEADS + h:3 * HEADS + h + 1]
            brow = b_rows[3 * HEADS + h:3 * HEADS + h + 1, :]
            igrow = ig_rows[2 * HEADS + h:2 * HEADS + h + 1, :]
            dmat = jnp.where(incl, bcol - brow + igrow, NEG_INF)
            stats = jnp.where(lane == h, jnp.max(dmat, axis=-1, keepdims=True), stats)
            q = x_ref[rows, h * HD:(h + 1) * HD]
            k = x_ref[rows, nqk + h * HD:nqk + (h + 1) * HD] * (HD ** -0.5)
            qk_ref[rows, h * L:(h + 1) * L] = _mm3(q, k, _NT)
            d_ref[rows, h * L:(h + 1) * L] = dmat
        st_ref[rows, :] = stats


def _mlstm_scan_kernel(x_ref, og_ref, qk_ref, d_ref, st_ref, c0_ref, n0_ref, m0_ref, gn_ref,
                       o_ref, cout_ref, nout_ref, mout_ref, c_scr, n_scr, m_scr, *, L, cps, B):
    i = pl.program_id(0)

    @pl.when(i == 0)
    def _():
        c_scr[...] = c0_ref[...]
        n_scr[...] = n0_ref[...]
        m_scr[...] = m0_ref[...]

    chains = [(b, h) for b in range(B) for h in range(HEADS)]
    gn = gn_ref[...]
    nqk = HEADS * HD
    for c in range(cps):
        rows = slice(c * L, (c + 1) * L)
        st = []
        for b, h in chains:
            q = x_ref[b, rows, h * HD:(h + 1) * HD]
            k = x_ref[b, rows, nqk + h * HD:nqk + (h + 1) * HD] * (HD ** -0.5)
            v = x_ref[b, rows, 2 * nqk + h * HD:2 * nqk + (h + 1) * HD]
            dmax = st_ref[b, rows, h:h + 1]
            ig = st_ref[b, rows, 2 * HEADS + h:2 * HEADS + h + 1]
            bcol = st_ref[b, rows, 3 * HEADS + h:3 * HEADS + h + 1]
            m_prev = m_scr[b, h:h + 1, 0:1]
            inter = bcol + m_prev
            m_t = jnp.maximum(inter, dmax)
            w_inter = jnp.exp(inter - m_t)
            qkw = qk_ref[b, rows, h * L:(h + 1) * L] * jnp.exp(d_ref[b, rows, h * L:(h + 1) * L] - m_t)
            m_new = m_t[L - 1:L, :]
            b_last = bcol[L - 1:L, :]
            kw = jnp.exp(b_last - bcol + ig - m_new) * k
            dec0 = jnp.exp(b_last + m_prev - m_new)
            st.append(dict(q=q, v=v, m_t=m_t, w_inter=w_inter, qkw=qkw, kw=kw, dec0=dec0, m_new=m_new,
                           c_old=c_scr[b, h], n_old=n_scr[b, h:h + 1, :]))
        upd = [_mm3(s["kw"], s["v"], _TN) for s in st]
        for (b, h), s, d_ in zip(chains, st, upd):
            c_scr[b, h] = s["dec0"] * s["c_old"] + d_
            n_scr[b, h:h + 1, :] = s["dec0"] * s["n_old"] + jnp.sum(s["kw"], axis=0, keepdims=True)
            m_scr[b, h:h + 1, :] = jnp.broadcast_to(s["m_new"], (1, HD))
        qc = [_mm3(s["q"], s["c_old"]) for s in st]
        pv = [_mm3(s["qkw"], s["v"]) for s in st]
        for (b, h), s, qc_, pv_ in zip(chains, st, qc, pv):
            sl = slice(h * HD, (h + 1) * HD)
            num = s["w_inter"] * qc_ + pv_
            den = (s["w_inter"] * jnp.sum(s["q"] * s["n_old"], axis=-1, keepdims=True)
                   + jnp.sum(s["qkw"], axis=-1, keepdims=True))
            hout = num / jnp.maximum(jnp.abs(den), jnp.exp(-s["m_t"]))
            on = hout * lax.rsqrt(jnp.mean(hout * hout, axis=-1, keepdims=True) + RMS_EPS) * gn
            o_ref[b, rows, sl] = (on * _sigmoid(og_ref[b, rows, sl])).astype(o_ref.dtype)

    @pl.when(i == pl.num_programs(0) - 1)
    def _():
        cout_ref[...] = c_scr[...]
        nout_ref[...] = n_scr[...]
        mout_ref[...] = m_scr[...]


def mlstm(z, sc, i_bias, f_bias, c0, n0p, m0p, gn, B, T, L):
    M = B * T
    nchunks = T // L
    G = _pick(nchunks, (4, 2, 1))
    tile = L * G
    nt = T // tile
    ib_vec = jnp.zeros((1, SCAL_COLS), F32).at[0, 2 * HEADS:3 * HEADS].set(i_bias.astype(F32))
    fb_vec = jnp.zeros((1, SCAL_COLS), F32).at[0, 3 * HEADS:4 * HEADS].set(f_bias.astype(F32))
    rowblk = lambda w, cb=0: pl.BlockSpec((tile, w), lambda b, i, _c=cb: (b * nt + i, _c))
    vec = pl.BlockSpec((1, SCAL_COLS), lambda b, i: (0, 0))
    qk, dm, stats = pl.pallas_call(
        functools.partial(_mlstm_prep_kernel, L=L, G=G),
        grid=(B, nt),
        in_specs=[rowblk(QKV_COLS, 3), rowblk(SCAL_COLS), vec, vec],
        out_specs=[rowblk(HEADS * L), rowblk(HEADS * L), rowblk(SCAL_COLS)],
        out_shape=[jax.ShapeDtypeStruct((M, HEADS * L), F32), jax.ShapeDtypeStruct((M, HEADS * L), F32),
                   jax.ShapeDtypeStruct((M, SCAL_COLS), F32)],
        compiler_params=_params(("parallel", "parallel")),
        name="mlstm_prep",
    )(z, sc, ib_vec, fb_vec)

    cps = _pick(nchunks, (4, 2, 1))
    rows = cps * L
    r3 = lambda a: a.reshape(B, T, a.shape[-1])
    blk = lambda w, cb=0: pl.BlockSpec((B, rows, w), lambda i, _c=cb: (0, i, _c))
    st = pl.BlockSpec((B, HEADS, HD, HD), lambda i: (0, 0, 0, 0))
    vec8 = pl.BlockSpec((B, 8, HD), lambda i: (0, 0, 0))
    o, c_new, n_new, m_new = pl.pallas_call(
        functools.partial(_mlstm_scan_kernel, L=L, cps=cps, B=B),
        grid=(nchunks // cps,),
        in_specs=[blk(QKV_COLS, 3), blk(HEADS * HD, 13), blk(HEADS * L), blk(HEADS * L), blk(SCAL_COLS),
                  st, vec8, vec8, pl.BlockSpec((1, HD), lambda i: (0, 0))],
        out_specs=[blk(HEADS * HD), st, vec8, vec8],
        out_shape=[jax.ShapeDtypeStruct((B, T, HEADS * HD), BF16), jax.ShapeDtypeStruct((B, HEADS, HD, HD), F32),
                   jax.ShapeDtypeStruct((B, 8, HD), F32), jax.ShapeDtypeStruct((B, 8, HD), F32)],
        scratch_shapes=[pltpu.VMEM((B, HEADS, HD, HD), F32), pltpu.VMEM((B, 8, HD), F32),
                        pltpu.VMEM((B, 8, HD), F32)],
        compiler_params=_params(("arbitrary",)),
        name="mlstm_scan",
    )(r3(z), r3(z), r3(qk), r3(dm), r3(stats), c0, n0p, m0p, gn.reshape(1, HD))
    return o.reshape(M, HEADS * HD), c_new, n_new, m_new


def _ffn_act_kernel(gate_ref, val_ref, halo_ref, prev_ref, cw_ref, o_ref, buf, *, tm):
    i = pl.program_id(1)
    buf[0:HALO, :] = jnp.where(i == 0, prev_ref[0], halo_ref[...])
    buf[HALO:HALO + tm, :] = gate_ref[...]
    cw = cw_ref[...]
    y = buf[HALO - 2:HALO - 2 + tm, :] * cw[0:1, :]
    for t in range(1, FFN_CONV):
        y = y + buf[HALO - 2 + t:HALO - 2 + t + tm, :] * cw[t:t + 1, :]
    o_ref[...] = (y * _sigmoid(y) * val_ref[...]).astype(o_ref.dtype)


def ffn_act(up, prev8, conv_w, B, T):
    M = B * T
    Fd = conv_w.shape[1]
    tm = _pick(T, (512, 256, 128, 64, 32, 16, 8))
    tf = _pick(Fd, (512, 256, 128))
    nt, nf = T // tm, Fd // tf
    hb = tm // HALO
    return pl.pallas_call(
        functools.partial(_ffn_act_kernel, tm=tm),
        grid=(B, nt, nf),
        in_specs=[pl.BlockSpec((tm, tf), lambda b, i, c: (b * nt + i, c)),
                  pl.BlockSpec((tm, tf), lambda b, i, c: (b * nt + i, nf + c)),
                  pl.BlockSpec((HALO, tf), lambda b, i, c: (jnp.maximum((b * nt + i) * hb - 1, 0), c)),
                  pl.BlockSpec((1, HALO, tf), lambda b, i, c: (b, 0, c)),
                  pl.BlockSpec((FFN_CONV, tf), lambda b, i, c: (0, c))],
        out_specs=pl.BlockSpec((tm, tf), lambda b, i, c: (b * nt + i, c)),
        out_shape=jax.ShapeDtypeStruct((M, Fd), BF16),
        scratch_shapes=[pltpu.VMEM((tm + HALO, tf), F32)],
        compiler_params=_params(("parallel", "parallel", "parallel")),
        name="ffn_act",
    )(up, up, up, prev8, conv_w)


def _t5_bucket(rel):
    half = NUM_BUCKETS // 2
    exact = half // 2
    n = jnp.abs(rel)
    nf = jnp.maximum(n, 1).astype(F32)
    large = exact + (jnp.log(nf / exact) / math.log(MAX_DISTANCE / exact) * (half - exact)).astype(jnp.int32)
    large = jnp.minimum(large, half - 1)
    return jnp.where(rel > 0, half, 0) + jnp.where(n < exact, n, large)


def _rel_bias_vec(rel_bias, rel):
    bucket = _t5_bucket(rel)
    rb = rel_bias.astype(F32) * LOG2E
    out = jnp.zeros((rb.shape[1], rel.shape[0]), F32)
    for kb in range(NUM_BUCKETS):
        out = jnp.where((bucket == kb)[None], rb[kb][:, None], out)
    return out


def _rel_bias_const(rel_bias, rel, R, C):
    v = _rel_bias_vec(rel_bias, jnp.full((1,), rel, jnp.int32))
    return jnp.broadcast_to(v[:, :, None], (v.shape[0], R, C))


def _rel_bias_toeplitz(rel_bias, R, C, off):
    n = R + C
    d = jnp.concatenate([jnp.arange(0, C + 1, dtype=jnp.int32), jnp.arange(-(R - 1), 0, dtype=jnp.int32)])
    vec = _rel_bias_vec(rel_bias, d + off)
    H = vec.shape[0]
    flat = jnp.broadcast_to(vec[:, None, :], (H, R, n)).reshape(H, R * n)
    return flat[:, :R * (n - 1)].reshape(H, R, n - 1)[:, :, :C]


def _pad_rows(a, rows):
    return jnp.pad(a, ((0, 0), (rows - a.shape[1], 0), (0, 0)))


def _layer(x, B, T, L, lw, init, attn_fn, want_kt=False):
    z, sc = rms_matmul(x, lw["norm_mix"], lw["w_in"], lw["w_in_scal"], name="in_proj")
    qn, kn32, kn16, v16, *knt = qk_prep(z, lw["da_q_norm"], lw["da_k_norm"], want_kt)
    o_da = attn_fn(qn, knt[0] if want_kt else kn16, v16)
    o_g, s_new = gdn(z, sc, _pad_rows(init["gconv"], HALO), lw["gdn_conv_w"], lw["gdn_neg_a"], lw["gdn_dt_bias"],
                     init["S"], lw["gdn_out_norm"], B, T, L)
    n0p = jnp.pad(init["n"], ((0, 0), (0, 8 - HEADS), (0, 0)))
    m0p = jnp.pad(jnp.broadcast_to(init["m"][:, :, None], (B, HEADS, HD)), ((0, 0), (0, 8 - HEADS), (0, 0)))
    o_m, c_new, n_new, m_new = mlstm(z, sc, lw["ml_i_bias"], lw["ml_f_bias"], init["C"], n0p, m0p,
                                     lw["ml_out_norm"], B, T, L)
    x1 = matmul_res([o_da, o_g, o_m], lw["w_out"], x, name="out_proj")
    up = rms_matmul(x1, lw["norm_ffn"], lw["w_up"], name="ffn_up")
    hmid = ffn_act(up, _pad_rows(init["fconv"], HALO), lw["ffn_conv_w"], B, T)
    x2 = matmul_res([hmid], lw["w_down"], x1, name="ffn_down")
    Fd = lw["ffn_conv_w"].shape[1]
    z3 = z.reshape(B, T, Z_COLS)
    st = {
        "k": kn32.reshape(B, T, HEADS, 2, HD),
        "v": z3[:, :, 2 * HEADS * DA_DV:3 * HEADS * DA_DV].reshape(B, T, HEADS, DA_DV),
        "gconv": z3[:, T - (GDN_CONV - 1):, 3 * HEADS * DA_DV:3 * HEADS * DA_DV + QKV_COLS],
        "S": s_new, "C": c_new, "n": n_new[:, :HEADS], "m": m_new[:, :HEADS, 0],
        "fconv": up.reshape(B, T, 2 * Fd)[:, T - (FFN_CONV - 1):, :Fd],
        "k16": kn16, "v16": v16,
    }
    return x2, st


def kernel(x_prompt, x_sample, cache_attn_k, cache_attn_v, state_gdn_conv, state_gdn_S, state_mlstm_C, state_mlstm_n, state_mlstm_m, state_ffn_conv, meta_tokens, rel_bias, norm_mix, norm_ffn, w_in, w_out, da_q_norm, da_k_norm, da_lq1, da_lk1, da_lq2, da_lk2, da_out_norm, gdn_conv_w, gdn_A_log, gdn_dt_bias, gdn_out_norm, ml_i_bias, ml_f_bias, ml_out_norm, ffn_w_up, ffn_conv_w, ffn_w_down):
    B, T, D = x_prompt.shape
    Bs, Ls, _ = x_sample.shape
    depth = w_in.shape[0]
    P = cache_attn_k.shape[2]
    NM = meta_tokens.shape[0]
    Fd = ffn_conv_w.shape[-1]
    W = HEADS * DA_DV
    assert T % CHUNK == 0 and Ls <= CHUNK and NM <= CHUNK and NM % 8 == 0 and Ls % 8 == 0

    tq = _pick(T, ATT_TQ)
    tk = _pick(tq, ATT_TK)
    tkc = _pick(P, (512, 256, 128))
    assert tk >= 96 or T == tk, "far prompt tiles must lie past the last distinct relative-position bucket"
    assert tkc >= 96 or P == tkc
    MP = 128

    r = jnp.arange(tq, dtype=jnp.int32)[:, None]
    c = jnp.arange(tk, dtype=jnp.int32)[None, :]
    tiles = []
    for d in range(-1, tq // tk):
        visible = ((c + tk * d) // CHUNK <= r // CHUNK)[None]
        tiles.append(jnp.where(visible, _rel_bias_toeplitz(rel_bias, tq, tk, tk * d), NEG_INF))
    bias_big = jnp.stack(tiles, axis=1)
    bias_far = _rel_bias_vec(rel_bias, jnp.full((1,), -2 * tk, jnp.int32))[:, 0]
    cm = jnp.arange(MP, dtype=jnp.int32)[None, :]
    bm0 = _rel_bias_toeplitz(rel_bias, tq, MP, -NM)
    bm1 = _rel_bias_const(rel_bias, -2 * tq, tq, MP)
    bias_meta_big = jnp.where((cm < NM)[None, None], jnp.stack([bm0, bm1], axis=1), NEG_INF)
    bias_new_s = _rel_bias_toeplitz(rel_bias, Ls, Ls, 0)
    bias_cache = jnp.stack([_rel_bias_const(rel_bias, -2 * tkc, Ls, tkc),
                            _rel_bias_toeplitz(rel_bias, Ls, tkc, -tkc)], axis=1)
    bias_new_m = _rel_bias_toeplitz(rel_bias, NM, NM, 0)

    sizes = (W, W, W, QKV_COLS, HEADS, HEADS, HEADS * HD, QKV_COLS, HEADS, HEADS, HEADS * HD)
    offs = [0]
    for s_ in sizes:
        offs.append(offs[-1] + s_)
    seg = lambda w, i: w[:, offs[i]:offs[i + 1]]

    xm = jnp.broadcast_to(meta_tokens.astype(F32)[None], (B, NM, D)).reshape(B * NM, D)
    xb = x_prompt.reshape(B * T, D)
    xs = x_sample.reshape(Bs * Ls, D)
    p_states, s_states = [], []
    for l in range(depth):
        wl = w_in[l]
        w_main = jnp.concatenate([seg(wl, 0), seg(wl, 1), seg(wl, 2), seg(wl, 3), seg(wl, 7), seg(wl, 6), seg(wl, 10)],
                                 axis=1).astype(BF16)
        w_scal = jnp.concatenate([seg(wl, 4), seg(wl, 5), seg(wl, 8), seg(wl, 9),
                                  jnp.zeros((D, SCAL_COLS - 4 * HEADS), F32)], axis=1).astype(BF16)
        lw = {
            "norm_mix": norm_mix[l], "norm_ffn": norm_ffn[l], "w_in": w_main, "w_in_scal": w_scal,
            "w_out": w_out[l].astype(BF16), "w_up": ffn_w_up[l].astype(BF16), "w_down": ffn_w_down[l].astype(BF16),
            "da_q_norm": da_q_norm[l], "da_k_norm": da_k_norm[l], "gdn_conv_w": gdn_conv_w[l],
            "gdn_neg_a": -jnp.exp(gdn_A_log[l].astype(F32)), "gdn_dt_bias": gdn_dt_bias[l], "gdn_out_norm": gdn_out_norm[l],
            "ml_i_bias": ml_i_bias[l], "ml_f_bias": ml_f_bias[l], "ml_out_norm": ml_out_norm[l],
            "ffn_conv_w": ffn_conv_w[l],
        }
        lam_init = 0.8 - 0.6 * math.exp(-0.3 * l)
        lam = (jnp.exp(jnp.sum(da_lq1[l].astype(F32) * da_lk1[l].astype(F32)))
               - jnp.exp(jnp.sum(da_lq2[l].astype(F32) * da_lk2[l].astype(F32))) + lam_init).reshape(1, 1)
        out_scale = 1.0 - lam_init
        gain_o = da_out_norm[l]

        zero = {"gconv": jnp.zeros((B, GDN_CONV - 1, QKV_COLS), F32), "S": jnp.zeros((B, HEADS, HD, HD), F32),
                "C": jnp.zeros((B, HEADS, HD, HD), F32), "n": jnp.zeros((B, HEADS, HD), F32),
                "m": jnp.zeros((B, HEADS), F32), "fconv": jnp.zeros((B, FFN_CONV - 1, Fd), F32)}
        xm, st_m = _layer(xm, B, NM, NM, lw, zero,
                          lambda q, k, v: attn_small(q, k, v, None, bias_new_m, lam, gain_o, out_scale, B, NM))

        kmeta = jnp.pad(st_m["k16"].reshape(B, NM, W), ((0, 0), (0, MP - NM), (0, 0)))
        kmeta_t = jnp.transpose(kmeta, (0, 2, 1)).reshape(B * W, MP)
        vmeta = jnp.pad(st_m["v16"].reshape(B, NM, W), ((0, 0), (0, MP - NM), (0, 0))).reshape(B * MP, W)
        xb, st_b = _layer(xb, B, T, CHUNK, lw, st_m,
                          lambda q, kt, v: attn_big(q, kt, v, kmeta_t, vmeta, bias_big, bias_far, bias_meta_big, lam, gain_o,
                                                    out_scale, B, T, tq, tk, ATT_RB), want_kt=True)

        init_s = {"gconv": state_gdn_conv[l], "S": state_gdn_S[l].astype(F32), "C": state_mlstm_C[l].astype(F32),
                  "n": state_mlstm_n[l].astype(F32), "m": state_mlstm_m[l].astype(F32), "fconv": state_ffn_conv[l]}
        cache = (cache_attn_k[l].reshape(Bs * P, W).astype(BF16), cache_attn_v[l].reshape(Bs * P, W).astype(BF16),
                 bias_cache, tkc)
        xs, st_s = _layer(xs, Bs, Ls, Ls, lw, init_s,
                          lambda q, k, v: attn_small(q, k, v, cache, bias_new_s, lam, gain_o, out_scale, Bs, Ls))

        names = ("k", "v", "gconv", "S", "C", "n", "m", "fconv")
        p_states.append([jnp.concatenate([st_m[n_], st_b[n_]], axis=1) if n_ in ("k", "v") else st_b[n_]
                         for n_ in names])
        s_states.append([st_s[n_] for n_ in names])

    p_out = [jnp.stack([p_states[l][i] for l in range(depth)]) for i in range(8)]
    s_out = [jnp.stack([s_states[l][i] for l in range(depth)]) for i in range(8)]
    y_prompt = xb.reshape(B, T, D)
    y_sample = xs.reshape(Bs, Ls, D)
    return (y_prompt, y_sample, *p_out, *s_out)
```

```python
import functools
import math

import jax
import jax.numpy as jnp
from jax import lax
from jax.experimental import pallas as pl
from jax.experimental.pallas import tpu as pltpu

F32 = jnp.float32
BF16 = jnp.bfloat16

HEADS = 4
HD = 128
DA_DV = 2 * HD
CHUNK = 64
GDN_CONV = 4
FFN_CONV = 3
NUM_BUCKETS = 32
MAX_DISTANCE = 128
RMS_EPS = 1e-6
NEG_INF = -1e30
HALO = 8

QKV_COLS = 3 * HEADS * HD
Z_COLS = 3 * HEADS * DA_DV + 2 * QKV_COLS + 2 * HEADS * HD
SCAL_COLS = 128

LOG2E = math.log2(math.e)
Q_SCALE = HD ** -0.5 * LOG2E

VMEM_LIMIT = 56 * 1024 * 1024


def _pick(n, cands):
    for c in cands:
        if n % c == 0:
            return c
    raise ValueError(f"no tile for {n} in {cands}")


def _params(sem):
    return pltpu.CompilerParams(dimension_semantics=sem, vmem_limit_bytes=VMEM_LIMIT)


def _sigmoid(x):
    return 0.5 * jnp.tanh(0.5 * x) + 0.5


def _softplus(x):
    return jnp.maximum(x, 0.0) + jnp.log(1.0 + jnp.exp(-jnp.abs(x)))


def _dot_nt(a, b):
    return lax.dot_general(a, b, (((1,), (1,)), ((), ())), preferred_element_type=F32)


def _rms_matmul_kernel(x_ref, g_ref, w_ref, *rest, has_extra):
    if has_extra:
        ws_ref, o_ref, os_ref, xn_ref = rest
    else:
        o_ref, xn_ref = rest

    @pl.when(pl.program_id(1) == 0)
    def _():
        x = x_ref[...]
        ms = jnp.mean(x * x, axis=-1, keepdims=True)
        xn = (x * lax.rsqrt(ms + RMS_EPS) * g_ref[...]).astype(BF16)
        xn_ref[...] = xn
        if has_extra:
            os_ref[...] = jnp.dot(xn, ws_ref[...], preferred_element_type=F32)

    o_ref[...] = jnp.dot(xn_ref[...], w_ref[...], preferred_element_type=F32).astype(o_ref.dtype)


def rms_matmul(x, gain, w, w_extra=None, name="rms_matmul"):
    M, K = x.shape
    N = w.shape[1]
    tm = _pick(M, (1024, 512, 256, 128, 64, 32, 16, 8))
    tn = _pick(N, (512, 256, 128))
    has_extra = w_extra is not None
    in_specs = [pl.BlockSpec((tm, K), lambda i, j: (i, 0)),
                pl.BlockSpec((1, K), lambda i, j: (0, 0)),
                pl.BlockSpec((K, tn), lambda i, j: (0, j))]
    out_specs = [pl.BlockSpec((tm, tn), lambda i, j: (i, j))]
    out_shape = [jax.ShapeDtypeStruct((M, N), F32)]
    args = [x, gain.reshape(1, K), w]
    if has_extra:
        ne = w_extra.shape[1]
        in_specs.append(pl.BlockSpec((K, ne), lambda i, j: (0, 0)))
        out_specs.append(pl.BlockSpec((tm, ne), lambda i, j: (i, 0)))
        out_shape.append(jax.ShapeDtypeStruct((M, ne), F32))
        args.append(w_extra)
    outs = pl.pallas_call(
        functools.partial(_rms_matmul_kernel, has_extra=has_extra),
        grid=(M // tm, N // tn),
        in_specs=in_specs, out_specs=out_specs, out_shape=out_shape,
        scratch_shapes=[pltpu.VMEM((tm, K), BF16)],
        compiler_params=_params(("parallel", "arbitrary")),
        name=name,
    )(*args)
    return outs if has_extra else outs[0]


def _matmul_res_kernel(*refs, n_a):
    a_refs, w_refs = refs[:n_a], refs[n_a:2 * n_a]
    res_ref, o_ref = refs[2 * n_a], refs[2 * n_a + 1]
    acc = res_ref[...]
    for a_ref, w_ref in zip(a_refs, w_refs):
        acc = acc + jnp.dot(a_ref[...], w_ref[...], preferred_element_type=F32)
    o_ref[...] = acc


def matmul_res(a_list, w, res, name="matmul_res"):
    M, N = res.shape
    ktot = sum(a.shape[1] for a in a_list)
    tm = _pick(M, (1024, 512, 256, 128, 64, 32, 16, 8) if ktot <= 4096 else (512, 256, 128, 64, 32, 16, 8))
    tn = _pick(N, (512, 256, 128))
    in_specs, off = [], 0
    for a in a_list:
        in_specs.append(pl.BlockSpec((tm, a.shape[1]), lambda i, j: (i, 0)))
    for a in a_list:
        k = a.shape[1]
        assert off % k == 0
        in_specs.append(pl.BlockSpec((k, tn), lambda i, j, _o=off // k: (_o, j)))
        off += k
    in_specs.append(pl.BlockSpec((tm, tn), lambda i, j: (i, j)))
    return pl.pallas_call(
        functools.partial(_matmul_res_kernel, n_a=len(a_list)),
        grid=(M // tm, N // tn),
        in_specs=in_specs,
        out_specs=pl.BlockSpec((tm, tn), lambda i, j: (i, j)),
        out_shape=jax.ShapeDtypeStruct((M, N), F32),
        compiler_params=_params(("parallel", "arbitrary")),
        name=name,
    )(*a_list, *([w] * len(a_list)), res)


def _qk_prep_kernel(q_ref, k_ref, v_ref, gq_ref, gk_ref, qn_ref, kn32_ref, kn16_ref, v16_ref, *rest):
    gq = gq_ref[...]
    gk = gk_ref[...]
    for g in range(2 * HEADS):
        sl = slice(g * HD, (g + 1) * HD)
        q = q_ref[:, sl]
        k = k_ref[:, sl]
        qn = q * lax.rsqrt(jnp.mean(q * q, axis=-1, keepdims=True) + RMS_EPS) * gq
        kn = k * lax.rsqrt(jnp.mean(k * k, axis=-1, keepdims=True) + RMS_EPS) * gk
        qn_ref[:, sl] = (qn * Q_SCALE).astype(BF16)
        kn32_ref[:, sl] = kn
        kn16_ref[:, sl] = kn.astype(BF16)
        if rest:
            rest[0][sl, :] = kn.T.astype(BF16)
    v16_ref[...] = v_ref[...].astype(BF16)


def qk_prep(z, gq, gk, want_kt=False):
    M = z.shape[0]
    W = 2 * HEADS * HD
    tm = _pick(M, (512, 256, 128, 64, 32, 16, 8))
    col = lambda c: pl.BlockSpec((tm, W), lambda i, _c=c: (i, _c))
    vec = pl.BlockSpec((1, HD), lambda i: (0, 0))
    row = pl.BlockSpec((tm, W), lambda i: (i, 0))
    out_specs = [row, row, row, row]
    out_shape = [jax.ShapeDtypeStruct((M, W), BF16), jax.ShapeDtypeStruct((M, W), F32),
                 jax.ShapeDtypeStruct((M, W), BF16), jax.ShapeDtypeStruct((M, W), BF16)]
    if want_kt:
        out_specs.append(pl.BlockSpec((W, tm), lambda i: (0, i)))
        out_shape.append(jax.ShapeDtypeStruct((W, M), BF16))
    return pl.pallas_call(
        _qk_prep_kernel,
        grid=(M // tm,),
        in_specs=[col(0), col(1), col(2), vec, vec],
        out_specs=out_specs,
        out_shape=out_shape,
        compiler_params=_params(("parallel",)),
        name="qk_prep",
    )(z, z, z, gq.reshape(1, HD), gk.reshape(1, HD))


def _softmax_update(mi, s, v, m_ref, l_ref, acc_ref):
    m_prev = m_ref[mi]
    m_new = jnp.maximum(m_prev, jnp.max(s, axis=-1, keepdims=True))
    alpha = jnp.exp2(m_prev - m_new)
    p = jnp.exp2(s - m_new)
    l_ref[mi] = alpha * l_ref[mi] + jnp.sum(p, axis=-1, keepdims=True)
    acc_ref[mi] = alpha * acc_ref[mi] + jnp.dot(p.astype(BF16), v, preferred_element_type=F32)
    m_ref[mi] = m_new


def _attn_init(m_ref, l_ref, acc_ref):
    m_ref[...] = jnp.full(m_ref.shape, -jnp.inf, F32)
    l_ref[...] = jnp.zeros(l_ref.shape, F32)
    acc_ref[...] = jnp.zeros(acc_ref.shape, F32)


def _attn_tile(q, k, v, bias, m_ref, l_ref, acc_ref):
    for mi in range(2):
        sl = slice(mi * HD, (mi + 1) * HD)
        s = _dot_nt(q[:, sl], k[:, sl]) + bias
        _softmax_update(mi, s, v, m_ref, l_ref, acc_ref)


def _attn_finish(lam, out_scale, g_ref, o_ref, l_ref, acc_ref):
    o = acc_ref[0] / l_ref[0] - lam * (acc_ref[1] / l_ref[1])
    on = o * lax.rsqrt(jnp.mean(o * o, axis=-1, keepdims=True) + RMS_EPS) * g_ref[...]
    o_ref[...] = (on * out_scale).astype(o_ref.dtype)


def _attn_scratch(tq):
    return [pltpu.VMEM((2, tq, 1), F32), pltpu.VMEM((2, tq, 1), F32), pltpu.VMEM((2, tq, DA_DV), F32)]


LANES = 128
ATT_TQ = (1024, 512, 256, 128, 64)
ATT_TK = (1024, 512, 256, 128, 64)
ATT_RB = 256


def _rowblock_tile(q_ref, kt_ref, v_ref, bias_ref, shift, m_ref, l_ref, acc_ref, rb):
    tq = q_ref.shape[0]
    tk = kt_ref.shape[1]
    nb = tk // LANES
    rb = min(rb, tq)
    nrb = tq // rb
    v = v_ref[...]

    def scores(r):
        rows = pl.ds(r * rb, rb)
        out = [jnp.dot(q_ref[rows, mi * HD:(mi + 1) * HD], kt_ref[mi * HD:(mi + 1) * HD, :],
                       preferred_element_type=F32) for mi in range(2)]
        if bias_ref is not None:
            bias = bias_ref[0, 0, rows, :]
            out = [s + bias for s in out]
        return out

    def update(r, s_pair):
        rows = pl.ds(r * rb, rb)
        for mi in range(2):
            s = s_pair[mi]
            blocks = [s[:, c * LANES:(c + 1) * LANES] for c in range(nb)]
            smax = blocks[0]
            for blk in blocks[1:]:
                smax = jnp.maximum(smax, blk)
            m_prev = m_ref[mi, rows, :]
            if shift is not None:
                m_prev = m_prev - shift
            m_new = jnp.maximum(m_prev, jnp.max(smax, axis=-1, keepdims=True))
            alpha = jnp.exp2(m_prev - m_new)
            ps = [jnp.exp2(blk - m_new) for blk in blocks]
            psum = ps[0]
            for pb in ps[1:]:
                psum = psum + pb
            l_ref[mi, rows, :] = alpha * l_ref[mi, rows, :] + psum
            m_ref[mi, rows, :] = m_new if shift is None else m_new + shift
            p = jnp.concatenate(ps, axis=1).astype(BF16) if nb > 1 else ps[0].astype(BF16)
            pv = jnp.dot(p, v, preferred_element_type=F32)
            acc_ref[mi, rows, :] = jnp.concatenate([alpha, alpha], axis=1) * acc_ref[mi, rows, :] + pv

    s_next = scores(0)
    for r in range(nrb):
        s_cur = s_next
        if r + 1 < nrb:
            s_next = scores(r + 1)
        update(r, s_cur)


def _attn_big_kernel(qi_ref, kj_ref, lam_ref, far_ref, q_ref, k_ref, v_ref, km_ref, vm_ref, bias_ref, bmeta_ref,
                     g_ref, o_ref, m_ref, l_ref, acc_ref, *, out_scale, ratio, rb):
    step = pl.program_id(2)
    i = qi_ref[step]
    j = kj_ref[step]
    is_far = j - ratio * i < -1

    @pl.when(j == 0)
    def _():
        _attn_init(m_ref, l_ref, acc_ref)
        _rowblock_tile(q_ref, km_ref, vm_ref, bmeta_ref, None, m_ref, l_ref, acc_ref, rb)

    @pl.when(is_far)
    def _():
        _rowblock_tile(q_ref, k_ref, v_ref, None, far_ref[pl.program_id(1)], m_ref, l_ref, acc_ref, rb)

    @pl.when(jnp.logical_not(is_far))
    def _():
        _rowblock_tile(q_ref, k_ref, v_ref, bias_ref, None, m_ref, l_ref, acc_ref, rb)

    @pl.when(j == ratio * (i + 1) - 1)
    def _():
        l0 = jnp.sum(l_ref[0], axis=-1, keepdims=True)
        l1 = jnp.sum(l_ref[1], axis=-1, keepdims=True)
        o = acc_ref[0] / l0 - lam_ref[0, 0] * (acc_ref[1] / l1)
        on = o * lax.rsqrt(jnp.mean(o * o, axis=-1, keepdims=True) + RMS_EPS) * g_ref[...]
        o_ref[...] = (on * out_scale).astype(o_ref.dtype)


def attn_big(qn, knt16, v16, kmeta_t, vmeta, bias, far, bmeta, lam, gain, out_scale, B, T, tq, tk, rb):
    nq, nk = T // tq, T // tk
    ratio = tq // tk
    mp = vmeta.shape[0] // B
    pairs = [(i, j) for i in range(nq) for j in range(ratio * (i + 1))]
    qi = jnp.asarray([p[0] for p in pairs], jnp.int32)
    kj = jnp.asarray([p[1] for p in pairs], jnp.int32)
    grid_spec = pltpu.PrefetchScalarGridSpec(
        num_scalar_prefetch=2,
        grid=(B, HEADS, len(pairs)),
        in_specs=[
            pl.BlockSpec(memory_space=pltpu.SMEM),
            pl.BlockSpec(memory_space=pltpu.SMEM),
            pl.BlockSpec((tq, DA_DV), lambda b, h, s, qi, kj: (b * nq + qi[s], h)),
            pl.BlockSpec((DA_DV, tk), lambda b, h, s, qi, kj: (h, b * nk + kj[s])),
            pl.BlockSpec((tk, DA_DV), lambda b, h, s, qi, kj: (b * nk + kj[s], h)),
            pl.BlockSpec((DA_DV, mp), lambda b, h, s, qi, kj: (b * HEADS + h, 0)),
            pl.BlockSpec((mp, DA_DV), lambda b, h, s, qi, kj: (b, h)),
            pl.BlockSpec((1, 1, tq, tk),
                         lambda b, h, s, qi, kj: (h, jnp.maximum(kj[s] - ratio * qi[s] + 1, 0), 0, 0)),
            pl.BlockSpec((1, 1, tq, mp), lambda b, h, s, qi, kj: (h, jnp.minimum(qi[s], 1), 0, 0)),
            pl.BlockSpec((1, DA_DV), lambda b, h, s, qi, kj: (0, 0)),
        ],
        out_specs=pl.BlockSpec((tq, DA_DV), lambda b, h, s, qi, kj: (b * nq + qi[s], h)),
        scratch_shapes=[pltpu.VMEM((2, tq, LANES), F32), pltpu.VMEM((2, tq, LANES), F32),
                        pltpu.VMEM((2, tq, DA_DV), F32)],
    )
    return pl.pallas_call(
        functools.partial(_attn_big_kernel, out_scale=out_scale, ratio=ratio, rb=rb),
        grid_spec=grid_spec,
        out_shape=jax.ShapeDtypeStruct((B * T, HEADS * DA_DV), BF16),
        compiler_params=_params(("parallel", "parallel", "arbitrary")),
        name="attn_prompt",
    )(qi, kj, lam, far, qn, knt16, v16, kmeta_t, vmeta, bias, bmeta, gain.reshape(1, DA_DV))


def _attn_small_kernel(lam_ref, q_ref, *rest, nc, out_scale):
    if nc > 0:
        ck_ref, cv_ref, bc_ref, k_ref, v_ref, bn_ref, g_ref, o_ref, m_ref, l_ref, acc_ref = rest
    else:
        k_ref, v_ref, bn_ref, g_ref, o_ref, m_ref, l_ref, acc_ref = rest
    j = pl.program_id(2)
    q = q_ref[...]

    @pl.when(j == 0)
    def _():
        _attn_init(m_ref, l_ref, acc_ref)

    if nc > 0:
        @pl.when(j < nc)
        def _():
            _attn_tile(q, ck_ref[...].astype(BF16), cv_ref[...].astype(BF16), bc_ref[0, 0], m_ref, l_ref, acc_ref)

    @pl.when(j == nc)
    def _():
        _attn_tile(q, k_ref[...], v_ref[...], bn_ref[0], m_ref, l_ref, acc_ref)
        _attn_finish(lam_ref[0, 0], out_scale, g_ref, o_ref, l_ref, acc_ref)


def attn_small(qn, kn16, v16, cache, bias_new, lam, gain, out_scale, B, L):
    W = HEADS * DA_DV
    blk = pl.BlockSpec((L, DA_DV), lambda b, h, j: (b, h))
    in_specs = [pl.BlockSpec(memory_space=pltpu.SMEM), blk]
    args = [lam, qn]
    nc = 0
    if cache is not None:
        ck, cv, bc, tk = cache
        nc = ck.shape[0] // B // tk
        cspec = pl.BlockSpec((tk, DA_DV), lambda b, h, j: (b * nc + jnp.minimum(j, nc - 1), h))
        in_specs += [cspec, cspec,
                     pl.BlockSpec((1, 1, L, tk), lambda b, h, j: (h, jnp.where(j >= nc - 1, 1, 0), 0, 0))]
        args += [ck, cv, bc]
    in_specs += [blk, blk, pl.BlockSpec((1, L, L), lambda b, h, j: (h, 0, 0)),
                 pl.BlockSpec((1, DA_DV), lambda b, h, j: (0, 0))]
    args += [kn16, v16, bias_new, gain.reshape(1, DA_DV)]
    return pl.pallas_call(
        functools.partial(_attn_small_kernel, nc=nc, out_scale=out_scale),
        grid=(B, HEADS, nc + 1),
        in_specs=in_specs,
        out_specs=blk,
        out_shape=jax.ShapeDtypeStruct((B * L, W), BF16),
        scratch_shapes=_attn_scratch(L),
        compiler_params=_params(("parallel", "parallel", "arbitrary")),
        name="attn_block",
    )(*args)


_NN = (((1,), (0,)), ((), ()))
_NT = (((1,), (1,)), ((), ()))
_TN = (((0,), (0,)), ((), ()))


def _dg(a, b, dn):
    return lax.dot_general(a, b, dn, preferred_element_type=F32)


def _bf16r(a):
    return a.astype(BF16).astype(F32)


def _mm3(a, b, dn=_NN):
    ah = _bf16r(a)
    al = a - ah
    bh = _bf16r(b)
    bl = b - bh
    return _dg(ah, bh, dn) + (_dg(ah, bl, dn) + _dg(al, bh, dn))


def _split3(a):
    hi = _bf16r(a)
    r1 = a - hi
    mid = _bf16r(r1)
    return hi, mid, r1 - mid


def _chunk_masks(L):
    row = lax.broadcasted_iota(jnp.int32, (L, L), 0)
    col = lax.broadcasted_iota(jnp.int32, (L, L), 1)
    return row, col


def _cumsum_cols_rows(x, tril, triu):
    hi, mid, lo = _split3(x)
    cols = _dg(tril, hi, _NN) + (_dg(tril, mid, _NN) + _dg(tril, lo, _NN))
    rows = _dg(hi, triu, _TN) + (_dg(mid, triu, _TN) + _dg(lo, triu, _TN))
    return cols, rows


def _gdn_prep_kernel(x_ref, halo_ref, prev_ref, sc_ref, cw_ref, nega_ref, dtb_ref,
                     w_ref, u_ref, qg_ref, kd_ref, qk_ref, gc_ref, xbuf, *, L, G):
    i = pl.program_id(1)
    tile = L * G
    xbuf[0:HALO, :] = jnp.where(i == 0, prev_ref[0], halo_ref[...])
    xbuf[HALO:HALO + tile, :] = x_ref[...]
    cw = cw_ref[...]
    y = xbuf[HALO - 3:HALO - 3 + tile, :] * cw[0:1, :]
    for t in range(1, GDN_CONV):
        y = y + xbuf[HALO - 3 + t:HALO - 3 + t + tile, :] * cw[t:t + 1, :]
    gcv = y * _sigmoid(y)

    row, col = _chunk_masks(L)
    incl = col <= row
    strict = col < row
    eye = (col == row).astype(F32)
    tril = incl.astype(F32)
    triu = (row <= col).astype(F32)
    sc = sc_ref[...]
    beta_all = _sigmoid(sc)
    g_all = nega_ref[...] * _softplus(sc + dtb_ref[...])
    nqk = HEADS * HD

    st = []
    for c in range(G):
        rows = slice(c * L, (c + 1) * L)
        g_cols, g_rows = _cumsum_cols_rows(g_all[rows], tril, triu)
        gc_ref[rows, :] = g_cols
        for h in range(HEADS):
            xq = gcv[rows, h * HD:(h + 1) * HD]
            xk = gcv[rows, nqk + h * HD:nqk + (h + 1) * HD]
            xv = gcv[rows, 2 * nqk + h * HD:2 * nqk + (h + 1) * HD]
            qn = xq * lax.rsqrt(jnp.sum(xq * xq, axis=-1, keepdims=True) + RMS_EPS) * (HD ** -0.5)
            kn = xk * lax.rsqrt(jnp.sum(xk * xk, axis=-1, keepdims=True) + RMS_EPS)
            beta = beta_all[rows, h:h + 1]
            gcol = g_cols[:, HEADS + h:HEADS + h + 1]
            grow = g_rows[HEADS + h:HEADS + h + 1, :]
            dec = jnp.where(incl, jnp.exp(jnp.where(incl, gcol - grow, 0.0)), 0.0)
            eg = jnp.exp(gcol)
            g_last = gcol[L - 1:L, :]
            st.append(dict(rows=rows, h=h, qn=qn, kn=kn, bv=beta * xv, bek=beta * eg * kn, beta=beta, dec=dec, eg=eg,
                           kd=kn * jnp.exp(g_last - gcol)))

    kk = [_mm3(s["kn"], s["kn"], _NT) for s in st]
    nmat = [jnp.where(strict, s["beta"] * k_ * s["dec"], 0.0) for s, k_ in zip(st, kk)]
    tinv = [eye - n for n in nmat]
    pw = [_mm3(n, n) for n in nmat]
    p = 2
    while True:
        tinv = [t + _mm3(t, q) for t, q in zip(tinv, pw)]
        p *= 2
        if p >= L:
            break
        pw = [_mm3(q, q) for q in pw]
    w_mat = [_mm3(t, s["bek"]) for t, s in zip(tinv, st)]
    u_mat = [_mm3(t, s["bv"]) for t, s in zip(tinv, st)]
    qk = [_mm3(s["qn"], s["kn"], _NT) * s["dec"] for s in st]
    for s, w_, u_, qk_ in zip(st, w_mat, u_mat, qk):
        rows, h = s["rows"], s["h"]
        sl = slice(h * HD, (h + 1) * HD)
        w_ref[rows, sl] = w_
        u_ref[rows, sl] = u_
        qg_ref[rows, sl] = s["eg"] * s["qn"]
        kd_ref[rows, sl] = s["kd"]
        qk_ref[rows, h * L:(h + 1) * L] = qk_


def _gdn_scan_kernel(w_ref, u_ref, qg_ref, kd_ref, qk_ref, gc_ref, gate_ref, s0_ref, gn_ref,
                     o_ref, sout_ref, s_scr, *, L, cps, B):
    i = pl.program_id(0)

    @pl.when(i == 0)
    def _():
        s_scr[...] = s0_ref[...]

    chains = [(b, h) for b in range(B) for h in range(HEADS)]
    gn = gn_ref[...]
    for c in range(cps):
        rows = slice(c * L, (c + 1) * L)
        last = (c + 1) * L - 1
        s_old = [s_scr[b, h] for b, h in chains]
        ws = [_mm3(w_ref[b, rows, h * HD:(h + 1) * HD], s_) for (b, h), s_ in zip(chains, s_old)]
        u = [u_ref[b, rows, h * HD:(h + 1) * HD] - w_ for (b, h), w_ in zip(chains, ws)]
        upd = [_mm3(kd_ref[b, rows, h * HD:(h + 1) * HD], u_, _TN) for (b, h), u_ in zip(chains, u)]
        for (b, h), s_, d_ in zip(chains, s_old, upd):
            ds = jnp.exp(gc_ref[b, last:last + 1, HEADS + h:HEADS + h + 1])
            s_scr[b, h] = ds * s_ + d_
        o1 = [_mm3(qg_ref[b, rows, h * HD:(h + 1) * HD], s_) for (b, h), s_ in zip(chains, s_old)]
        o2 = [_mm3(qk_ref[b, rows, h * L:(h + 1) * L], u_) for (b, h), u_ in zip(chains, u)]
        for (b, h), a_, b_ in zip(chains, o1, o2):
            sl = slice(h * HD, (h + 1) * HD)
            o = a_ + b_
            on = o * lax.rsqrt(jnp.mean(o * o, axis=-1, keepdims=True) + RMS_EPS) * gn
            gt = gate_ref[b, rows, sl]
            o_ref[b, rows, sl] = (on * (gt * _sigmoid(gt))).astype(o_ref.dtype)

    @pl.when(i == pl.num_programs(0) - 1)
    def _():
        sout_ref[...] = s_scr[...]


def gdn(z, sc, prev8, conv_w, neg_a, dt_bias, s0, gn, B, T, L):
    M = B * T
    nchunks = T // L
    G = _pick(nchunks, (2, 1))
    tile = L * G
    nt = T // tile
    hb = tile // HALO
    nega_vec = jnp.zeros((1, SCAL_COLS), F32).at[0, HEADS:2 * HEADS].set(neg_a)
    dtb_vec = jnp.zeros((1, SCAL_COLS), F32).at[0, HEADS:2 * HEADS].set(dt_bias.astype(F32))
    rowblk = lambda w: pl.BlockSpec((tile, w), lambda b, i: (b * nt + i, 0))
    vec = pl.BlockSpec((1, SCAL_COLS), lambda b, i: (0, 0))
    w_mat, u_mat, qg, kd, qk, gc = pl.pallas_call(
        functools.partial(_gdn_prep_kernel, L=L, G=G),
        grid=(B, nt),
        in_specs=[pl.BlockSpec((tile, QKV_COLS), lambda b, i: (b * nt + i, 2)),
                  pl.BlockSpec((HALO, QKV_COLS), lambda b, i: (jnp.maximum((b * nt + i) * hb - 1, 0), 2)),
                  pl.BlockSpec((1, HALO, QKV_COLS), lambda b, i: (b, 0, 0)),
                  rowblk(SCAL_COLS),
                  pl.BlockSpec((GDN_CONV, QKV_COLS), lambda b, i: (0, 0)),
                  vec, vec],
        out_specs=[rowblk(HEADS * HD)] * 4 + [rowblk(HEADS * L), rowblk(SCAL_COLS)],
        out_shape=[jax.ShapeDtypeStruct((M, HEADS * HD), F32)] * 4
                  + [jax.ShapeDtypeStruct((M, HEADS * L), F32), jax.ShapeDtypeStruct((M, SCAL_COLS), F32)],
        scratch_shapes=[pltpu.VMEM((tile + HALO, QKV_COLS), F32)],
        compiler_params=_params(("parallel", "parallel")),
        name="gdn_prep",
    )(z, z, prev8, sc, conv_w, nega_vec, dtb_vec)

    cps = _pick(nchunks, (4, 2, 1))
    rows = cps * L
    r3 = lambda a: a.reshape(B, T, a.shape[-1])
    blk = lambda w, cb=0: pl.BlockSpec((B, rows, w), lambda i, _c=cb: (0, i, _c))
    st = pl.BlockSpec((B, HEADS, HD, HD), lambda i: (0, 0, 0, 0))
    o, s_new = pl.pallas_call(
        functools.partial(_gdn_scan_kernel, L=L, cps=cps, B=B),
        grid=(nchunks // cps,),
        in_specs=[blk(HEADS * HD)] * 4 + [blk(HEADS * L), blk(SCAL_COLS), blk(HEADS * HD, 12), st,
                                          pl.BlockSpec((1, HD), lambda i: (0, 0))],
        out_specs=[blk(HEADS * HD), st],
        out_shape=[jax.ShapeDtypeStruct((B, T, HEADS * HD), BF16), jax.ShapeDtypeStruct((B, HEADS, HD, HD), F32)],
        scratch_shapes=[pltpu.VMEM((B, HEADS, HD, HD), F32)],
        compiler_params=_params(("arbitrary",)),
        name="gdn_scan",
    )(r3(w_mat), r3(u_mat), r3(qg), r3(kd), r3(qk), r3(gc), r3(z), s0, gn.reshape(1, HD))
    return o.reshape(M, HEADS * HD), s_new


def _mlstm_prep_kernel(x_ref, sc_ref, ib_ref, fb_ref, qk_ref, d_ref, st_ref, *, L, G):
    row, col = _chunk_masks(L)
    incl = col <= row
    eye = (col == row).astype(F32)
    tril = incl.astype(F32)
    triu = (row <= col).astype(F32)
    sc = sc_ref[...]
    ig_all = sc + ib_ref[...]
    lf_all = -_softplus(-(sc + fb_ref[...]))
    lane = lax.broadcasted_iota(jnp.int32, (L, SCAL_COLS), 1)
    nqk = HEADS * HD
    for c in range(G):
        rows = slice(c * L, (c + 1) * L)
        b_cols, b_rows = _cumsum_cols_rows(lf_all[rows], tril, triu)
        ih, im, il = _split3(ig_all[rows])
        ig_rows = _dg(ih, eye, _TN) + (_dg(im, eye, _TN) + _dg(il, eye, _TN))
        stats = jnp.where(lane >= 3 * HEADS, b_cols, ig_all[rows])
        for h in range(HEADS):
            bcol = b_cols[:, 3 * HEADS + h:3 * HEADS + h + 1]
            brow = b_rows[3 * HEADS + h:3 * HEADS + h + 1, :]
            igrow = ig_rows[2 * HEADS + h:2 * HEADS + h + 1, :]
            dmat = jnp.where(incl, bcol - brow + igrow, NEG_INF)
            stats = jnp.where(lane == h, jnp.max(dmat, axis=-1, keepdims=True), stats)
            q = x_ref[rows, h * HD:(h + 1) * HD]
            k = x_ref[rows, nqk + h * HD:nqk + (h + 1) * HD] * (HD ** -0.5)
            qk_ref[rows, h * L:(h + 1) * L] = _mm3(q, k, _NT)
            d_ref[rows, h * L:(h + 1) * L] = dmat
        st_ref[rows, :] = stats


def _mlstm_scan_kernel(x_ref, og_ref, qk_ref, d_ref, st_ref, c0_ref, n0_ref, m0_ref, gn_ref,
                       o_ref, cout_ref, nout_ref, mout_ref, c_scr, n_scr, m_scr, *, L, cps, B):
    i = pl.program_id(0)

    @pl.when(i == 0)
    def _():
        c_scr[...] = c0_ref[...]
        n_scr[...] = n0_ref[...]
        m_scr[...] = m0_ref[...]

    chains = [(b, h) for b in range(B) for h in range(HEADS)]
    gn = gn_ref[...]
    nqk = HEADS * HD
    for c in range(cps):
        rows = slice(c * L, (c + 1) * L)
        st = []
        for b, h in chains:
            q = x_ref[b, rows, h * HD:(h + 1) * HD]
            k = x_ref[b, rows, nqk + h * HD:nqk + (h + 1) * HD] * (HD ** -0.5)
            v = x_ref[b, rows, 2 * nqk + h * HD:2 * nqk + (h + 1) * HD]
            dmax = st_ref[b, rows, h:h + 1]
            ig = st_ref[b, rows, 2 * HEADS + h:2 * HEADS + h + 1]
            bcol = st_ref[b, rows, 3 * HEADS + h:3 * HEADS + h + 1]
            m_prev = m_scr[b, h:h + 1, 0:1]
            inter = bcol + m_prev
            m_t = jnp.maximum(inter, dmax)
            w_inter = jnp.exp(inter - m_t)
            qkw = qk_ref[b, rows, h * L:(h + 1) * L] * jnp.exp(d_ref[b, rows, h * L:(h + 1) * L] - m_t)
            m_new = m_t[L - 1:L, :]
            b_last = bcol[L - 1:L, :]
            kw = jnp.exp(b_last - bcol + ig - m_new) * k
            dec0 = jnp.exp(b_last + m_prev - m_new)
            st.append(dict(q=q, v=v, m_t=m_t, w_inter=w_inter, qkw=qkw, kw=kw, dec0=dec0, m_new=m_new,
                           c_old=c_scr[b, h], n_old=n_scr[b, h:h + 1, :]))
        upd = [_mm3(s["kw"], s["v"], _TN) for s in st]
        for (b, h), s, d_ in zip(chains, st, upd):
            c_scr[b, h] = s["dec0"] * s["c_old"] + d_
            n_scr[b, h:h + 1, :] = s["dec0"] * s["n_old"] + jnp.sum(s["kw"], axis=0, keepdims=True)
            m_scr[b, h:h + 1, :] = jnp.broadcast_to(s["m_new"], (1, HD))
        qc = [_mm3(s["q"], s["c_old"]) for s in st]
        pv = [_mm3(s["qkw"], s["v"]) for s in st]
        for (b, h), s, qc_, pv_ in zip(chains, st, qc, pv):
            sl = slice(h * HD, (h + 1) * HD)
            num = s["w_inter"] * qc_ + pv_
            den = (s["w_inter"] * jnp.sum(s["q"] * s["n_old"], axis=-1, keepdims=True)
                   + jnp.sum(s["qkw"], axis=-1, keepdims=True))
            hout = num / jnp.maximum(jnp.abs(den), jnp.exp(-s["m_t"]))
            on = hout * lax.rsqrt(jnp.mean(hout * hout, axis=-1, keepdims=True) + RMS_EPS) * gn
            o_ref[b, rows, sl] = (on * _sigmoid(og_ref[b, rows, sl])).astype(o_ref.dtype)

    @pl.when(i == pl.num_programs(0) - 1)
    def _():
        cout_ref[...] = c_scr[...]
        nout_ref[...] = n_scr[...]
        mout_ref[...] = m_scr[...]


def mlstm(z, sc, i_bias, f_bias, c0, n0p, m0p, gn, B, T, L):
    M = B * T
    nchunks = T // L
    G = _pick(nchunks, (4, 2, 1))
    tile = L * G
    nt = T // tile
    ib_vec = jnp.zeros((1, SCAL_COLS), F32).at[0, 2 * HEADS:3 * HEADS].set(i_bias.astype(F32))
    fb_vec = jnp.zeros((1, SCAL_COLS), F32).at[0, 3 * HEADS:4 * HEADS].set(f_bias.astype(F32))
    rowblk = lambda w, cb=0: pl.BlockSpec((tile, w), lambda b, i, _c=cb: (b * nt + i, _c))
    vec = pl.BlockSpec((1, SCAL_COLS), lambda b, i: (0, 0))
    qk, dm, stats = pl.pallas_call(
        functools.partial(_mlstm_prep_kernel, L=L, G=G),
        grid=(B, nt),
        in_specs=[rowblk(QKV_COLS, 3), rowblk(SCAL_COLS), vec, vec],
        out_specs=[rowblk(HEADS * L), rowblk(HEADS * L), rowblk(SCAL_COLS)],
        out_shape=[jax.ShapeDtypeStruct((M, HEADS * L), F32), jax.ShapeDtypeStruct((M, HEADS * L), F32),
                   jax.ShapeDtypeStruct((M, SCAL_COLS), F32)],
        compiler_params=_params(("parallel", "parallel")),
        name="mlstm_prep",
    )(z, sc, ib_vec, fb_vec)

    cps = _pick(nchunks, (4, 2, 1))
    rows = cps * L
    r3 = lambda a: a.reshape(B, T, a.shape[-1])
    blk = lambda w, cb=0: pl.BlockSpec((B, rows, w), lambda i, _c=cb: (0, i, _c))
    st = pl.BlockSpec((B, HEADS, HD, HD), lambda i: (0, 0, 0, 0))
    vec8 = pl.BlockSpec((B, 8, HD), lambda i: (0, 0, 0))
    o, c_new, n_new, m_new = pl.pallas_call(
        functools.partial(_mlstm_scan_kernel, L=L, cps=cps, B=B),
        grid=(nchunks // cps,),
        in_specs=[blk(QKV_COLS, 3), blk(HEADS * HD, 13), blk(HEADS * L), blk(HEADS * L), blk(SCAL_COLS),
                  st, vec8, vec8, pl.BlockSpec((1, HD), lambda i: (0, 0))],
        out_specs=[blk(HEADS * HD), st, vec8, vec8],
        out_shape=[jax.ShapeDtypeStruct((B, T, HEADS * HD), BF16), jax.ShapeDtypeStruct((B, HEADS, HD, HD), F32),
                   jax.ShapeDtypeStruct((B, 8, HD), F32), jax.ShapeDtypeStruct((B, 8, HD), F32)],
        scratch_shapes=[pltpu.VMEM((B, HEADS, HD, HD), F32), pltpu.VMEM((B, 8, HD), F32),
                        pltpu.VMEM((B, 8, HD), F32)],
        compiler_params=_params(("arbitrary",)),
        name="mlstm_scan",
    )(r3(z), r3(z), r3(qk), r3(dm), r3(stats), c0, n0p, m0p, gn.reshape(1, HD))
    return o.reshape(M, HEADS * HD), c_new, n_new, m_new


def _ffn_up_act_kernel(x_ref, xh_ref, g_ref, wg_ref, wv_ref, prev_ref, cw_ref, o_ref, tail_ref,
                       xn_ref, xhn_ref, buf, *, tm, tpb):
    i = pl.program_id(0)

    def norm(x):
        ms = jnp.mean(x * x, axis=-1, keepdims=True)
        return (x * lax.rsqrt(ms + RMS_EPS) * g_ref[...]).astype(BF16)

    @pl.when(pl.program_id(1) == 0)
    def _():
        xn_ref[...] = norm(x_ref[...])
        xh = norm(xh_ref[...])
        xhn_ref[...] = jnp.concatenate([xh, xh], axis=0)

    wg = wg_ref[...]
    gate = jnp.dot(xn_ref[...], wg, preferred_element_type=F32)
    val = jnp.dot(xn_ref[...], wv_ref[...], preferred_element_type=F32)
    halo = jnp.dot(xhn_ref[...], wg, preferred_element_type=F32)[0:HALO, :]
    buf[0:HALO, :] = jnp.where(i % tpb == 0, prev_ref[0], halo)
    buf[HALO:HALO + tm, :] = gate
    cwh = 0.5 * cw_ref[...]
    g = buf[...]
    yh = g[HALO:, :] * cwh[FFN_CONV - 1:FFN_CONV, :]
    for t in range(1, FFN_CONV):
        yh = yh + pltpu.roll(g, t, axis=0)[HALO:, :] * cwh[FFN_CONV - 1 - t:FFN_CONV - t, :]
    o_ref[...] = (yh * (jnp.tanh(yh) + 1.0) * val).astype(o_ref.dtype)
    tail_ref[0] = buf[tm:tm + HALO, :]


def ffn_up_act(x, gain, w_up, prev8, conv_w, B, T):
    M, K = x.shape
    Fd = conv_w.shape[1]
    tm = _pick(T, (1024, 512, 256, 128, 64, 32, 16, 8))
    tn = _pick(Fd, (512, 256, 128))
    tpb, nf = T // tm, Fd // tn
    hb = tm // HALO
    return pl.pallas_call(
        functools.partial(_ffn_up_act_kernel, tm=tm, tpb=tpb),
        grid=(M // tm, nf),
        in_specs=[pl.BlockSpec((tm, K), lambda i, j: (i, 0)),
                  pl.BlockSpec((HALO, K), lambda i, j: (jnp.maximum(i * hb - 1, 0), 0)),
                  pl.BlockSpec((1, K), lambda i, j: (0, 0)),
                  pl.BlockSpec((K, tn), lambda i, j: (0, j)),
                  pl.BlockSpec((K, tn), lambda i, j: (0, nf + j)),
                  pl.BlockSpec((1, HALO, tn), lambda i, j: (i // tpb, 0, j)),
                  pl.BlockSpec((FFN_CONV, tn), lambda i, j: (0, j))],
        out_specs=[pl.BlockSpec((tm, tn), lambda i, j: (i, j)),
                   pl.BlockSpec((1, HALO, tn), lambda i, j: (i // tpb, 0, j))],
        out_shape=[jax.ShapeDtypeStruct((M, Fd), BF16), jax.ShapeDtypeStruct((B, HALO, Fd), F32)],
        scratch_shapes=[pltpu.VMEM((tm, K), BF16), pltpu.VMEM((2 * HALO, K), BF16),
                        pltpu.VMEM((tm + HALO, tn), F32)],
        compiler_params=_params(("arbitrary", "arbitrary")),
        name="ffn_up_act",
    )(x, x, gain.reshape(1, K), w_up, w_up, prev8, conv_w)


def _t5_bucket(rel):
    half = NUM_BUCKETS // 2
    exact = half // 2
    n = jnp.abs(rel)
    nf = jnp.maximum(n, 1).astype(F32)
    large = exact + (jnp.log(nf / exact) / math.log(MAX_DISTANCE / exact) * (half - exact)).astype(jnp.int32)
    large = jnp.minimum(large, half - 1)
    return jnp.where(rel > 0, half, 0) + jnp.where(n < exact, n, large)


def _rel_bias_vec(rel_bias, rel):
    bucket = _t5_bucket(rel)
    rb = rel_bias.astype(F32) * LOG2E
    H = rb.shape[1]
    out = jnp.zeros((H,) + rel.shape, F32)
    for kb in range(NUM_BUCKETS):
        out = jnp.where((bucket == kb)[None], rb[kb].reshape((H,) + (1,) * rel.ndim), out)
    return out


def _rel_bias_const(rel_bias, rel, R, C):
    v = _rel_bias_vec(rel_bias, jnp.full((1,), rel, jnp.int32))
    return jnp.broadcast_to(v[:, :, None], (v.shape[0], R, C))


def _rel_bias_toeplitz(rel_bias, R, C, off):
    ar = lambda n: jnp.arange(n, dtype=jnp.int32)
    if R * C <= 256 * 1024 or R % LANES or C % LANES:
        return _rel_bias_vec(rel_bias, ar(C)[None, :] - ar(R)[:, None] + off)
    nr, nc = R // LANES, C // LANES
    deltas = jnp.arange(-(nr - 1), nc, dtype=jnp.int32)
    rel = LANES * deltas[:, None, None] + (ar(LANES)[None, None, :] - ar(LANES)[None, :, None]) + off
    small = _rel_bias_vec(rel_bias, rel)
    rows = [jnp.concatenate([small[:, e - a + nr - 1] for e in range(nc)], axis=-1) for a in range(nr)]
    return jnp.concatenate(rows, axis=-2)


def _pad_rows(a, rows):
    return jnp.pad(a, ((0, 0), (rows - a.shape[1], 0), (0, 0)))


def _layer(x, B, T, L, lw, init, attn_fn, want_kt=False):
    z, sc = rms_matmul(x, lw["norm_mix"], lw["w_in"], lw["w_in_scal"], name="in_proj")
    qn, kn32, kn16, v16, *knt = qk_prep(z, lw["da_q_norm"], lw["da_k_norm"], want_kt)
    o_da = attn_fn(qn, knt[0] if want_kt else kn16, v16)
    o_g, s_new = gdn(z, sc, _pad_rows(init["gconv"], HALO), lw["gdn_conv_w"], lw["gdn_neg_a"], lw["gdn_dt_bias"],
                     init["S"], lw["gdn_out_norm"], B, T, L)
    n0p = jnp.pad(init["n"], ((0, 0), (0, 8 - HEADS), (0, 0)))
    m0p = jnp.pad(jnp.broadcast_to(init["m"][:, :, None], (B, HEADS, HD)), ((0, 0), (0, 8 - HEADS), (0, 0)))
    o_m, c_new, n_new, m_new = mlstm(z, sc, lw["ml_i_bias"], lw["ml_f_bias"], init["C"], n0p, m0p,
                                     lw["ml_out_norm"], B, T, L)
    x1 = matmul_res([o_da, o_g, o_m], lw["w_out"], x, name="out_proj")
    hmid, gate_tail = ffn_up_act(x1, lw["norm_ffn"], lw["w_up"], _pad_rows(init["fconv"], HALO), lw["ffn_conv_w"], B, T)
    x2 = matmul_res([hmid], lw["w_down"], x1, name="ffn_down")
    z3 = z.reshape(B, T, Z_COLS)
    st = {
        "k": kn32.reshape(B, T, HEADS, 2, HD),
        "v": z3[:, :, 2 * HEADS * DA_DV:3 * HEADS * DA_DV].reshape(B, T, HEADS, DA_DV),
        "gconv": z3[:, T - (GDN_CONV - 1):, 3 * HEADS * DA_DV:3 * HEADS * DA_DV + QKV_COLS],
        "S": s_new, "C": c_new, "n": n_new[:, :HEADS], "m": m_new[:, :HEADS, 0],
        "fconv": gate_tail[:, HALO - (FFN_CONV - 1):, :],
        "k16": kn16, "v16": v16,
    }
    return x2, st


def kernel(x_prompt, x_sample, cache_attn_k, cache_attn_v, state_gdn_conv, state_gdn_S, state_mlstm_C, state_mlstm_n, state_mlstm_m, state_ffn_conv, meta_tokens, rel_bias, norm_mix, norm_ffn, w_in, w_out, da_q_norm, da_k_norm, da_lq1, da_lk1, da_lq2, da_lk2, da_out_norm, gdn_conv_w, gdn_A_log, gdn_dt_bias, gdn_out_norm, ml_i_bias, ml_f_bias, ml_out_norm, ffn_w_up, ffn_conv_w, ffn_w_down):
    B, T, D = x_prompt.shape
    Bs, Ls, _ = x_sample.shape
    depth = w_in.shape[0]
    P = cache_attn_k.shape[2]
    NM = meta_tokens.shape[0]
    Fd = ffn_conv_w.shape[-1]
    W = HEADS * DA_DV
    assert T % CHUNK == 0 and Ls <= CHUNK and NM <= CHUNK and NM % 8 == 0 and Ls % 8 == 0

    tq = _pick(T, ATT_TQ)
    tk = _pick(tq, ATT_TK)
    tkc = _pick(P, (512, 256, 128))
    assert tk >= 96 or T == tk, "far prompt tiles must lie past the last distinct relative-position bucket"
    assert tkc >= 96 or P == tkc
    MP = 128

    r = jnp.arange(tq, dtype=jnp.int32)[:, None]
    c = jnp.arange(tk, dtype=jnp.int32)[None, :]
    tiles = []
    for d in range(-1, tq // tk):
        visible = ((c + tk * d) // CHUNK <= r // CHUNK)[None]
        tiles.append(jnp.where(visible, _rel_bias_toeplitz(rel_bias, tq, tk, tk * d), NEG_INF))
    bias_big = jnp.stack(tiles, axis=1)
    bias_far = _rel_bias_vec(rel_bias, jnp.full((1,), -2 * tk, jnp.int32))[:, 0]
    cm = jnp.arange(MP, dtype=jnp.int32)[None, :]
    bm0 = _rel_bias_toeplitz(rel_bias, tq, MP, -NM)
    bm1 = _rel_bias_const(rel_bias, -2 * tq, tq, MP)
    bias_meta_big = jnp.where((cm < NM)[None, None], jnp.stack([bm0, bm1], axis=1), NEG_INF)
    bias_new_s = _rel_bias_toeplitz(rel_bias, Ls, Ls, 0)
    bias_cache = jnp.stack([_rel_bias_const(rel_bias, -2 * tkc, Ls, tkc),
                            _rel_bias_toeplitz(rel_bias, Ls, tkc, -tkc)], axis=1)
    bias_new_m = _rel_bias_toeplitz(rel_bias, NM, NM, 0)

    sizes = (W, W, W, QKV_COLS, HEADS, HEADS, HEADS * HD, QKV_COLS, HEADS, HEADS, HEADS * HD)
    offs = [0]
    for s_ in sizes:
        offs.append(offs[-1] + s_)
    seg = lambda w, i: w[:, offs[i]:offs[i + 1]]

    xm = jnp.broadcast_to(meta_tokens.astype(F32)[None], (B, NM, D)).reshape(B * NM, D)
    xb = x_prompt.reshape(B * T, D)
    xs = x_sample.reshape(Bs * Ls, D)
    p_states, s_states = [], []
    for l in range(depth):
        wl = w_in[l]
        w_main = jnp.concatenate([seg(wl, 0), seg(wl, 1), seg(wl, 2), seg(wl, 3), seg(wl, 7), seg(wl, 6), seg(wl, 10)],
                                 axis=1).astype(BF16)
        w_scal = jnp.concatenate([seg(wl, 4), seg(wl, 5), seg(wl, 8), seg(wl, 9),
                                  jnp.zeros((D, SCAL_COLS - 4 * HEADS), F32)], axis=1).astype(BF16)
        lw = {
            "norm_mix": norm_mix[l], "norm_ffn": norm_ffn[l], "w_in": w_main, "w_in_scal": w_scal,
            "w_out": w_out[l].astype(BF16), "w_up": ffn_w_up[l].astype(BF16), "w_down": ffn_w_down[l].astype(BF16),
            "da_q_norm": da_q_norm[l], "da_k_norm": da_k_norm[l], "gdn_conv_w": gdn_conv_w[l],
            "gdn_neg_a": -jnp.exp(gdn_A_log[l].astype(F32)), "gdn_dt_bias": gdn_dt_bias[l], "gdn_out_norm": gdn_out_norm[l],
            "ml_i_bias": ml_i_bias[l], "ml_f_bias": ml_f_bias[l], "ml_out_norm": ml_out_norm[l],
            "ffn_conv_w": ffn_conv_w[l],
        }
        lam_init = 0.8 - 0.6 * math.exp(-0.3 * l)
        lam = (jnp.exp(jnp.sum(da_lq1[l].astype(F32) * da_lk1[l].astype(F32)))
               - jnp.exp(jnp.sum(da_lq2[l].astype(F32) * da_lk2[l].astype(F32))) + lam_init).reshape(1, 1)
        out_scale = 1.0 - lam_init
        gain_o = da_out_norm[l]

        zero = {"gconv": jnp.zeros((B, GDN_CONV - 1, QKV_COLS), F32), "S": jnp.zeros((B, HEADS, HD, HD), F32),
                "C": jnp.zeros((B, HEADS, HD, HD), F32), "n": jnp.zeros((B, HEADS, HD), F32),
                "m": jnp.zeros((B, HEADS), F32), "fconv": jnp.zeros((B, FFN_CONV - 1, Fd), F32)}
        xm, st_m = _layer(xm, B, NM, NM, lw, zero,
                          lambda q, k, v: attn_small(q, k, v, None, bias_new_m, lam, gain_o, out_scale, B, NM))

        kmeta = jnp.pad(st_m["k16"].reshape(B, NM, W), ((0, 0), (0, MP - NM), (0, 0)))
        kmeta_t = jnp.transpose(kmeta, (0, 2, 1)).reshape(B * W, MP)
        vmeta = jnp.pad(st_m["v16"].reshape(B, NM, W), ((0, 0), (0, MP - NM), (0, 0))).reshape(B * MP, W)
        xb, st_b = _layer(xb, B, T, CHUNK, lw, st_m,
                          lambda q, kt, v: attn_big(q, kt, v, kmeta_t, vmeta, bias_big, bias_far, bias_meta_big, lam, gain_o,
                                                    out_scale, B, T, tq, tk, ATT_RB), want_kt=True)

        init_s = {"gconv": state_gdn_conv[l], "S": state_gdn_S[l].astype(F32), "C": state_mlstm_C[l].astype(F32),
                  "n": state_mlstm_n[l].astype(F32), "m": state_mlstm_m[l].astype(F32), "fconv": state_ffn_conv[l]}
        cache = (cache_attn_k[l].reshape(Bs * P, W).astype(BF16), cache_attn_v[l].reshape(Bs * P, W).astype(BF16),
                 bias_cache, tkc)
        xs, st_s = _layer(xs, Bs, Ls, Ls, lw, init_s,
                          lambda q, k, v: attn_small(q, k, v, cache, bias_new_s, lam, gain_o, out_scale, Bs, Ls))

        names = ("k", "v", "gconv", "S", "C", "n", "m", "fconv")
        p_states.append([jnp.concatenate([st_m[n_], st_b[n_]], axis=1) if n_ in ("k", "v") else st_b[n_]
                         for n_ in names])
        s_states.append([st_s[n_] for n_ in names])

    p_out = [jnp.stack([p_states[l][i] for l in range(depth)]) for i in range(8)]
    s_out = [jnp.stack([s_states[l][i] for l in range(depth)]) for i in range(8)]
    y_prompt = xb.reshape(B, T, D)
    y_sample = xs.reshape(Bs, Ls, D)
    return (y_prompt, y_sample, *p_out, *s_out)
```

```python
import functools
import math

import jax
import jax.numpy as jnp
from jax import lax
from jax.experimental import pallas as pl
from jax.experimental.pallas import tpu as pltpu

F32 = jnp.float32
BF16 = jnp.bfloat16

HEADS = 4
HD = 128
DA_DV = 2 * HD
CHUNK = 64
GDN_CONV = 4
FFN_CONV = 3
NUM_BUCKETS = 32
MAX_DISTANCE = 128
RMS_EPS = 1e-6
NEG_INF = -1e30
HALO = 8

QKV_COLS = 3 * HEADS * HD
Z_COLS = 3 * HEADS * DA_DV + 2 * QKV_COLS + 2 * HEADS * HD
SCAL_COLS = 128

LOG2E = math.log2(math.e)
Q_SCALE = HD ** -0.5 * LOG2E

VMEM_LIMIT = 56 * 1024 * 1024


def _pick(n, cands):
    for c in cands:
        if n % c == 0:
            return c
    raise ValueError(f"no tile for {n} in {cands}")


def _params(sem):
    return pltpu.CompilerParams(dimension_semantics=sem, vmem_limit_bytes=VMEM_LIMIT)


def _sigmoid(x):
    return 0.5 * jnp.tanh(0.5 * x) + 0.5


def _softplus(x):
    return jnp.maximum(x, 0.0) + jnp.log(1.0 + jnp.exp(-jnp.abs(x)))


def _dot_nt(a, b):
    return lax.dot_general(a, b, (((1,), (1,)), ((), ())), preferred_element_type=F32)


def _rms_matmul_kernel(x_ref, g_ref, w_ref, *rest, has_extra):
    if has_extra:
        ws_ref, o_ref, os_ref, xn_ref = rest
    else:
        o_ref, xn_ref = rest

    @pl.when(pl.program_id(1) == 0)
    def _():
        x = x_ref[...]
        ms = jnp.mean(x * x, axis=-1, keepdims=True)
        xn = (x * lax.rsqrt(ms + RMS_EPS) * g_ref[...]).astype(BF16)
        xn_ref[...] = xn
        if has_extra:
            os_ref[...] = jnp.dot(xn, ws_ref[...], preferred_element_type=F32)

    o_ref[...] = jnp.dot(xn_ref[...], w_ref[...], preferred_element_type=F32).astype(o_ref.dtype)


def rms_matmul(x, gain, w, w_extra=None, name="rms_matmul"):
    M, K = x.shape
    N = w.shape[1]
    tm = _pick(M, (1024, 512, 256, 128, 64, 32, 16, 8))
    tn = _pick(N, (512, 256, 128))
    has_extra = w_extra is not None
    in_specs = [pl.BlockSpec((tm, K), lambda i, j: (i, 0)),
                pl.BlockSpec((1, K), lambda i, j: (0, 0)),
                pl.BlockSpec((K, tn), lambda i, j: (0, j))]
    out_specs = [pl.BlockSpec((tm, tn), lambda i, j: (i, j))]
    out_shape = [jax.ShapeDtypeStruct((M, N), F32)]
    args = [x, gain.reshape(1, K), w]
    if has_extra:
        ne = w_extra.shape[1]
        in_specs.append(pl.BlockSpec((K, ne), lambda i, j: (0, 0)))
        out_specs.append(pl.BlockSpec((tm, ne), lambda i, j: (i, 0)))
        out_shape.append(jax.ShapeDtypeStruct((M, ne), F32))
        args.append(w_extra)
    outs = pl.pallas_call(
        functools.partial(_rms_matmul_kernel, has_extra=has_extra),
        grid=(M // tm, N // tn),
        in_specs=in_specs, out_specs=out_specs, out_shape=out_shape,
        scratch_shapes=[pltpu.VMEM((tm, K), BF16)],
        compiler_params=_params(("parallel", "arbitrary")),
        name=name,
    )(*args)
    return outs if has_extra else outs[0]


def _matmul_res_kernel(*refs, n_a):
    a_refs, w_refs = refs[:n_a], refs[n_a:2 * n_a]
    res_ref, o_ref = refs[2 * n_a], refs[2 * n_a + 1]
    acc = res_ref[...]
    for a_ref, w_ref in zip(a_refs, w_refs):
        acc = acc + jnp.dot(a_ref[...], w_ref[...], preferred_element_type=F32)
    o_ref[...] = acc


def matmul_res(a_list, w, res, name="matmul_res"):
    M, N = res.shape
    ktot = sum(a.shape[1] for a in a_list)
    tm = _pick(M, (1024, 512, 256, 128, 64, 32, 16, 8) if ktot <= 4096 else (512, 256, 128, 64, 32, 16, 8))
    tn = _pick(N, (512, 256, 128))
    in_specs, off = [], 0
    for a in a_list:
        in_specs.append(pl.BlockSpec((tm, a.shape[1]), lambda i, j: (i, 0)))
    for a in a_list:
        k = a.shape[1]
        assert off % k == 0
        in_specs.append(pl.BlockSpec((k, tn), lambda i, j, _o=off // k: (_o, j)))
        off += k
    in_specs.append(pl.BlockSpec((tm, tn), lambda i, j: (i, j)))
    return pl.pallas_call(
        functools.partial(_matmul_res_kernel, n_a=len(a_list)),
        grid=(M // tm, N // tn),
        in_specs=in_specs,
        out_specs=pl.BlockSpec((tm, tn), lambda i, j: (i, j)),
        out_shape=jax.ShapeDtypeStruct((M, N), F32),
        compiler_params=_params(("parallel", "arbitrary")),
        name=name,
    )(*a_list, *([w] * len(a_list)), res)


def _qk_prep_kernel(q_ref, k_ref, v_ref, gq_ref, gk_ref, qn_ref, kn32_ref, kn16_ref, v16_ref, *rest):
    gq = gq_ref[...]
    gk = gk_ref[...]
    groups = [slice(g * HD, (g + 1) * HD) for g in range(2 * HEADS)]
    q = [q_ref[:, sl] for sl in groups]
    k = [k_ref[:, sl] for sl in groups]
    q2 = [jnp.mean(x * x, axis=-1, keepdims=True) for x in q]
    k2 = [jnp.mean(x * x, axis=-1, keepdims=True) for x in k]
    for sl, q_, k_, q2_, k2_ in zip(groups, q, k, q2, k2):
        qn = q_ * lax.rsqrt(q2_ + RMS_EPS) * gq
        kn = k_ * lax.rsqrt(k2_ + RMS_EPS) * gk
        qn_ref[:, sl] = (qn * Q_SCALE).astype(BF16)
        kn32_ref[:, sl] = kn
        kn16_ref[:, sl] = kn.astype(BF16)
        if rest:
            rest[0][sl, :] = kn.T.astype(BF16)
    v16_ref[...] = v_ref[...].astype(BF16)


def qk_prep(z, gq, gk, want_kt=False):
    M = z.shape[0]
    W = 2 * HEADS * HD
    tm = _pick(M, (512, 256, 128, 64, 32, 16, 8))
    col = lambda c: pl.BlockSpec((tm, W), lambda i, _c=c: (i, _c))
    vec = pl.BlockSpec((1, HD), lambda i: (0, 0))
    row = pl.BlockSpec((tm, W), lambda i: (i, 0))
    out_specs = [row, row, row, row]
    out_shape = [jax.ShapeDtypeStruct((M, W), BF16), jax.ShapeDtypeStruct((M, W), F32),
                 jax.ShapeDtypeStruct((M, W), BF16), jax.ShapeDtypeStruct((M, W), BF16)]
    if want_kt:
        out_specs.append(pl.BlockSpec((W, tm), lambda i: (0, i)))
        out_shape.append(jax.ShapeDtypeStruct((W, M), BF16))
    return pl.pallas_call(
        _qk_prep_kernel,
        grid=(M // tm,),
        in_specs=[col(0), col(1), col(2), vec, vec],
        out_specs=out_specs,
        out_shape=out_shape,
        compiler_params=_params(("parallel",)),
        name="qk_prep",
    )(z, z, z, gq.reshape(1, HD), gk.reshape(1, HD))


def _softmax_update(mi, s, v, m_ref, l_ref, acc_ref):
    m_prev = m_ref[mi]
    m_new = jnp.maximum(m_prev, jnp.max(s, axis=-1, keepdims=True))
    alpha = jnp.exp2(m_prev - m_new)
    p = jnp.exp2(s - m_new)
    l_ref[mi] = alpha * l_ref[mi] + jnp.sum(p, axis=-1, keepdims=True)
    acc_ref[mi] = alpha * acc_ref[mi] + jnp.dot(p.astype(BF16), v, preferred_element_type=F32)
    m_ref[mi] = m_new


def _attn_init(m_ref, l_ref, acc_ref):
    m_ref[...] = jnp.full(m_ref.shape, -jnp.inf, F32)
    l_ref[...] = jnp.zeros(l_ref.shape, F32)
    acc_ref[...] = jnp.zeros(acc_ref.shape, F32)


def _attn_tile(q, k, v, bias, m_ref, l_ref, acc_ref):
    for mi in range(2):
        sl = slice(mi * HD, (mi + 1) * HD)
        s = _dot_nt(q[:, sl], k[:, sl]) + bias
        _softmax_update(mi, s, v, m_ref, l_ref, acc_ref)


def _attn_finish(lam, out_scale, g_ref, o_ref, l_ref, acc_ref):
    o = acc_ref[0] / l_ref[0] - lam * (acc_ref[1] / l_ref[1])
    on = o * lax.rsqrt(jnp.mean(o * o, axis=-1, keepdims=True) + RMS_EPS) * g_ref[...]
    o_ref[...] = (on * out_scale).astype(o_ref.dtype)


def _attn_scratch(tq):
    return [pltpu.VMEM((2, tq, 1), F32), pltpu.VMEM((2, tq, 1), F32), pltpu.VMEM((2, tq, DA_DV), F32)]


LANES = 128
ATT_TQ = (1024, 512, 256, 128, 64)
ATT_TK = (1024, 512, 256, 128, 64)
ATT_RB = 256


def _rowblock_tile(q_ref, kt_ref, v_ref, bias_ref, shift, m_ref, l_ref, acc_ref, rb):
    tq = q_ref.shape[0]
    tk = kt_ref.shape[1]
    nb = tk // LANES
    rb = min(rb, tq)
    nrb = tq // rb
    v = v_ref[...]

    def scores(r):
        rows = pl.ds(r * rb, rb)
        out = [jnp.dot(q_ref[rows, mi * HD:(mi + 1) * HD], kt_ref[mi * HD:(mi + 1) * HD, :],
                       preferred_element_type=F32) for mi in range(2)]
        if bias_ref is not None:
            bias = bias_ref[0, 0, rows, :]
            out = [s + bias for s in out]
        return out

    def update(r, s_pair):
        rows = pl.ds(r * rb, rb)
        for mi in range(2):
            s = s_pair[mi]
            blocks = [s[:, c * LANES:(c + 1) * LANES] for c in range(nb)]
            smax = blocks[0]
            for blk in blocks[1:]:
                smax = jnp.maximum(smax, blk)
            m_prev = m_ref[mi, rows, :]
            if shift is not None:
                m_prev = m_prev - shift
            m_new = jnp.maximum(m_prev, jnp.max(smax, axis=-1, keepdims=True))
            alpha = jnp.exp2(m_prev - m_new)
            ps = [jnp.exp2(blk - m_new) for blk in blocks]
            psum = ps[0]
            for pb in ps[1:]:
                psum = psum + pb
            l_ref[mi, rows, :] = alpha * l_ref[mi, rows, :] + psum
            m_ref[mi, rows, :] = m_new if shift is None else m_new + shift
            p = jnp.concatenate(ps, axis=1).astype(BF16) if nb > 1 else ps[0].astype(BF16)
            pv = jnp.dot(p, v, preferred_element_type=F32)
            acc_ref[mi, rows, :] = jnp.concatenate([alpha, alpha], axis=1) * acc_ref[mi, rows, :] + pv

    s_next = scores(0)
    for r in range(nrb):
        s_cur = s_next
        if r + 1 < nrb:
            s_next = scores(r + 1)
        update(r, s_cur)


def _attn_big_kernel(qi_ref, kj_ref, lam_ref, far_ref, q_ref, k_ref, v_ref, km_ref, vm_ref, bias_ref, bmeta_ref,
                     g_ref, o_ref, m_ref, l_ref, acc_ref, *, out_scale, ratio, rb):
    step = pl.program_id(2)
    i = qi_ref[step]
    j = kj_ref[step]
    is_far = j - ratio * i < -1

    @pl.when(j == 0)
    def _():
        _attn_init(m_ref, l_ref, acc_ref)
        _rowblock_tile(q_ref, km_ref, vm_ref, bmeta_ref, None, m_ref, l_ref, acc_ref, rb)

    @pl.when(is_far)
    def _():
        _rowblock_tile(q_ref, k_ref, v_ref, None, far_ref[pl.program_id(1)], m_ref, l_ref, acc_ref, rb)

    @pl.when(jnp.logical_not(is_far))
    def _():
        _rowblock_tile(q_ref, k_ref, v_ref, bias_ref, None, m_ref, l_ref, acc_ref, rb)

    @pl.when(j == ratio * (i + 1) - 1)
    def _():
        l0 = jnp.sum(l_ref[0], axis=-1, keepdims=True)
        l1 = jnp.sum(l_ref[1], axis=-1, keepdims=True)
        o = acc_ref[0] / l0 - lam_ref[0, 0] * (acc_ref[1] / l1)
        on = o * lax.rsqrt(jnp.mean(o * o, axis=-1, keepdims=True) + RMS_EPS) * g_ref[...]
        o_ref[...] = (on * out_scale).astype(o_ref.dtype)


def attn_big(qn, knt16, v16, kmeta_t, vmeta, bias, far, bmeta, lam, gain, out_scale, B, T, tq, tk, rb):
    nq, nk = T // tq, T // tk
    ratio = tq // tk
    mp = vmeta.shape[0] // B
    pairs = [(i, j) for i in range(nq) for j in range(ratio * (i + 1))]
    qi = jnp.asarray([p[0] for p in pairs], jnp.int32)
    kj = jnp.asarray([p[1] for p in pairs], jnp.int32)
    grid_spec = pltpu.PrefetchScalarGridSpec(
        num_scalar_prefetch=2,
        grid=(B, HEADS, len(pairs)),
        in_specs=[
            pl.BlockSpec(memory_space=pltpu.SMEM),
            pl.BlockSpec(memory_space=pltpu.SMEM),
            pl.BlockSpec((tq, DA_DV), lambda b, h, s, qi, kj: (b * nq + qi[s], h)),
            pl.BlockSpec((DA_DV, tk), lambda b, h, s, qi, kj: (h, b * nk + kj[s])),
            pl.BlockSpec((tk, DA_DV), lambda b, h, s, qi, kj: (b * nk + kj[s], h)),
            pl.BlockSpec((DA_DV, mp), lambda b, h, s, qi, kj: (b * HEADS + h, 0)),
            pl.BlockSpec((mp, DA_DV), lambda b, h, s, qi, kj: (b, h)),
            pl.BlockSpec((1, 1, tq, tk),
                         lambda b, h, s, qi, kj: (h, jnp.maximum(kj[s] - ratio * qi[s] + 1, 0), 0, 0)),
            pl.BlockSpec((1, 1, tq, mp), lambda b, h, s, qi, kj: (h, jnp.minimum(qi[s], 1), 0, 0)),
            pl.BlockSpec((1, DA_DV), lambda b, h, s, qi, kj: (0, 0)),
        ],
        out_specs=pl.BlockSpec((tq, DA_DV), lambda b, h, s, qi, kj: (b * nq + qi[s], h)),
        scratch_shapes=[pltpu.VMEM((2, tq, LANES), F32), pltpu.VMEM((2, tq, LANES), F32),
                        pltpu.VMEM((2, tq, DA_DV), F32)],
    )
    return pl.pallas_call(
        functools.partial(_attn_big_kernel, out_scale=out_scale, ratio=ratio, rb=rb),
        grid_spec=grid_spec,
        out_shape=jax.ShapeDtypeStruct((B * T, HEADS * DA_DV), BF16),
        compiler_params=_params(("parallel", "parallel", "arbitrary")),
        name="attn_prompt",
    )(qi, kj, lam, far, qn, knt16, v16, kmeta_t, vmeta, bias, bmeta, gain.reshape(1, DA_DV))


def _attn_small_kernel(lam_ref, q_ref, *rest, nc, out_scale):
    if nc > 0:
        ck_ref, cv_ref, bc_ref, k_ref, v_ref, bn_ref, g_ref, o_ref, m_ref, l_ref, acc_ref = rest
    else:
        k_ref, v_ref, bn_ref, g_ref, o_ref, m_ref, l_ref, acc_ref = rest
    j = pl.program_id(2)
    q = q_ref[...]

    @pl.when(j == 0)
    def _():
        _attn_init(m_ref, l_ref, acc_ref)

    if nc > 0:
        @pl.when(j < nc)
        def _():
            _attn_tile(q, ck_ref[...].astype(BF16), cv_ref[...].astype(BF16), bc_ref[0, 0], m_ref, l_ref, acc_ref)

    @pl.when(j == nc)
    def _():
        _attn_tile(q, k_ref[...], v_ref[...], bn_ref[0], m_ref, l_ref, acc_ref)
        _attn_finish(lam_ref[0, 0], out_scale, g_ref, o_ref, l_ref, acc_ref)


def attn_small(qn, kn16, v16, cache, bias_new, lam, gain, out_scale, B, L):
    W = HEADS * DA_DV
    blk = pl.BlockSpec((L, DA_DV), lambda b, h, j: (b, h))
    in_specs = [pl.BlockSpec(memory_space=pltpu.SMEM), blk]
    args = [lam, qn]
    nc = 0
    if cache is not None:
        ck, cv, bc, tk = cache
        nc = ck.shape[0] // B // tk
        cspec = pl.BlockSpec((tk, DA_DV), lambda b, h, j: (b * nc + jnp.minimum(j, nc - 1), h))
        in_specs += [cspec, cspec,
                     pl.BlockSpec((1, 1, L, tk), lambda b, h, j: (h, jnp.where(j >= nc - 1, 1, 0), 0, 0))]
        args += [ck, cv, bc]
    in_specs += [blk, blk, pl.BlockSpec((1, L, L), lambda b, h, j: (h, 0, 0)),
                 pl.BlockSpec((1, DA_DV), lambda b, h, j: (0, 0))]
    args += [kn16, v16, bias_new, gain.reshape(1, DA_DV)]
    return pl.pallas_call(
        functools.partial(_attn_small_kernel, nc=nc, out_scale=out_scale),
        grid=(B, HEADS, nc + 1),
        in_specs=in_specs,
        out_specs=blk,
        out_shape=jax.ShapeDtypeStruct((B * L, W), BF16),
        scratch_shapes=_attn_scratch(L),
        compiler_params=_params(("parallel", "parallel", "arbitrary")),
        name="attn_block",
    )(*args)


_NN = (((1,), (0,)), ((), ()))
_NT = (((1,), (1,)), ((), ()))
_TN = (((0,), (0,)), ((), ()))


def _dg(a, b, dn):
    return lax.dot_general(a, b, dn, preferred_element_type=F32)


def _bf16r(a):
    return a.astype(BF16).astype(F32)


def _mm3(a, b, dn=_NN):
    ah = _bf16r(a)
    al = a - ah
    bh = _bf16r(b)
    bl = b - bh
    return _dg(ah, bh, dn) + (_dg(ah, bl, dn) + _dg(al, bh, dn))


def _split3(a):
    hi = _bf16r(a)
    r1 = a - hi
    mid = _bf16r(r1)
    return hi, mid, r1 - mid


def _chunk_masks(L):
    row = lax.broadcasted_iota(jnp.int32, (L, L), 0)
    col = lax.broadcasted_iota(jnp.int32, (L, L), 1)
    return row, col


def _cumsum_cols_rows(x, tril, triu):
    hi, mid, lo = _split3(x)
    cols = _dg(tril, hi, _NN) + (_dg(tril, mid, _NN) + _dg(tril, lo, _NN))
    rows = _dg(hi, triu, _TN) + (_dg(mid, triu, _TN) + _dg(lo, triu, _TN))
    return cols, rows


def _gdn_prep_kernel(x_ref, halo_ref, prev_ref, sc_ref, cw_ref, nega_ref, dtb_ref,
                     w_ref, u_ref, qg_ref, kd_ref, qk_ref, gc_ref, xbuf, *, L, G):
    i = pl.program_id(1)
    tile = L * G
    xbuf[0:HALO, :] = jnp.where(i == 0, prev_ref[0], halo_ref[...])
    xbuf[HALO:HALO + tile, :] = x_ref[...]
    cw = cw_ref[...]
    y = xbuf[HALO - 3:HALO - 3 + tile, :] * cw[0:1, :]
    for t in range(1, GDN_CONV):
        y = y + xbuf[HALO - 3 + t:HALO - 3 + t + tile, :] * cw[t:t + 1, :]
    gcv = y * _sigmoid(y)

    row, col = _chunk_masks(L)
    incl = col <= row
    strict = col < row
    eye = (col == row).astype(F32)
    tril = incl.astype(F32)
    triu = (row <= col).astype(F32)
    sc = sc_ref[...]
    beta_all = _sigmoid(sc)
    g_all = nega_ref[...] * _softplus(sc + dtb_ref[...])
    nqk = HEADS * HD

    ids = [(c, h) for c in range(G) for h in range(HEADS)]
    cum = [_cumsum_cols_rows(g_all[c * L:(c + 1) * L], tril, triu) for c in range(G)]
    for c in range(G):
        gc_ref[c * L:(c + 1) * L, :] = cum[c][0]
    xq = [gcv[c * L:(c + 1) * L, h * HD:(h + 1) * HD] for c, h in ids]
    xk = [gcv[c * L:(c + 1) * L, nqk + h * HD:nqk + (h + 1) * HD] for c, h in ids]
    xv = [gcv[c * L:(c + 1) * L, 2 * nqk + h * HD:2 * nqk + (h + 1) * HD] for c, h in ids]
    q2 = [jnp.sum(x * x, axis=-1, keepdims=True) for x in xq]
    k2 = [jnp.sum(x * x, axis=-1, keepdims=True) for x in xk]
    qn = [x * lax.rsqrt(s2 + RMS_EPS) * (HD ** -0.5) for x, s2 in zip(xq, q2)]
    kn = [x * lax.rsqrt(s2 + RMS_EPS) for x, s2 in zip(xk, k2)]
    beta = [beta_all[c * L:(c + 1) * L, h:h + 1] for c, h in ids]
    gcol = [cum[c][0][:, HEADS + h:HEADS + h + 1] for c, h in ids]
    grow = [cum[c][1][HEADS + h:HEADS + h + 1, :] for c, h in ids]
    dec = [jnp.where(incl, jnp.exp(jnp.where(incl, a - b, 0.0)), 0.0) for a, b in zip(gcol, grow)]
    eg = [jnp.exp(a) for a in gcol]
    st = [dict(rows=slice(c * L, (c + 1) * L), h=h, qn=qn[t], kn=kn[t], bv=beta[t] * xv[t],
               bek=beta[t] * eg[t] * kn[t], beta=beta[t], dec=dec[t], eg=eg[t],
               kd=kn[t] * jnp.exp(gcol[t][L - 1:L, :] - gcol[t])) for t, (c, h) in enumerate(ids)]

    kk = [_mm3(s["kn"], s["kn"], _NT) for s in st]
    nmat = [jnp.where(strict, s["beta"] * k_ * s["dec"], 0.0) for s, k_ in zip(st, kk)]
    tinv = [eye - n for n in nmat]
    pw = [_mm3(n, n) for n in nmat]
    p = 2
    while True:
        tinv = [t + _mm3(t, q) for t, q in zip(tinv, pw)]
        p *= 2
        if p >= L:
            break
        pw = [_mm3(q, q) for q in pw]
    w_mat = [_mm3(t, s["bek"]) for t, s in zip(tinv, st)]
    u_mat = [_mm3(t, s["bv"]) for t, s in zip(tinv, st)]
    qk = [_mm3(s["qn"], s["kn"], _NT) * s["dec"] for s in st]
    for s, w_, u_, qk_ in zip(st, w_mat, u_mat, qk):
        rows, h = s["rows"], s["h"]
        sl = slice(h * HD, (h + 1) * HD)
        w_ref[rows, sl] = w_
        u_ref[rows, sl] = u_
        qg_ref[rows, sl] = s["eg"] * s["qn"]
        kd_ref[rows, sl] = s["kd"]
        qk_ref[rows, h * L:(h + 1) * L] = qk_


def _gdn_scan_kernel(w_ref, u_ref, qg_ref, kd_ref, qk_ref, gc_ref, gate_ref, s0_ref, gn_ref,
                     o_ref, sout_ref, s_scr, *, L, cps, B):
    i = pl.program_id(0)

    @pl.when(i == 0)
    def _():
        s_scr[...] = s0_ref[...]

    chains = [(b, h) for b in range(B) for h in range(HEADS)]
    gn = gn_ref[...]
    for c in range(cps):
        rows = slice(c * L, (c + 1) * L)
        last = (c + 1) * L - 1
        s_old = [s_scr[b, h] for b, h in chains]
        ws = [_mm3(w_ref[b, rows, h * HD:(h + 1) * HD], s_) for (b, h), s_ in zip(chains, s_old)]
        u = [u_ref[b, rows, h * HD:(h + 1) * HD] - w_ for (b, h), w_ in zip(chains, ws)]
        upd = [_mm3(kd_ref[b, rows, h * HD:(h + 1) * HD], u_, _TN) for (b, h), u_ in zip(chains, u)]
        ds = [jnp.exp(gc_ref[b, last:last + 1, HEADS + h:HEADS + h + 1]) for b, h in chains]
        for (b, h), s_, d_, ds_ in zip(chains, s_old, upd, ds):
            s_scr[b, h] = ds_ * s_ + d_
        o1 = [_mm3(qg_ref[b, rows, h * HD:(h + 1) * HD], s_) for (b, h), s_ in zip(chains, s_old)]
        o2 = [_mm3(qk_ref[b, rows, h * L:(h + 1) * L], u_) for (b, h), u_ in zip(chains, u)]
        o = [a_ + b_ for a_, b_ in zip(o1, o2)]
        ms = [jnp.mean(o_ * o_, axis=-1, keepdims=True) for o_ in o]
        for (b, h), o_, ms_ in zip(chains, o, ms):
            sl = slice(h * HD, (h + 1) * HD)
            on = o_ * lax.rsqrt(ms_ + RMS_EPS) * gn
            gt = gate_ref[b, rows, sl]
            o_ref[b, rows, sl] = (on * (gt * _sigmoid(gt))).astype(o_ref.dtype)

    @pl.when(i == pl.num_programs(0) - 1)
    def _():
        sout_ref[...] = s_scr[...]


def gdn(z, sc, prev8, conv_w, neg_a, dt_bias, s0, gn, B, T, L):
    M = B * T
    nchunks = T // L
    G = _pick(nchunks, (2, 1))
    tile = L * G
    nt = T // tile
    hb = tile // HALO
    nega_vec = jnp.zeros((1, SCAL_COLS), F32).at[0, HEADS:2 * HEADS].set(neg_a)
    dtb_vec = jnp.zeros((1, SCAL_COLS), F32).at[0, HEADS:2 * HEADS].set(dt_bias.astype(F32))
    rowblk = lambda w: pl.BlockSpec((tile, w), lambda b, i: (b * nt + i, 0))
    vec = pl.BlockSpec((1, SCAL_COLS), lambda b, i: (0, 0))
    w_mat, u_mat, qg, kd, qk, gc = pl.pallas_call(
        functools.partial(_gdn_prep_kernel, L=L, G=G),
        grid=(B, nt),
        in_specs=[pl.BlockSpec((tile, QKV_COLS), lambda b, i: (b * nt + i, 2)),
                  pl.BlockSpec((HALO, QKV_COLS), lambda b, i: (jnp.maximum((b * nt + i) * hb - 1, 0), 2)),
                  pl.BlockSpec((1, HALO, QKV_COLS), lambda b, i: (b, 0, 0)),
                  rowblk(SCAL_COLS),
                  pl.BlockSpec((GDN_CONV, QKV_COLS), lambda b, i: (0, 0)),
                  vec, vec],
        out_specs=[rowblk(HEADS * HD)] * 4 + [rowblk(HEADS * L), rowblk(SCAL_COLS)],
        out_shape=[jax.ShapeDtypeStruct((M, HEADS * HD), F32)] * 4
                  + [jax.ShapeDtypeStruct((M, HEADS * L), F32), jax.ShapeDtypeStruct((M, SCAL_COLS), F32)],
        scratch_shapes=[pltpu.VMEM((tile + HALO, QKV_COLS), F32)],
        compiler_params=_params(("parallel", "parallel")),
        name="gdn_prep",
    )(z, z, prev8, sc, conv_w, nega_vec, dtb_vec)

    cps = _pick(nchunks, (4, 2, 1))
    rows = cps * L
    r3 = lambda a: a.reshape(B, T, a.shape[-1])
    blk = lambda w, cb=0: pl.BlockSpec((B, rows, w), lambda i, _c=cb: (0, i, _c))
    st = pl.BlockSpec((B, HEADS, HD, HD), lambda i: (0, 0, 0, 0))
    o, s_new = pl.pallas_call(
        functools.partial(_gdn_scan_kernel, L=L, cps=cps, B=B),
        grid=(nchunks // cps,),
        in_specs=[blk(HEADS * HD)] * 4 + [blk(HEADS * L), blk(SCAL_COLS), blk(HEADS * HD, 12), st,
                                          pl.BlockSpec((1, HD), lambda i: (0, 0))],
        out_specs=[blk(HEADS * HD), st],
        out_shape=[jax.ShapeDtypeStruct((B, T, HEADS * HD), BF16), jax.ShapeDtypeStruct((B, HEADS, HD, HD), F32)],
        scratch_shapes=[pltpu.VMEM((B, HEADS, HD, HD), F32)],
        compiler_params=_params(("arbitrary",)),
        name="gdn_scan",
    )(r3(w_mat), r3(u_mat), r3(qg), r3(kd), r3(qk), r3(gc), r3(z), s0, gn.reshape(1, HD))
    return o.reshape(M, HEADS * HD), s_new


def _mlstm_prep_kernel(x_ref, sc_ref, ib_ref, fb_ref, pv_ref, u_ref, n_ref, st_ref, *, L, G):
    row, col = _chunk_masks(L)
    incl = col <= row
    eye = (col == row).astype(F32)
    tril = incl.astype(F32)
    triu = (row <= col).astype(F32)
    sc = sc_ref[...]
    ig_all = sc + ib_ref[...]
    lf_all = -_softplus(-(sc + fb_ref[...]))
    lane = lax.broadcasted_iota(jnp.int32, (L, SCAL_COLS), 1)
    nqk = HEADS * HD
    ids = [(c, h) for c in range(G) for h in range(HEADS)]
    crow = lambda c: slice(c * L, (c + 1) * L)
    cum = [_cumsum_cols_rows(lf_all[crow(c)], tril, triu) for c in range(G)]
    ig3 = [_split3(ig_all[crow(c)]) for c in range(G)]
    ig_rows = [_dg(a, eye, _TN) + (_dg(b, eye, _TN) + _dg(c_, eye, _TN)) for a, b, c_ in ig3]
    bcol = [cum[c][0][:, 3 * HEADS + h:3 * HEADS + h + 1] for c, h in ids]
    brow = [cum[c][1][3 * HEADS + h:3 * HEADS + h + 1, :] for c, h in ids]
    igrow = [ig_rows[c][2 * HEADS + h:2 * HEADS + h + 1, :] for c, h in ids]
    igcol = [ig_all[crow(c), 2 * HEADS + h:2 * HEADS + h + 1] for c, h in ids]
    dmat = [jnp.where(incl, a - b + g_, NEG_INF) for a, b, g_ in zip(bcol, brow, igrow)]
    dmax = [jnp.max(d_, axis=-1, keepdims=True) for d_ in dmat]
    q = [x_ref[crow(c), h * HD:(h + 1) * HD] for c, h in ids]
    k = [x_ref[crow(c), nqk + h * HD:nqk + (h + 1) * HD] * (HD ** -0.5) for c, h in ids]
    v = [x_ref[crow(c), 2 * nqk + h * HD:2 * nqk + (h + 1) * HD] for c, h in ids]
    kw = [jnp.exp(b_[L - 1:L, :] - b_ + g_ - m_[L - 1:L, :]) * k_ for b_, g_, m_, k_ in zip(bcol, igcol, dmax, k)]
    w = [jnp.exp(d_ - m_) for d_, m_ in zip(dmat, dmax)]
    qk = [_mm3(q_, k_, _NT) * w_ for q_, k_, w_ in zip(q, k, w)]
    pv = [_mm3(a_, v_) for a_, v_ in zip(qk, v)]
    um = [_mm3(kw_, v_, _TN) for kw_, v_ in zip(kw, v)]
    rs = [jnp.sum(a_, axis=-1, keepdims=True) for a_ in qk]
    ns = [jnp.sum(kw_, axis=0, keepdims=True) for kw_ in kw]
    for t, (c, h) in enumerate(ids):
        sl = slice(h * HD, (h + 1) * HD)
        pv_ref[crow(c), sl] = pv[t]
        u_ref[c * HD:(c + 1) * HD, sl] = um[t]
        n_ref[c * 8:(c + 1) * 8, sl] = jnp.broadcast_to(ns[t], (8, HD))
    for c in range(G):
        stats = jnp.where(lane >= 3 * HEADS, cum[c][0], 0.0)
        for h in range(HEADS):
            stats = jnp.where(lane == h, dmax[c * HEADS + h], stats)
            stats = jnp.where(lane == HEADS + h, rs[c * HEADS + h], stats)
        st_ref[crow(c), :] = stats


def _mlstm_scan_kernel(x_ref, og_ref, pv_ref, u_ref, n_ref, st_ref, c0_ref, n0_ref, m0_ref, gn_ref,
                       o_ref, cout_ref, nout_ref, mout_ref, c_scr, n_scr, m_scr, *, L, cps, B):
    i = pl.program_id(0)

    @pl.when(i == 0)
    def _():
        c_scr[...] = c0_ref[...]
        n_scr[...] = n0_ref[...]
        m_scr[...] = m0_ref[...]

    chains = [(b, h) for b in range(B) for h in range(HEADS)]
    gn = gn_ref[...]
    for c in range(cps):
        rows = slice(c * L, (c + 1) * L)
        n_ch = range(len(chains))
        hs = [slice(h * HD, (h + 1) * HD) for _, h in chains]
        dmax = [st_ref[b, rows, h:h + 1] for b, h in chains]
        rsum = [st_ref[b, rows, HEADS + h:HEADS + h + 1] for b, h in chains]
        bcol = [st_ref[b, rows, 3 * HEADS + h:3 * HEADS + h + 1] for b, h in chains]
        m_prev = [m_scr[b, h, 0:1, 0:1] for b, h in chains]
        inter = [bcol[t] + m_prev[t] for t in n_ch]
        m_t = [jnp.maximum(inter[t], dmax[t]) for t in n_ch]
        m_new = [m_t[t][L - 1:L, :] for t in n_ch]
        f_new = [jnp.exp(dmax[t][L - 1:L, :] - m_new[t]) for t in n_ch]
        dec0 = [jnp.exp(bcol[t][L - 1:L, :] + m_prev[t] - m_new[t]) for t in n_ch]
        c_old = [c_scr[b, h] for b, h in chains]
        n_old = [n_scr[b, h] for b, h in chains]
        q = [x_ref[b, rows, hs[t]] for t, (b, _) in enumerate(chains)]
        for t, (b, h) in enumerate(chains):
            c_scr[b, h] = dec0[t] * c_old[t] + f_new[t] * u_ref[b, c * HD:(c + 1) * HD, hs[t]]
            n_scr[b, h] = dec0[t] * n_old[t] + f_new[t] * n_ref[b, c * 8:(c + 1) * 8, hs[t]]
            m_scr[b, h] = jnp.broadcast_to(m_new[t], (8, HD))
        qc = [_mm3(q[t], c_old[t]) for t in n_ch]
        w_inter = [jnp.exp(inter[t] - m_t[t]) for t in n_ch]
        e1 = [jnp.exp(dmax[t] - m_t[t]) for t in n_ch]
        qn = [jnp.sum(q[t] * n_old[t][0:1, :], axis=-1, keepdims=True) for t in n_ch]
        den = [w_inter[t] * qn[t] + e1[t] * rsum[t] for t in n_ch]
        scale = [1.0 / jnp.maximum(jnp.abs(den[t]), jnp.exp(-m_t[t])) for t in n_ch]
        hout = [(w_inter[t] * qc[t] + e1[t] * pv_ref[b, rows, hs[t]]) * scale[t] for t, (b, _) in enumerate(chains)]
        ms = [jnp.mean(hout[t] * hout[t], axis=-1, keepdims=True) for t in n_ch]
        for t, (b, h) in enumerate(chains):
            on = hout[t] * lax.rsqrt(ms[t] + RMS_EPS) * gn
            o_ref[b, rows, hs[t]] = (on * _sigmoid(og_ref[b, rows, hs[t]])).astype(o_ref.dtype)

    @pl.when(i == pl.num_programs(0) - 1)
    def _():
        cout_ref[...] = c_scr[...]
        nout_ref[...] = n_scr[...]
        mout_ref[...] = m_scr[...]


def mlstm(z, sc, i_bias, f_bias, c0, n0p, m0p, gn, B, T, L):
    M = B * T
    nchunks = T // L
    G = _pick(nchunks, (4, 2, 1))
    tile = L * G
    nt = T // tile
    ib_vec = jnp.zeros((1, SCAL_COLS), F32).at[0, 2 * HEADS:3 * HEADS].set(i_bias.astype(F32))
    fb_vec = jnp.zeros((1, SCAL_COLS), F32).at[0, 3 * HEADS:4 * HEADS].set(f_bias.astype(F32))
    rowblk = lambda r, w, cb=0: pl.BlockSpec((r, w), lambda b, i, _c=cb: (b * nt + i, _c))
    vec = pl.BlockSpec((1, SCAL_COLS), lambda b, i: (0, 0))
    nc_all = B * nchunks
    pv, um, nm, stats = pl.pallas_call(
        functools.partial(_mlstm_prep_kernel, L=L, G=G),
        grid=(B, nt),
        in_specs=[rowblk(tile, QKV_COLS, 3), rowblk(tile, SCAL_COLS), vec, vec],
        out_specs=[rowblk(tile, HEADS * HD), rowblk(G * HD, HEADS * HD), rowblk(G * 8, HEADS * HD),
                   rowblk(tile, SCAL_COLS)],
        out_shape=[jax.ShapeDtypeStruct((M, HEADS * HD), F32), jax.ShapeDtypeStruct((nc_all * HD, HEADS * HD), F32),
                   jax.ShapeDtypeStruct((nc_all * 8, HEADS * HD), F32), jax.ShapeDtypeStruct((M, SCAL_COLS), F32)],
        compiler_params=_params(("parallel", "parallel")),
        name="mlstm_prep",
    )(z, sc, ib_vec, fb_vec)

    cps = _pick(nchunks, (4, 2, 1))
    r3 = lambda a: a.reshape(B, a.shape[0] // B, a.shape[-1])
    blk = lambda r, w, cb=0: pl.BlockSpec((B, r, w), lambda i, _c=cb: (0, i, _c))
    st = pl.BlockSpec((B, HEADS, HD, HD), lambda i: (0, 0, 0, 0))
    vec8 = pl.BlockSpec((B, HEADS, 8, HD), lambda i: (0, 0, 0, 0))
    o, c_new, n_new, m_new = pl.pallas_call(
        functools.partial(_mlstm_scan_kernel, L=L, cps=cps, B=B),
        grid=(nchunks // cps,),
        in_specs=[blk(cps * L, HEADS * HD, 3 * QKV_COLS // (HEADS * HD)), blk(cps * L, HEADS * HD, 13),
                  blk(cps * L, HEADS * HD), blk(cps * HD, HEADS * HD), blk(cps * 8, HEADS * HD),
                  blk(cps * L, SCAL_COLS), st, vec8, vec8, pl.BlockSpec((1, HD), lambda i: (0, 0))],
        out_specs=[blk(cps * L, HEADS * HD), st, vec8, vec8],
        out_shape=[jax.ShapeDtypeStruct((B, T, HEADS * HD), BF16), jax.ShapeDtypeStruct((B, HEADS, HD, HD), F32),
                   jax.ShapeDtypeStruct((B, HEADS, 8, HD), F32), jax.ShapeDtypeStruct((B, HEADS, 8, HD), F32)],
        scratch_shapes=[pltpu.VMEM((B, HEADS, HD, HD), F32), pltpu.VMEM((B, HEADS, 8, HD), F32),
                        pltpu.VMEM((B, HEADS, 8, HD), F32)],
        compiler_params=_params(("arbitrary",)),
        name="mlstm_scan",
    )(r3(z), r3(z), r3(pv), r3(um), r3(nm), r3(stats), c0, n0p, m0p, gn.reshape(1, HD))
    return o.reshape(M, HEADS * HD), c_new, n_new, m_new


def _ffn_up_act_kernel(x_ref, xh_ref, g_ref, wg_ref, wv_ref, prev_ref, cw_ref, o_ref, tail_ref,
                       xn_ref, xhn_ref, buf, *, tm, tpb):
    i = pl.program_id(0)

    def norm(x):
        ms = jnp.mean(x * x, axis=-1, keepdims=True)
        return (x * lax.rsqrt(ms + RMS_EPS) * g_ref[...]).astype(BF16)

    @pl.when(pl.program_id(1) == 0)
    def _():
        xn_ref[...] = norm(x_ref[...])
        xh = norm(xh_ref[...])
        xhn_ref[...] = jnp.concatenate([xh, xh], axis=0)

    wg = wg_ref[...]
    gate = jnp.dot(xn_ref[...], wg, preferred_element_type=F32)
    val = jnp.dot(xn_ref[...], wv_ref[...], preferred_element_type=F32)
    halo = jnp.dot(xhn_ref[...], wg, preferred_element_type=F32)[0:HALO, :]
    buf[0:HALO, :] = jnp.where(i % tpb == 0, prev_ref[0], halo)
    buf[HALO:HALO + tm, :] = gate
    cwh = 0.5 * cw_ref[...]
    g = buf[...]
    yh = g[HALO:, :] * cwh[FFN_CONV - 1:FFN_CONV, :]
    for t in range(1, FFN_CONV):
        yh = yh + pltpu.roll(g, t, axis=0)[HALO:, :] * cwh[FFN_CONV - 1 - t:FFN_CONV - t, :]
    o_ref[...] = (yh * (jnp.tanh(yh) + 1.0) * val).astype(o_ref.dtype)
    tail_ref[0] = buf[tm:tm + HALO, :]


def ffn_up_act(x, gain, w_up, prev8, conv_w, B, T):
    M, K = x.shape
    Fd = conv_w.shape[1]
    tm = _pick(T, (1024, 512, 256, 128, 64, 32, 16, 8))
    tn = _pick(Fd, (512, 256, 128))
    tpb, nf = T // tm, Fd // tn
    hb = tm // HALO
    hmid, tails = pl.pallas_call(
        functools.partial(_ffn_up_act_kernel, tm=tm, tpb=tpb),
        grid=(M // tm, nf),
        in_specs=[pl.BlockSpec((tm, K), lambda i, j: (i, 0)),
                  pl.BlockSpec((HALO, K), lambda i, j: (jnp.maximum(i * hb - 1, 0), 0)),
                  pl.BlockSpec((1, K), lambda i, j: (0, 0)),
                  pl.BlockSpec((K, tn), lambda i, j: (0, j)),
                  pl.BlockSpec((K, tn), lambda i, j: (0, nf + j)),
                  pl.BlockSpec((1, HALO, tn), lambda i, j: (i // tpb, 0, j)),
                  pl.BlockSpec((FFN_CONV, tn), lambda i, j: (0, j))],
        out_specs=[pl.BlockSpec((tm, tn), lambda i, j: (i, j)),
                   pl.BlockSpec((1, HALO, tn), lambda i, j: (i, 0, j))],
        out_shape=[jax.ShapeDtypeStruct((M, Fd), BF16), jax.ShapeDtypeStruct((M // tm, HALO, Fd), F32)],
        scratch_shapes=[pltpu.VMEM((tm, K), BF16), pltpu.VMEM((2 * HALO, K), BF16),
                        pltpu.VMEM((tm + HALO, tn), F32)],
        compiler_params=_params(("parallel", "arbitrary")),
        name="ffn_up_act",
    )(x, x, gain.reshape(1, K), w_up, w_up, prev8, conv_w)
    return hmid, tails[tpb - 1::tpb]


def _t5_bucket(rel):
    half = NUM_BUCKETS // 2
    exact = half // 2
    n = jnp.abs(rel)
    nf = jnp.maximum(n, 1).astype(F32)
    large = exact + (jnp.log(nf / exact) / math.log(MAX_DISTANCE / exact) * (half - exact)).astype(jnp.int32)
    large = jnp.minimum(large, half - 1)
    return jnp.where(rel > 0, half, 0) + jnp.where(n < exact, n, large)


def _rel_bias_vec(rel_bias, rel):
    bucket = _t5_bucket(rel)
    rb = rel_bias.astype(F32) * LOG2E
    H = rb.shape[1]
    out = jnp.zeros((H,) + rel.shape, F32)
    for kb in range(NUM_BUCKETS):
        out = jnp.where((bucket == kb)[None], rb[kb].reshape((H,) + (1,) * rel.ndim), out)
    return out


def _rel_bias_const(rel_bias, rel, R, C):
    v = _rel_bias_vec(rel_bias, jnp.full((1,), rel, jnp.int32))
    return jnp.broadcast_to(v[:, :, None], (v.shape[0], R, C))


def _rel_bias_toeplitz(rel_bias, R, C, off):
    ar = lambda n: jnp.arange(n, dtype=jnp.int32)
    if R * C <= 256 * 1024 or R % LANES or C % LANES:
        return _rel_bias_vec(rel_bias, ar(C)[None, :] - ar(R)[:, None] + off)
    nr, nc = R // LANES, C // LANES
    deltas = jnp.arange(-(nr - 1), nc, dtype=jnp.int32)
    rel = LANES * deltas[:, None, None] + (ar(LANES)[None, None, :] - ar(LANES)[None, :, None]) + off
    small = _rel_bias_vec(rel_bias, rel)
    rows = [jnp.concatenate([small[:, e - a + nr - 1] for e in range(nc)], axis=-1) for a in range(nr)]
    return jnp.concatenate(rows, axis=-2)


def _pad_rows(a, rows):
    return jnp.pad(a, ((0, 0), (rows - a.shape[1], 0), (0, 0)))


def _layer(x, B, T, L, lw, init, attn_fn, want_kt=False):
    z, sc = rms_matmul(x, lw["norm_mix"], lw["w_in"], lw["w_in_scal"], name="in_proj")
    qn, kn32, kn16, v16, *knt = qk_prep(z, lw["da_q_norm"], lw["da_k_norm"], want_kt)
    o_da = attn_fn(qn, knt[0] if want_kt else kn16, v16)
    o_g, s_new = gdn(z, sc, _pad_rows(init["gconv"], HALO), lw["gdn_conv_w"], lw["gdn_neg_a"], lw["gdn_dt_bias"],
                     init["S"], lw["gdn_out_norm"], B, T, L)
    n0p = jnp.broadcast_to(init["n"][:, :, None, :], (B, HEADS, 8, HD))
    m0p = jnp.broadcast_to(init["m"][:, :, None, None], (B, HEADS, 8, HD))
    o_m, c_new, n_new, m_new = mlstm(z, sc, lw["ml_i_bias"], lw["ml_f_bias"], init["C"], n0p, m0p,
                                     lw["ml_out_norm"], B, T, L)
    x1 = matmul_res([o_da, o_g, o_m], lw["w_out"], x, name="out_proj")
    hmid, gate_tail = ffn_up_act(x1, lw["norm_ffn"], lw["w_up"], _pad_rows(init["fconv"], HALO), lw["ffn_conv_w"], B, T)
    x2 = matmul_res([hmid], lw["w_down"], x1, name="ffn_down")
    z3 = z.reshape(B, T, Z_COLS)
    st = {
        "k": kn32.reshape(B, T, HEADS, 2, HD),
        "v": z3[:, :, 2 * HEADS * DA_DV:3 * HEADS * DA_DV].reshape(B, T, HEADS, DA_DV),
        "gconv": z3[:, T - (GDN_CONV - 1):, 3 * HEADS * DA_DV:3 * HEADS * DA_DV + QKV_COLS],
        "S": s_new, "C": c_new, "n": n_new[:, :, 0, :], "m": m_new[:, :, 0, 0],
        "fconv": gate_tail[:, HALO - (FFN_CONV - 1):, :],
        "k16": kn16, "v16": v16,
    }
    return x2, st


def kernel(x_prompt, x_sample, cache_attn_k, cache_attn_v, state_gdn_conv, state_gdn_S, state_mlstm_C, state_mlstm_n, state_mlstm_m, state_ffn_conv, meta_tokens, rel_bias, norm_mix, norm_ffn, w_in, w_out, da_q_norm, da_k_norm, da_lq1, da_lk1, da_lq2, da_lk2, da_out_norm, gdn_conv_w, gdn_A_log, gdn_dt_bias, gdn_out_norm, ml_i_bias, ml_f_bias, ml_out_norm, ffn_w_up, ffn_conv_w, ffn_w_down):
    B, T, D = x_prompt.shape
    Bs, Ls, _ = x_sample.shape
    depth = w_in.shape[0]
    P = cache_attn_k.shape[2]
    NM = meta_tokens.shape[0]
    Fd = ffn_conv_w.shape[-1]
    W = HEADS * DA_DV
    assert T % CHUNK == 0 and Ls <= CHUNK and NM <= CHUNK and NM % 8 == 0 and Ls % 8 == 0

    tq = _pick(T, ATT_TQ)
    tk = _pick(tq, ATT_TK)
    tkc = _pick(P, (2048, 1024, 512, 256, 128))
    assert tk >= 96 or T == tk, "far prompt tiles must lie past the last distinct relative-position bucket"
    assert tkc >= 96 or P == tkc
    MP = 128

    r = jnp.arange(tq, dtype=jnp.int32)[:, None]
    c = jnp.arange(tk, dtype=jnp.int32)[None, :]
    tiles = []
    for d in range(-1, tq // tk):
        visible = ((c + tk * d) // CHUNK <= r // CHUNK)[None]
        tiles.append(jnp.where(visible, _rel_bias_toeplitz(rel_bias, tq, tk, tk * d), NEG_INF))
    bias_big = jnp.stack(tiles, axis=1)
    bias_far = _rel_bias_vec(rel_bias, jnp.full((1,), -2 * tk, jnp.int32))[:, 0]
    cm = jnp.arange(MP, dtype=jnp.int32)[None, :]
    bm0 = _rel_bias_toeplitz(rel_bias, tq, MP, -NM)
    bm1 = _rel_bias_const(rel_bias, -2 * tq, tq, MP)
    bias_meta_big = jnp.where((cm < NM)[None, None], jnp.stack([bm0, bm1], axis=1), NEG_INF)
    bias_new_s = _rel_bias_toeplitz(rel_bias, Ls, Ls, 0)
    bias_cache = jnp.stack([_rel_bias_const(rel_bias, -2 * tkc, Ls, tkc),
                            _rel_bias_toeplitz(rel_bias, Ls, tkc, -tkc)], axis=1)
    bias_new_m = _rel_bias_toeplitz(rel_bias, NM, NM, 0)

    sizes = (W, W, W, QKV_COLS, HEADS, HEADS, HEADS * HD, QKV_COLS, HEADS, HEADS, HEADS * HD)
    offs = [0]
    for s_ in sizes:
        offs.append(offs[-1] + s_)
    seg = lambda w, i: w[:, offs[i]:offs[i + 1]]

    xm = jnp.broadcast_to(meta_tokens.astype(F32)[None], (B, NM, D)).reshape(B * NM, D)
    xb = x_prompt.reshape(B * T, D)
    xs = x_sample.reshape(Bs * Ls, D)
    p_states, s_states = [], []
    for l in range(depth):
        wl = w_in[l]
        w_main = jnp.concatenate([seg(wl, 0), seg(wl, 1), seg(wl, 2), seg(wl, 3), seg(wl, 7), seg(wl, 6), seg(wl, 10)],
                                 axis=1).astype(BF16)
        w_scal = jnp.concatenate([seg(wl, 4), seg(wl, 5), seg(wl, 8), seg(wl, 9),
                                  jnp.zeros((D, SCAL_COLS - 4 * HEADS), F32)], axis=1).astype(BF16)
        lw = {
            "norm_mix": norm_mix[l], "norm_ffn": norm_ffn[l], "w_in": w_main, "w_in_scal": w_scal,
            "w_out": w_out[l].astype(BF16), "w_up": ffn_w_up[l].astype(BF16), "w_down": ffn_w_down[l].astype(BF16),
            "da_q_norm": da_q_norm[l], "da_k_norm": da_k_norm[l], "gdn_conv_w": gdn_conv_w[l],
            "gdn_neg_a": -jnp.exp(gdn_A_log[l].astype(F32)), "gdn_dt_bias": gdn_dt_bias[l], "gdn_out_norm": gdn_out_norm[l],
            "ml_i_bias": ml_i_bias[l], "ml_f_bias": ml_f_bias[l], "ml_out_norm": ml_out_norm[l],
            "ffn_conv_w": ffn_conv_w[l],
        }
        lam_init = 0.8 - 0.6 * math.exp(-0.3 * l)
        lam = (jnp.exp(jnp.sum(da_lq1[l].astype(F32) * da_lk1[l].astype(F32)))
               - jnp.exp(jnp.sum(da_lq2[l].astype(F32) * da_lk2[l].astype(F32))) + lam_init).reshape(1, 1)
        out_scale = 1.0 - lam_init
        gain_o = da_out_norm[l]

        zero = {"gconv": jnp.zeros((B, GDN_CONV - 1, QKV_COLS), F32), "S": jnp.zeros((B, HEADS, HD, HD), F32),
                "C": jnp.zeros((B, HEADS, HD, HD), F32), "n": jnp.zeros((B, HEADS, HD), F32),
                "m": jnp.zeros((B, HEADS), F32), "fconv": jnp.zeros((B, FFN_CONV - 1, Fd), F32)}
        xm, st_m = _layer(xm, B, NM, NM, lw, zero,
                          lambda q, k, v: attn_small(q, k, v, None, bias_new_m, lam, gain_o, out_scale, B, NM))

        kmeta = jnp.pad(st_m["k16"].reshape(B, NM, W), ((0, 0), (0, MP - NM), (0, 0)))
        kmeta_t = jnp.transpose(kmeta, (0, 2, 1)).reshape(B * W, MP)
        vmeta = jnp.pad(st_m["v16"].reshape(B, NM, W), ((0, 0), (0, MP - NM), (0, 0))).reshape(B * MP, W)
        xb, st_b = _layer(xb, B, T, CHUNK, lw, st_m,
                          lambda q, kt, v: attn_big(q, kt, v, kmeta_t, vmeta, bias_big, bias_far, bias_meta_big, lam, gain_o,
                                                    out_scale, B, T, tq, tk, ATT_RB), want_kt=True)

        init_s = {"gconv": state_gdn_conv[l], "S": state_gdn_S[l].astype(F32), "C": state_mlstm_C[l].astype(F32),
                  "n": state_mlstm_n[l].astype(F32), "m": state_mlstm_m[l].astype(F32), "fconv": state_ffn_conv[l]}
        cache = (cache_attn_k[l].reshape(Bs * P, W).astype(BF16), cache_attn_v[l].reshape(Bs * P, W).astype(BF16),
                 bias_cache, tkc)
        xs, st_s = _layer(xs, Bs, Ls, Ls, lw, init_s,
                          lambda q, k, v: attn_small(q, k, v, cache, bias_new_s, lam, gain_o, out_scale, Bs, Ls))

        names = ("k", "v", "gconv", "S", "C", "n", "m", "fconv")
        p_states.append([jnp.concatenate([st_m[n_], st_b[n_]], axis=1) if n_ in ("k", "v") else st_b[n_]
                         for n_ in names])
        s_states.append([st_s[n_] for n_ in names])

    p_out = [jnp.stack([p_states[l][i] for l in range(depth)]) for i in range(8)]
    s_out = [jnp.stack([s_states[l][i] for l in range(depth)]) for i in range(8)]
    y_prompt = xb.reshape(B, T, D)
    y_sample = xs.reshape(Bs, Ls, D)
    return (y_prompt, y_sample, *p_out, *s_out)
```

```python
import functools
import math

import jax
import jax.numpy as jnp
from jax import lax
from jax.experimental import pallas as pl
from jax.experimental.pallas import tpu as pltpu

F32 = jnp.float32
BF16 = jnp.bfloat16

HEADS = 4
HD = 128
DA_DV = 2 * HD
CHUNK = 64
GDN_CONV = 4
FFN_CONV = 3
NUM_BUCKETS = 32
MAX_DISTANCE = 128
RMS_EPS = 1e-6
NEG_INF = -1e30
HALO = 8

QKV_COLS = 3 * HEADS * HD
Z_COLS = 3 * HEADS * DA_DV + 2 * QKV_COLS + 2 * HEADS * HD
SCAL_COLS = 128

LOG2E = math.log2(math.e)
Q_SCALE = HD ** -0.5 * LOG2E

VMEM_LIMIT = 56 * 1024 * 1024


def _pick(n, cands):
    for c in cands:
        if n % c == 0:
            return c
    raise ValueError(f"no tile for {n} in {cands}")


def _params(sem):
    return pltpu.CompilerParams(dimension_semantics=sem, vmem_limit_bytes=VMEM_LIMIT)


def _sigmoid(x):
    return 0.5 * jnp.tanh(0.5 * x) + 0.5


def _softplus(x):
    return jnp.maximum(x, 0.0) + jnp.log(1.0 + jnp.exp(-jnp.abs(x)))


def _dot_nt(a, b):
    return lax.dot_general(a, b, (((1,), (1,)), ((), ())), preferred_element_type=F32)


def _rms_matmul_kernel(x_ref, g_ref, w_ref, *rest, has_extra):
    if has_extra:
        ws_ref, o_ref, os_ref, xn_ref = rest
    else:
        o_ref, xn_ref = rest

    @pl.when(pl.program_id(1) == 0)
    def _():
        x = x_ref[...]
        ms = jnp.mean(x * x, axis=-1, keepdims=True)
        xn = (x * lax.rsqrt(ms + RMS_EPS) * g_ref[...]).astype(BF16)
        xn_ref[...] = xn
        if has_extra:
            os_ref[...] = jnp.dot(xn, ws_ref[...], preferred_element_type=F32)

    o_ref[...] = jnp.dot(xn_ref[...], w_ref[...], preferred_element_type=F32).astype(o_ref.dtype)


def rms_matmul(x, gain, w, w_extra=None, name="rms_matmul"):
    M, K = x.shape
    N = w.shape[1]
    tm = _pick(M, (1024, 512, 256, 128, 64, 32, 16, 8))
    tn = _pick(N, (1024, 512, 256, 128))
    has_extra = w_extra is not None
    in_specs = [pl.BlockSpec((tm, K), lambda i, j: (i, 0)),
                pl.BlockSpec((1, K), lambda i, j: (0, 0)),
                pl.BlockSpec((K, tn), lambda i, j: (0, j))]
    out_specs = [pl.BlockSpec((tm, tn), lambda i, j: (i, j))]
    out_shape = [jax.ShapeDtypeStruct((M, N), F32)]
    args = [x, gain.reshape(1, K), w]
    if has_extra:
        ne = w_extra.shape[1]
        in_specs.append(pl.BlockSpec((K, ne), lambda i, j: (0, 0)))
        out_specs.append(pl.BlockSpec((tm, ne), lambda i, j: (i, 0)))
        out_shape.append(jax.ShapeDtypeStruct((M, ne), F32))
        args.append(w_extra)
    outs = pl.pallas_call(
        functools.partial(_rms_matmul_kernel, has_extra=has_extra),
        grid=(M // tm, N // tn),
        in_specs=in_specs, out_specs=out_specs, out_shape=out_shape,
        scratch_shapes=[pltpu.VMEM((tm, K), BF16)],
        compiler_params=_params(("parallel", "arbitrary")),
        name=name,
    )(*args)
    return outs if has_extra else outs[0]


def _matmul_res_kernel(*refs, n_a):
    a_refs, w_refs = refs[:n_a], refs[n_a:2 * n_a]
    res_ref, o_ref = refs[2 * n_a], refs[2 * n_a + 1]
    acc = res_ref[...]
    for a_ref, w_ref in zip(a_refs, w_refs):
        acc = acc + jnp.dot(a_ref[...], w_ref[...], preferred_element_type=F32)
    o_ref[...] = acc


def matmul_res(a_list, w, res, name="matmul_res"):
    M, N = res.shape
    ktot = sum(a.shape[1] for a in a_list)
    tm = _pick(M, (1024, 512, 256, 128, 64, 32, 16, 8))
    tn = _pick(N, (1024, 512, 256, 128) if ktot <= 4096 else (512, 256, 128))
    in_specs, off = [], 0
    for a in a_list:
        in_specs.append(pl.BlockSpec((tm, a.shape[1]), lambda i, j: (i, 0)))
    for a in a_list:
        k = a.shape[1]
        assert off % k == 0
        in_specs.append(pl.BlockSpec((k, tn), lambda i, j, _o=off // k: (_o, j)))
        off += k
    in_specs.append(pl.BlockSpec((tm, tn), lambda i, j: (i, j)))
    return pl.pallas_call(
        functools.partial(_matmul_res_kernel, n_a=len(a_list)),
        grid=(M // tm, N // tn),
        in_specs=in_specs,
        out_specs=pl.BlockSpec((tm, tn), lambda i, j: (i, j)),
        out_shape=jax.ShapeDtypeStruct((M, N), F32),
        compiler_params=_params(("parallel", "arbitrary")),
        name=name,
    )(*a_list, *([w] * len(a_list)), res)


def _qk_prep_kernel(q_ref, k_ref, v_ref, gq_ref, gk_ref, qn_ref, kn32_ref, kn16_ref, v16_ref, *rest):
    gq = gq_ref[...]
    gk = gk_ref[...]
    groups = [slice(g * HD, (g + 1) * HD) for g in range(2 * HEADS)]
    q = [q_ref[:, sl] for sl in groups]
    k = [k_ref[:, sl] for sl in groups]
    q2 = [jnp.mean(x * x, axis=-1, keepdims=True) for x in q]
    k2 = [jnp.mean(x * x, axis=-1, keepdims=True) for x in k]
    for sl, q_, k_, q2_, k2_ in zip(groups, q, k, q2, k2):
        qn = q_ * lax.rsqrt(q2_ + RMS_EPS) * gq
        kn = k_ * lax.rsqrt(k2_ + RMS_EPS) * gk
        qn_ref[:, sl] = (qn * Q_SCALE).astype(BF16)
        kn32_ref[:, sl] = kn
        kn16_ref[:, sl] = kn.astype(BF16)
        if rest:
            rest[0][sl, :] = kn.T.astype(BF16)
    v16_ref[...] = v_ref[...].astype(BF16)


def qk_prep(z, gq, gk, want_kt=False):
    M = z.shape[0]
    W = 2 * HEADS * HD
    tm = _pick(M, (512, 256, 128, 64, 32, 16, 8))
    col = lambda c: pl.BlockSpec((tm, W), lambda i, _c=c: (i, _c))
    vec = pl.BlockSpec((1, HD), lambda i: (0, 0))
    row = pl.BlockSpec((tm, W), lambda i: (i, 0))
    out_specs = [row, row, row, row]
    out_shape = [jax.ShapeDtypeStruct((M, W), BF16), jax.ShapeDtypeStruct((M, W), F32),
                 jax.ShapeDtypeStruct((M, W), BF16), jax.ShapeDtypeStruct((M, W), BF16)]
    if want_kt:
        out_specs.append(pl.BlockSpec((W, tm), lambda i: (0, i)))
        out_shape.append(jax.ShapeDtypeStruct((W, M), BF16))
    return pl.pallas_call(
        _qk_prep_kernel,
        grid=(M // tm,),
        in_specs=[col(0), col(1), col(2), vec, vec],
        out_specs=out_specs,
        out_shape=out_shape,
        compiler_params=_params(("parallel",)),
        name="qk_prep",
    )(z, z, z, gq.reshape(1, HD), gk.reshape(1, HD))


def _softmax_update(mi, s, v, m_ref, l_ref, acc_ref):
    m_prev = m_ref[mi]
    m_new = jnp.maximum(m_prev, jnp.max(s, axis=-1, keepdims=True))
    alpha = jnp.exp2(m_prev - m_new)
    p = jnp.exp2(s - m_new)
    l_ref[mi] = alpha * l_ref[mi] + jnp.sum(p, axis=-1, keepdims=True)
    acc_ref[mi] = alpha * acc_ref[mi] + jnp.dot(p.astype(BF16), v, preferred_element_type=F32)
    m_ref[mi] = m_new


def _attn_init(m_ref, l_ref, acc_ref):
    m_ref[...] = jnp.full(m_ref.shape, -jnp.inf, F32)
    l_ref[...] = jnp.zeros(l_ref.shape, F32)
    acc_ref[...] = jnp.zeros(acc_ref.shape, F32)


def _attn_tile(q, k, v, bias, m_ref, l_ref, acc_ref):
    for mi in range(2):
        sl = slice(mi * HD, (mi + 1) * HD)
        s = _dot_nt(q[:, sl], k[:, sl]) + bias
        _softmax_update(mi, s, v, m_ref, l_ref, acc_ref)


def _attn_finish(lam, out_scale, g_ref, o_ref, l_ref, acc_ref):
    o = acc_ref[0] / l_ref[0] - lam * (acc_ref[1] / l_ref[1])
    on = o * lax.rsqrt(jnp.mean(o * o, axis=-1, keepdims=True) + RMS_EPS) * g_ref[...]
    o_ref[...] = (on * out_scale).astype(o_ref.dtype)


def _attn_scratch(tq):
    return [pltpu.VMEM((2, tq, 1), F32), pltpu.VMEM((2, tq, 1), F32), pltpu.VMEM((2, tq, DA_DV), F32)]


LANES = 128
ATT_TQ = (1024, 512, 256, 128, 64)
ATT_TK = (1024, 512, 256, 128, 64)
ATT_RB = 256


def _rowblock_tile(q_ref, kt_ref, v_ref, bias_ref, shift, m_ref, l_ref, acc_ref, rb, diagonal=False):
    tq = q_ref.shape[0]
    tk = kt_ref.shape[1]
    rb = min(rb, tq)
    nrb = tq // rb
    assert rb % CHUNK == 0 or not diagonal
    ncols = lambda r: min(tk, (r + 1) * rb) if diagonal else tk

    def scores(r):
        rows = pl.ds(r * rb, rb)
        out = [jnp.dot(q_ref[rows, mi * HD:(mi + 1) * HD], kt_ref[mi * HD:(mi + 1) * HD, 0:ncols(r)],
                       preferred_element_type=F32) for mi in range(2)]
        if bias_ref is not None:
            bias = bias_ref[0, 0, rows, 0:ncols(r)]
            out = [s + bias for s in out]
        return out

    def update(r, s_pair):
        rows = pl.ds(r * rb, rb)
        nb = ncols(r) // LANES
        v = v_ref[0:ncols(r), :]
        for mi in range(2):
            s = s_pair[mi]
            blocks = [s[:, c * LANES:(c + 1) * LANES] for c in range(nb)]
            smax = blocks[0]
            for blk in blocks[1:]:
                smax = jnp.maximum(smax, blk)
            m_prev = m_ref[mi, rows, :]
            if shift is not None:
                m_prev = m_prev - shift
            m_new = jnp.maximum(m_prev, jnp.max(smax, axis=-1, keepdims=True))
            alpha = jnp.exp2(m_prev - m_new)
            ps = [jnp.exp2(blk - m_new) for blk in blocks]
            psum = ps[0]
            for pb in ps[1:]:
                psum = psum + pb
            l_ref[mi, rows, :] = alpha * l_ref[mi, rows, :] + psum
            m_ref[mi, rows, :] = m_new if shift is None else m_new + shift
            p = jnp.concatenate(ps, axis=1).astype(BF16) if nb > 1 else ps[0].astype(BF16)
            pv = jnp.dot(p, v, preferred_element_type=F32)
            acc_ref[mi, rows, :] = jnp.concatenate([alpha, alpha], axis=1) * acc_ref[mi, rows, :] + pv

    s_next = scores(0)
    for r in range(nrb):
        s_cur = s_next
        if r + 1 < nrb:
            s_next = scores(r + 1)
        update(r, s_cur)


def _attn_big_kernel(qi_ref, kj_ref, lam_ref, far_ref, q_ref, k_ref, v_ref, km_ref, vm_ref, bias_ref, bmeta_ref,
                     g_ref, o_ref, m_ref, l_ref, acc_ref, *, out_scale, ratio, rb):
    step = pl.program_id(2)
    i = qi_ref[step]
    j = kj_ref[step]
    is_far = j - ratio * i < -1

    @pl.when(j == 0)
    def _():
        _attn_init(m_ref, l_ref, acc_ref)
        _rowblock_tile(q_ref, km_ref, vm_ref, bmeta_ref, None, m_ref, l_ref, acc_ref, rb)

    @pl.when(is_far)
    def _():
        _rowblock_tile(q_ref, k_ref, v_ref, None, far_ref[pl.program_id(1)], m_ref, l_ref, acc_ref, rb)

    if ratio == 1:
        @pl.when(j == i - 1)
        def _():
            _rowblock_tile(q_ref, k_ref, v_ref, bias_ref, None, m_ref, l_ref, acc_ref, rb)

        @pl.when(j == i)
        def _():
            _rowblock_tile(q_ref, k_ref, v_ref, bias_ref, None, m_ref, l_ref, acc_ref, rb, diagonal=True)
    else:
        @pl.when(jnp.logical_not(is_far))
        def _():
            _rowblock_tile(q_ref, k_ref, v_ref, bias_ref, None, m_ref, l_ref, acc_ref, rb)

    @pl.when(j == ratio * (i + 1) - 1)
    def _():
        l0 = jnp.sum(l_ref[0], axis=-1, keepdims=True)
        l1 = jnp.sum(l_ref[1], axis=-1, keepdims=True)
        o = acc_ref[0] / l0 - lam_ref[0, 0] * (acc_ref[1] / l1)
        on = o * lax.rsqrt(jnp.mean(o * o, axis=-1, keepdims=True) + RMS_EPS) * g_ref[...]
        o_ref[...] = (on * out_scale).astype(o_ref.dtype)


def attn_big(qn, knt16, v16, kmeta_t, vmeta, bias, far, bmeta, lam, gain, out_scale, B, T, tq, tk, rb):
    nq, nk = T // tq, T // tk
    ratio = tq // tk
    mp = vmeta.shape[0] // B
    pairs = [(i, j) for i in range(nq) for j in range(ratio * (i + 1))]
    qi = jnp.asarray([p[0] for p in pairs], jnp.int32)
    kj = jnp.asarray([p[1] for p in pairs], jnp.int32)
    grid_spec = pltpu.PrefetchScalarGridSpec(
        num_scalar_prefetch=2,
        grid=(B, HEADS, len(pairs)),
        in_specs=[
            pl.BlockSpec(memory_space=pltpu.SMEM),
            pl.BlockSpec(memory_space=pltpu.SMEM),
            pl.BlockSpec((tq, DA_DV), lambda b, h, s, qi, kj: (b * nq + qi[s], h)),
            pl.BlockSpec((DA_DV, tk), lambda b, h, s, qi, kj: (h, b * nk + kj[s])),
            pl.BlockSpec((tk, DA_DV), lambda b, h, s, qi, kj: (b * nk + kj[s], h)),
            pl.BlockSpec((DA_DV, mp), lambda b, h, s, qi, kj: (b * HEADS + h, 0)),
            pl.BlockSpec((mp, DA_DV), lambda b, h, s, qi, kj: (b, h)),
            pl.BlockSpec((1, 1, tq, tk),
                         lambda b, h, s, qi, kj: (h, jnp.maximum(kj[s] - ratio * qi[s] + 1, 0), 0, 0)),
            pl.BlockSpec((1, 1, tq, mp), lambda b, h, s, qi, kj: (h, jnp.minimum(qi[s], 1), 0, 0)),
            pl.BlockSpec((1, DA_DV), lambda b, h, s, qi, kj: (0, 0)),
        ],
        out_specs=pl.BlockSpec((tq, DA_DV), lambda b, h, s, qi, kj: (b * nq + qi[s], h)),
        scratch_shapes=[pltpu.VMEM((2, tq, LANES), F32), pltpu.VMEM((2, tq, LANES), F32),
                        pltpu.VMEM((2, tq, DA_DV), F32)],
    )
    return pl.pallas_call(
        functools.partial(_attn_big_kernel, out_scale=out_scale, ratio=ratio, rb=rb),
        grid_spec=grid_spec,
        out_shape=jax.ShapeDtypeStruct((B * T, HEADS * DA_DV), BF16),
        compiler_params=_params(("parallel", "parallel", "arbitrary")),
        name="attn_prompt",
    )(qi, kj, lam, far, qn, knt16, v16, kmeta_t, vmeta, bias, bmeta, gain.reshape(1, DA_DV))


def _attn_small_kernel(lam_ref, q_ref, *rest, nc, out_scale):
    if nc > 0:
        ck_ref, cv_ref, bc_ref, k_ref, v_ref, bn_ref, g_ref, o_ref, m_ref, l_ref, acc_ref = rest
    else:
        k_ref, v_ref, bn_ref, g_ref, o_ref, m_ref, l_ref, acc_ref = rest
    j = pl.program_id(2)
    q = q_ref[...]

    @pl.when(j == 0)
    def _():
        _attn_init(m_ref, l_ref, acc_ref)

    if nc > 0:
        @pl.when(j < nc)
        def _():
            _attn_tile(q, ck_ref[...].astype(BF16), cv_ref[...].astype(BF16), bc_ref[0, 0], m_ref, l_ref, acc_ref)

    @pl.when(j == nc)
    def _():
        _attn_tile(q, k_ref[...], v_ref[...], bn_ref[0], m_ref, l_ref, acc_ref)
        _attn_finish(lam_ref[0, 0], out_scale, g_ref, o_ref, l_ref, acc_ref)


def attn_small(qn, kn16, v16, cache, bias_new, lam, gain, out_scale, B, L):
    W = HEADS * DA_DV
    blk = pl.BlockSpec((L, DA_DV), lambda b, h, j: (b, h))
    in_specs = [pl.BlockSpec(memory_space=pltpu.SMEM), blk]
    args = [lam, qn]
    nc = 0
    if cache is not None:
        ck, cv, bc, tk = cache
        nc = ck.shape[0] // B // tk
        cspec = pl.BlockSpec((tk, DA_DV), lambda b, h, j: (b * nc + jnp.minimum(j, nc - 1), h))
        in_specs += [cspec, cspec,
                     pl.BlockSpec((1, 1, L, tk), lambda b, h, j: (h, jnp.where(j >= nc - 1, 1, 0), 0, 0))]
        args += [ck, cv, bc]
    in_specs += [blk, blk, pl.BlockSpec((1, L, L), lambda b, h, j: (h, 0, 0)),
                 pl.BlockSpec((1, DA_DV), lambda b, h, j: (0, 0))]
    args += [kn16, v16, bias_new, gain.reshape(1, DA_DV)]
    return pl.pallas_call(
        functools.partial(_attn_small_kernel, nc=nc, out_scale=out_scale),
        grid=(B, HEADS, nc + 1),
        in_specs=in_specs,
        out_specs=blk,
        out_shape=jax.ShapeDtypeStruct((B * L, W), BF16),
        scratch_shapes=_attn_scratch(L),
        compiler_params=_params(("parallel", "parallel", "arbitrary")),
        name="attn_block",
    )(*args)


_NN = (((1,), (0,)), ((), ()))
_NT = (((1,), (1,)), ((), ()))
_TN = (((0,), (0,)), ((), ()))


def _dg(a, b, dn):
    return lax.dot_general(a, b, dn, preferred_element_type=F32)


def _bf16r(a):
    return a.astype(BF16).astype(F32)


def _mm3(a, b, dn=_NN):
    ah = _bf16r(a)
    al = a - ah
    bh = _bf16r(b)
    bl = b - bh
    return _dg(ah, bh, dn) + (_dg(ah, bl, dn) + _dg(al, bh, dn))


def _split3(a):
    hi = _bf16r(a)
    r1 = a - hi
    mid = _bf16r(r1)
    return hi, mid, r1 - mid


def _chunk_masks(L):
    row = lax.broadcasted_iota(jnp.int32, (L, L), 0)
    col = lax.broadcasted_iota(jnp.int32, (L, L), 1)
    return row, col


def _cumsum_cols_rows(x, tril, triu):
    hi, mid, lo = _split3(x)
    cols = _dg(tril, hi, _NN) + (_dg(tril, mid, _NN) + _dg(tril, lo, _NN))
    rows = _dg(hi, triu, _TN) + (_dg(mid, triu, _TN) + _dg(lo, triu, _TN))
    return cols, rows


def _gdn_prep_kernel(x_ref, halo_ref, prev_ref, sc_ref, cw_ref, nega_ref, dtb_ref,
                     w_ref, u_ref, qg_ref, kd_ref, qk_ref, gc_ref, xbuf, *, L, G):
    i = pl.program_id(1)
    tile = L * G
    xbuf[0:HALO, :] = jnp.where(i == 0, prev_ref[0], halo_ref[...])
    xbuf[HALO:HALO + tile, :] = x_ref[...]
    cw = cw_ref[...]
    y = xbuf[HALO - 3:HALO - 3 + tile, :] * cw[0:1, :]
    for t in range(1, GDN_CONV):
        y = y + xbuf[HALO - 3 + t:HALO - 3 + t + tile, :] * cw[t:t + 1, :]
    gcv = y * _sigmoid(y)

    row, col = _chunk_masks(L)
    incl = col <= row
    strict = col < row
    eye = (col == row).astype(F32)
    tril = incl.astype(F32)
    triu = (row <= col).astype(F32)
    sc = sc_ref[...]
    beta_all = _sigmoid(sc)
    g_all = nega_ref[...] * _softplus(sc + dtb_ref[...])
    nqk = HEADS * HD

    ids = [(c, h) for c in range(G) for h in range(HEADS)]
    cum = [_cumsum_cols_rows(g_all[c * L:(c + 1) * L], tril, triu) for c in range(G)]
    for c in range(G):
        gc_ref[c * L:(c + 1) * L, :] = cum[c][0]
    xq = [gcv[c * L:(c + 1) * L, h * HD:(h + 1) * HD] for c, h in ids]
    xk = [gcv[c * L:(c + 1) * L, nqk + h * HD:nqk + (h + 1) * HD] for c, h in ids]
    xv = [gcv[c * L:(c + 1) * L, 2 * nqk + h * HD:2 * nqk + (h + 1) * HD] for c, h in ids]
    q2 = [jnp.sum(x * x, axis=-1, keepdims=True) for x in xq]
    k2 = [jnp.sum(x * x, axis=-1, keepdims=True) for x in xk]
    qn = [x * lax.rsqrt(s2 + RMS_EPS) * (HD ** -0.5) for x, s2 in zip(xq, q2)]
    kn = [x * lax.rsqrt(s2 + RMS_EPS) for x, s2 in zip(xk, k2)]
    beta = [beta_all[c * L:(c + 1) * L, h:h + 1] for c, h in ids]
    gcol = [cum[c][0][:, HEADS + h:HEADS + h + 1] for c, h in ids]
    grow = [cum[c][1][HEADS + h:HEADS + h + 1, :] for c, h in ids]
    dec = [jnp.where(incl, jnp.exp(jnp.where(incl, a - b, 0.0)), 0.0) for a, b in zip(gcol, grow)]
    eg = [jnp.exp(a) for a in gcol]
    st = [dict(rows=slice(c * L, (c + 1) * L), h=h, qn=qn[t], kn=kn[t], bv=beta[t] * xv[t],
               bek=beta[t] * eg[t] * kn[t], beta=beta[t], dec=dec[t], eg=eg[t],
               kd=kn[t] * jnp.exp(gcol[t][L - 1:L, :] - gcol[t])) for t, (c, h) in enumerate(ids)]

    kk = [_mm3(s["kn"], s["kn"], _NT) for s in st]
    nmat = [jnp.where(strict, s["beta"] * k_ * s["dec"], 0.0) for s, k_ in zip(st, kk)]
    tinv = [eye - n for n in nmat]
    pw = [_mm3(n, n) for n in nmat]
    p = 2
    while True:
        tinv = [t + _mm3(t, q) for t, q in zip(tinv, pw)]
        p *= 2
        if p >= L:
            break
        pw = [_mm3(q, q) for q in pw]
    w_mat = [_mm3(t, s["bek"]) for t, s in zip(tinv, st)]
    u_mat = [_mm3(t, s["bv"]) for t, s in zip(tinv, st)]
    qk = [_mm3(s["qn"], s["kn"], _NT) * s["dec"] for s in st]
    for s, w_, u_, qk_ in zip(st, w_mat, u_mat, qk):
        rows, h = s["rows"], s["h"]
        sl = slice(h * HD, (h + 1) * HD)
        w_ref[rows, sl] = w_
        u_ref[rows, sl] = u_
        qg_ref[rows, sl] = s["eg"] * s["qn"]
        kd_ref[rows, sl] = s["kd"]
        qk_ref[rows, h * L:(h + 1) * L] = qk_


def _gdn_scan_kernel(w_ref, u_ref, qg_ref, kd_ref, qk_ref, gc_ref, gate_ref, s0_ref, gn_ref,
                     o_ref, sout_ref, s_scr, *, L, cps, B):
    i = pl.program_id(0)

    @pl.when(i == 0)
    def _():
        s_scr[...] = s0_ref[...]

    chains = [(b, h) for b in range(B) for h in range(HEADS)]
    gn = gn_ref[...]
    for c in range(cps):
        rows = slice(c * L, (c + 1) * L)
        last = (c + 1) * L - 1
        s_old = [s_scr[b, h] for b, h in chains]
        ws = [_mm3(w_ref[b, rows, h * HD:(h + 1) * HD], s_) for (b, h), s_ in zip(chains, s_old)]
        u = [u_ref[b, rows, h * HD:(h + 1) * HD] - w_ for (b, h), w_ in zip(chains, ws)]
        upd = [_mm3(kd_ref[b, rows, h * HD:(h + 1) * HD], u_, _TN) for (b, h), u_ in zip(chains, u)]
        ds = [jnp.exp(gc_ref[b, last:last + 1, HEADS + h:HEADS + h + 1]) for b, h in chains]
        for (b, h), s_, d_, ds_ in zip(chains, s_old, upd, ds):
            s_scr[b, h] = ds_ * s_ + d_
        o1 = [_mm3(qg_ref[b, rows, h * HD:(h + 1) * HD], s_) for (b, h), s_ in zip(chains, s_old)]
        o2 = [_mm3(qk_ref[b, rows, h * L:(h + 1) * L], u_) for (b, h), u_ in zip(chains, u)]
        o = [a_ + b_ for a_, b_ in zip(o1, o2)]
        ms = [jnp.mean(o_ * o_, axis=-1, keepdims=True) for o_ in o]
        for (b, h), o_, ms_ in zip(chains, o, ms):
            sl = slice(h * HD, (h + 1) * HD)
            on = o_ * lax.rsqrt(ms_ + RMS_EPS) * gn
            gt = gate_ref[b, rows, sl]
            o_ref[b, rows, sl] = (on * (gt * _sigmoid(gt))).astype(o_ref.dtype)

    @pl.when(i == pl.num_programs(0) - 1)
    def _():
        sout_ref[...] = s_scr[...]


def gdn(z, sc, prev8, conv_w, neg_a, dt_bias, s0, gn, B, T, L):
    M = B * T
    nchunks = T // L
    G = _pick(nchunks, (2, 1))
    tile = L * G
    nt = T // tile
    hb = tile // HALO
    nega_vec = jnp.zeros((1, SCAL_COLS), F32).at[0, HEADS:2 * HEADS].set(neg_a)
    dtb_vec = jnp.zeros((1, SCAL_COLS), F32).at[0, HEADS:2 * HEADS].set(dt_bias.astype(F32))
    rowblk = lambda w: pl.BlockSpec((tile, w), lambda b, i: (b * nt + i, 0))
    vec = pl.BlockSpec((1, SCAL_COLS), lambda b, i: (0, 0))
    w_mat, u_mat, qg, kd, qk, gc = pl.pallas_call(
        functools.partial(_gdn_prep_kernel, L=L, G=G),
        grid=(B, nt),
        in_specs=[pl.BlockSpec((tile, QKV_COLS), lambda b, i: (b * nt + i, 2)),
                  pl.BlockSpec((HALO, QKV_COLS), lambda b, i: (jnp.maximum((b * nt + i) * hb - 1, 0), 2)),
                  pl.BlockSpec((1, HALO, QKV_COLS), lambda b, i: (b, 0, 0)),
                  rowblk(SCAL_COLS),
                  pl.BlockSpec((GDN_CONV, QKV_COLS), lambda b, i: (0, 0)),
                  vec, vec],
        out_specs=[rowblk(HEADS * HD)] * 4 + [rowblk(HEADS * L), rowblk(SCAL_COLS)],
        out_shape=[jax.ShapeDtypeStruct((M, HEADS * HD), F32)] * 4
                  + [jax.ShapeDtypeStruct((M, HEADS * L), F32), jax.ShapeDtypeStruct((M, SCAL_COLS), F32)],
        scratch_shapes=[pltpu.VMEM((tile + HALO, QKV_COLS), F32)],
        compiler_params=_params(("parallel", "parallel")),
        name="gdn_prep",
    )(z, z, prev8, sc, conv_w, nega_vec, dtb_vec)

    cps = _pick(nchunks, (4, 2, 1))
    rows = cps * L
    r3 = lambda a: a.reshape(B, T, a.shape[-1])
    blk = lambda w, cb=0: pl.BlockSpec((B, rows, w), lambda i, _c=cb: (0, i, _c))
    st = pl.BlockSpec((B, HEADS, HD, HD), lambda i: (0, 0, 0, 0))
    o, s_new = pl.pallas_call(
        functools.partial(_gdn_scan_kernel, L=L, cps=cps, B=B),
        grid=(nchunks // cps,),
        in_specs=[blk(HEADS * HD)] * 4 + [blk(HEADS * L), blk(SCAL_COLS), blk(HEADS * HD, 12), st,
                                          pl.BlockSpec((1, HD), lambda i: (0, 0))],
        out_specs=[blk(HEADS * HD), st],
        out_shape=[jax.ShapeDtypeStruct((B, T, HEADS * HD), BF16), jax.ShapeDtypeStruct((B, HEADS, HD, HD), F32)],
        scratch_shapes=[pltpu.VMEM((B, HEADS, HD, HD), F32)],
        compiler_params=_params(("arbitrary",)),
        name="gdn_scan",
    )(r3(w_mat), r3(u_mat), r3(qg), r3(kd), r3(qk), r3(gc), r3(z), s0, gn.reshape(1, HD))
    return o.reshape(M, HEADS * HD), s_new


def _mlstm_prep_kernel(x_ref, sc_ref, ib_ref, fb_ref, pv_ref, u_ref, n_ref, st_ref, *, L, G):
    row, col = _chunk_masks(L)
    incl = col <= row
    eye = (col == row).astype(F32)
    tril = incl.astype(F32)
    triu = (row <= col).astype(F32)
    sc = sc_ref[...]
    ig_all = sc + ib_ref[...]
    lf_all = -_softplus(-(sc + fb_ref[...]))
    lane = lax.broadcasted_iota(jnp.int32, (L, SCAL_COLS), 1)
    nqk = HEADS * HD
    ids = [(c, h) for c in range(G) for h in range(HEADS)]
    crow = lambda c: slice(c * L, (c + 1) * L)
    cum = [_cumsum_cols_rows(lf_all[crow(c)], tril, triu) for c in range(G)]
    ig3 = [_split3(ig_all[crow(c)]) for c in range(G)]
    ig_rows = [_dg(a, eye, _TN) + (_dg(b, eye, _TN) + _dg(c_, eye, _TN)) for a, b, c_ in ig3]
    bcol = [cum[c][0][:, 3 * HEADS + h:3 * HEADS + h + 1] for c, h in ids]
    brow = [cum[c][1][3 * HEADS + h:3 * HEADS + h + 1, :] for c, h in ids]
    igrow = [ig_rows[c][2 * HEADS + h:2 * HEADS + h + 1, :] for c, h in ids]
    igcol = [ig_all[crow(c), 2 * HEADS + h:2 * HEADS + h + 1] for c, h in ids]
    dmat = [jnp.where(incl, a - b + g_, NEG_INF) for a, b, g_ in zip(bcol, brow, igrow)]
    dmax = [jnp.max(d_, axis=-1, keepdims=True) for d_ in dmat]
    q = [x_ref[crow(c), h * HD:(h + 1) * HD] for c, h in ids]
    k = [x_ref[crow(c), nqk + h * HD:nqk + (h + 1) * HD] * (HD ** -0.5) for c, h in ids]
    v = [x_ref[crow(c), 2 * nqk + h * HD:2 * nqk + (h + 1) * HD] for c, h in ids]
    kw = [jnp.exp(b_[L - 1:L, :] - b_ + g_ - m_[L - 1:L, :]) * k_ for b_, g_, m_, k_ in zip(bcol, igcol, dmax, k)]
    w = [jnp.exp(d_ - m_) for d_, m_ in zip(dmat, dmax)]
    qk = [_mm3(q_, k_, _NT) * w_ for q_, k_, w_ in zip(q, k, w)]
    pv = [_mm3(a_, v_) for a_, v_ in zip(qk, v)]
    um = [_mm3(kw_, v_, _TN) for kw_, v_ in zip(kw, v)]
    rs = [jnp.sum(a_, axis=-1, keepdims=True) for a_ in qk]
    ns = [jnp.sum(kw_, axis=0, keepdims=True) for kw_ in kw]
    for t, (c, h) in enumerate(ids):
        sl = slice(h * HD, (h + 1) * HD)
        pv_ref[crow(c), sl] = pv[t]
        u_ref[c * HD:(c + 1) * HD, sl] = um[t]
        n_ref[c * 8:(c + 1) * 8, sl] = jnp.broadcast_to(ns[t], (8, HD))
    for c in range(G):
        stats = jnp.where(lane >= 3 * HEADS, cum[c][0], 0.0)
        for h in range(HEADS):
            stats = jnp.where(lane == h, dmax[c * HEADS + h], stats)
            stats = jnp.where(lane == HEADS + h, rs[c * HEADS + h], stats)
        st_ref[crow(c), :] = stats


def _mlstm_scan_kernel(x_ref, og_ref, pv_ref, u_ref, n_ref, st_ref, c0_ref, n0_ref, m0_ref, gn_ref,
                       o_ref, cout_ref, nout_ref, mout_ref, c_scr, n_scr, m_scr, *, L, cps, B):
    i = pl.program_id(0)

    @pl.when(i == 0)
    def _():
        c_scr[...] = c0_ref[...]
        n_scr[...] = n0_ref[...]
        m_scr[...] = m0_ref[...]

    chains = [(b, h) for b in range(B) for h in range(HEADS)]
    gn = gn_ref[...]
    for c in range(cps):
        rows = slice(c * L, (c + 1) * L)
        n_ch = range(len(chains))
        hs = [slice(h * HD, (h + 1) * HD) for _, h in chains]
        dmax = [st_ref[b, rows, h:h + 1] for b, h in chains]
        rsum = [st_ref[b, rows, HEADS + h:HEADS + h + 1] for b, h in chains]
        bcol = [st_ref[b, rows, 3 * HEADS + h:3 * HEADS + h + 1] for b, h in chains]
        m_prev = [m_scr[b, h, 0:1, 0:1] for b, h in chains]
        inter = [bcol[t] + m_prev[t] for t in n_ch]
        m_t = [jnp.maximum(inter[t], dmax[t]) for t in n_ch]
        m_new = [m_t[t][L - 1:L, :] for t in n_ch]
        f_new = [jnp.exp(dmax[t][L - 1:L, :] - m_new[t]) for t in n_ch]
        dec0 = [jnp.exp(bcol[t][L - 1:L, :] + m_prev[t] - m_new[t]) for t in n_ch]
        c_old = [c_scr[b, h] for b, h in chains]
        n_old = [n_scr[b, h] for b, h in chains]
        q = [x_ref[b, rows, hs[t]] for t, (b, _) in enumerate(chains)]
        for t, (b, h) in enumerate(chains):
            c_scr[b, h] = dec0[t] * c_old[t] + f_new[t] * u_ref[b, c * HD:(c + 1) * HD, hs[t]]
            n_scr[b, h] = dec0[t] * n_old[t] + f_new[t] * n_ref[b, c * 8:(c + 1) * 8, hs[t]]
            m_scr[b, h] = jnp.broadcast_to(m_new[t], (8, HD))
        qc = [_mm3(q[t], c_old[t]) for t in n_ch]
        w_inter = [jnp.exp(inter[t] - m_t[t]) for t in n_ch]
        e1 = [jnp.exp(dmax[t] - m_t[t]) for t in n_ch]
        qn = [jnp.sum(q[t] * n_old[t][0:1, :], axis=-1, keepdims=True) for t in n_ch]
        den = [w_inter[t] * qn[t] + e1[t] * rsum[t] for t in n_ch]
        scale = [1.0 / jnp.maximum(jnp.abs(den[t]), jnp.exp(-m_t[t])) for t in n_ch]
        hout = [(w_inter[t] * qc[t] + e1[t] * pv_ref[b, rows, hs[t]]) * scale[t] for t, (b, _) in enumerate(chains)]
        ms = [jnp.mean(hout[t] * hout[t], axis=-1, keepdims=True) for t in n_ch]
        for t, (b, h) in enumerate(chains):
            on = hout[t] * lax.rsqrt(ms[t] + RMS_EPS) * gn
            o_ref[b, rows, hs[t]] = (on * _sigmoid(og_ref[b, rows, hs[t]])).astype(o_ref.dtype)

    @pl.when(i == pl.num_programs(0) - 1)
    def _():
        cout_ref[...] = c_scr[...]
        nout_ref[...] = n_scr[...]
        mout_ref[...] = m_scr[...]


def mlstm(z, sc, i_bias, f_bias, c0, n0p, m0p, gn, B, T, L):
    M = B * T
    nchunks = T // L
    G = _pick(nchunks, (4, 2, 1))
    tile = L * G
    nt = T // tile
    ib_vec = jnp.zeros((1, SCAL_COLS), F32).at[0, 2 * HEADS:3 * HEADS].set(i_bias.astype(F32))
    fb_vec = jnp.zeros((1, SCAL_COLS), F32).at[0, 3 * HEADS:4 * HEADS].set(f_bias.astype(F32))
    rowblk = lambda r, w, cb=0: pl.BlockSpec((r, w), lambda b, i, _c=cb: (b * nt + i, _c))
    vec = pl.BlockSpec((1, SCAL_COLS), lambda b, i: (0, 0))
    nc_all = B * nchunks
    pv, um, nm, stats = pl.pallas_call(
        functools.partial(_mlstm_prep_kernel, L=L, G=G),
        grid=(B, nt),
        in_specs=[rowblk(tile, QKV_COLS, 3), rowblk(tile, SCAL_COLS), vec, vec],
        out_specs=[rowblk(tile, HEADS * HD), rowblk(G * HD, HEADS * HD), rowblk(G * 8, HEADS * HD),
                   rowblk(tile, SCAL_COLS)],
        out_shape=[jax.ShapeDtypeStruct((M, HEADS * HD), F32), jax.ShapeDtypeStruct((nc_all * HD, HEADS * HD), F32),
                   jax.ShapeDtypeStruct((nc_all * 8, HEADS * HD), F32), jax.ShapeDtypeStruct((M, SCAL_COLS), F32)],
        compiler_params=_params(("parallel", "parallel")),
        name="mlstm_prep",
    )(z, sc, ib_vec, fb_vec)

    cps = _pick(nchunks, (4, 2, 1))
    r3 = lambda a: a.reshape(B, a.shape[0] // B, a.shape[-1])
    blk = lambda r, w, cb=0: pl.BlockSpec((B, r, w), lambda i, _c=cb: (0, i, _c))
    st = pl.BlockSpec((B, HEADS, HD, HD), lambda i: (0, 0, 0, 0))
    vec8 = pl.BlockSpec((B, HEADS, 8, HD), lambda i: (0, 0, 0, 0))
    o, c_new, n_new, m_new = pl.pallas_call(
        functools.partial(_mlstm_scan_kernel, L=L, cps=cps, B=B),
        grid=(nchunks // cps,),
        in_specs=[blk(cps * L, HEADS * HD, 3 * QKV_COLS // (HEADS * HD)), blk(cps * L, HEADS * HD, 13),
                  blk(cps * L, HEADS * HD), blk(cps * HD, HEADS * HD), blk(cps * 8, HEADS * HD),
                  blk(cps * L, SCAL_COLS), st, vec8, vec8, pl.BlockSpec((1, HD), lambda i: (0, 0))],
        out_specs=[blk(cps * L, HEADS * HD), st, vec8, vec8],
        out_shape=[jax.ShapeDtypeStruct((B, T, HEADS * HD), BF16), jax.ShapeDtypeStruct((B, HEADS, HD, HD), F32),
                   jax.ShapeDtypeStruct((B, HEADS, 8, HD), F32), jax.ShapeDtypeStruct((B, HEADS, 8, HD), F32)],
        scratch_shapes=[pltpu.VMEM((B, HEADS, HD, HD), F32), pltpu.VMEM((B, HEADS, 8, HD), F32),
                        pltpu.VMEM((B, HEADS, 8, HD), F32)],
        compiler_params=_params(("arbitrary",)),
        name="mlstm_scan",
    )(r3(z), r3(z), r3(pv), r3(um), r3(nm), r3(stats), c0, n0p, m0p, gn.reshape(1, HD))
    return o.reshape(M, HEADS * HD), c_new, n_new, m_new


def _ffn_up_act_kernel(x_ref, xh_ref, g_ref, wg_ref, wv_ref, prev_ref, cw_ref, o_ref, tail_ref,
                       xn_ref, xhn_ref, buf, *, tm, tpb):
    i = pl.program_id(0)

    def norm(x):
        ms = jnp.mean(x * x, axis=-1, keepdims=True)
        return (x * lax.rsqrt(ms + RMS_EPS) * g_ref[...]).astype(BF16)

    @pl.when(pl.program_id(1) == 0)
    def _():
        xn_ref[...] = norm(x_ref[...])
        xh = norm(xh_ref[...])
        xhn_ref[...] = jnp.concatenate([xh, xh], axis=0)

    wg = wg_ref[...]
    gate = jnp.dot(xn_ref[...], wg, preferred_element_type=F32)
    val = jnp.dot(xn_ref[...], wv_ref[...], preferred_element_type=F32)
    halo = jnp.dot(xhn_ref[...], wg, preferred_element_type=F32)[0:HALO, :]
    buf[0:HALO, :] = jnp.where(i % tpb == 0, prev_ref[0], halo)
    buf[HALO:HALO + tm, :] = gate
    cwh = 0.5 * cw_ref[...]
    g = buf[...]
    yh = g[HALO:, :] * cwh[FFN_CONV - 1:FFN_CONV, :]
    for t in range(1, FFN_CONV):
        yh = yh + pltpu.roll(g, t, axis=0)[HALO:, :] * cwh[FFN_CONV - 1 - t:FFN_CONV - t, :]
    o_ref[...] = (yh * (jnp.tanh(yh) + 1.0) * val).astype(o_ref.dtype)
    tail_ref[0] = buf[tm:tm + HALO, :]


def ffn_up_act(x, gain, w_up, prev8, conv_w, B, T):
    M, K = x.shape
    Fd = conv_w.shape[1]
    tm = _pick(T, (1024, 512, 256, 128, 64, 32, 16, 8))
    tn = _pick(Fd, (512, 256, 128))
    tpb, nf = T // tm, Fd // tn
    hb = tm // HALO
    hmid, tails = pl.pallas_call(
        functools.partial(_ffn_up_act_kernel, tm=tm, tpb=tpb),
        grid=(M // tm, nf),
        in_specs=[pl.BlockSpec((tm, K), lambda i, j: (i, 0)),
                  pl.BlockSpec((HALO, K), lambda i, j: (jnp.maximum(i * hb - 1, 0), 0)),
                  pl.BlockSpec((1, K), lambda i, j: (0, 0)),
                  pl.BlockSpec((K, tn), lambda i, j: (0, j)),
                  pl.BlockSpec((K, tn), lambda i, j: (0, nf + j)),
                  pl.BlockSpec((1, HALO, tn), lambda i, j: (i // tpb, 0, j)),
                  pl.BlockSpec((FFN_CONV, tn), lambda i, j: (0, j))],
        out_specs=[pl.BlockSpec((tm, tn), lambda i, j: (i, j)),
                   pl.BlockSpec((1, HALO, tn), lambda i, j: (i, 0, j))],
        out_shape=[jax.ShapeDtypeStruct((M, Fd), BF16), jax.ShapeDtypeStruct((M // tm, HALO, Fd), F32)],
        scratch_shapes=[pltpu.VMEM((tm, K), BF16), pltpu.VMEM((2 * HALO, K), BF16),
                        pltpu.VMEM((tm + HALO, tn), F32)],
        compiler_params=_params(("parallel", "arbitrary")),
        name="ffn_up_act",
    )(x, x, gain.reshape(1, K), w_up, w_up, prev8, conv_w)
    return hmid, tails[tpb - 1::tpb]


def _t5_bucket(rel):
    half = NUM_BUCKETS // 2
    exact = half // 2
    n = jnp.abs(rel)
    nf = jnp.maximum(n, 1).astype(F32)
    large = exact + (jnp.log(nf / exact) / math.log(MAX_DISTANCE / exact) * (half - exact)).astype(jnp.int32)
    large = jnp.minimum(large, half - 1)
    return jnp.where(rel > 0, half, 0) + jnp.where(n < exact, n, large)


def _rel_bias_vec(rel_bias, rel):
    bucket = _t5_bucket(rel)
    rb = rel_bias.astype(F32) * LOG2E
    H = rb.shape[1]
    out = jnp.zeros((H,) + rel.shape, F32)
    for kb in range(NUM_BUCKETS):
        out = jnp.where((bucket == kb)[None], rb[kb].reshape((H,) + (1,) * rel.ndim), out)
    return out


def _rel_bias_const(rel_bias, rel, R, C):
    v = _rel_bias_vec(rel_bias, jnp.full((1,), rel, jnp.int32))
    return jnp.broadcast_to(v[:, :, None], (v.shape[0], R, C))


def _rel_bias_toeplitz(rel_bias, R, C, off):
    ar = lambda n: jnp.arange(n, dtype=jnp.int32)
    if R * C <= 256 * 1024 or R % LANES or C % LANES:
        return _rel_bias_vec(rel_bias, ar(C)[None, :] - ar(R)[:, None] + off)
    nr, nc = R // LANES, C // LANES
    deltas = jnp.arange(-(nr - 1), nc, dtype=jnp.int32)
    rel = LANES * deltas[:, None, None] + (ar(LANES)[None, None, :] - ar(LANES)[None, :, None]) + off
    small = _rel_bias_vec(rel_bias, rel)
    rows = [jnp.concatenate([small[:, e - a + nr - 1] for e in range(nc)], axis=-1) for a in range(nr)]
    return jnp.concatenate(rows, axis=-2)


def _pad_rows(a, rows):
    return jnp.pad(a, ((0, 0), (rows - a.shape[1], 0), (0, 0)))


def _layer(x, B, T, L, lw, init, attn_fn, want_kt=False):
    z, sc = rms_matmul(x, lw["norm_mix"], lw["w_in"], lw["w_in_scal"], name="in_proj")
    qn, kn32, kn16, v16, *knt = qk_prep(z, lw["da_q_norm"], lw["da_k_norm"], want_kt)
    o_da = attn_fn(qn, knt[0] if want_kt else kn16, v16)
    o_g, s_new = gdn(z, sc, _pad_rows(init["gconv"], HALO), lw["gdn_conv_w"], lw["gdn_neg_a"], lw["gdn_dt_bias"],
                     init["S"], lw["gdn_out_norm"], B, T, L)
    n0p = jnp.broadcast_to(init["n"][:, :, None, :], (B, HEADS, 8, HD))
    m0p = jnp.broadcast_to(init["m"][:, :, None, None], (B, HEADS, 8, HD))
    o_m, c_new, n_new, m_new = mlstm(z, sc, lw["ml_i_bias"], lw["ml_f_bias"], init["C"], n0p, m0p,
                                     lw["ml_out_norm"], B, T, L)
    x1 = matmul_res([o_da, o_g, o_m], lw["w_out"], x, name="out_proj")
    hmid, gate_tail = ffn_up_act(x1, lw["norm_ffn"], lw["w_up"], _pad_rows(init["fconv"], HALO), lw["ffn_conv_w"], B, T)
    x2 = matmul_res([hmid], lw["w_down"], x1, name="ffn_down")
    z3 = z.reshape(B, T, Z_COLS)
    st = {
        "k": kn32.reshape(B, T, HEADS, 2, HD),
        "v": z3[:, :, 2 * HEADS * DA_DV:3 * HEADS * DA_DV].reshape(B, T, HEADS, DA_DV),
        "gconv": z3[:, T - (GDN_CONV - 1):, 3 * HEADS * DA_DV:3 * HEADS * DA_DV + QKV_COLS],
        "S": s_new, "C": c_new, "n": n_new[:, :, 0, :], "m": m_new[:, :, 0, 0],
        "fconv": gate_tail[:, HALO - (FFN_CONV - 1):, :],
        "k16": kn16, "v16": v16,
    }
    return x2, st


def kernel(x_prompt, x_sample, cache_attn_k, cache_attn_v, state_gdn_conv, state_gdn_S, state_mlstm_C, state_mlstm_n, state_mlstm_m, state_ffn_conv, meta_tokens, rel_bias, norm_mix, norm_ffn, w_in, w_out, da_q_norm, da_k_norm, da_lq1, da_lk1, da_lq2, da_lk2, da_out_norm, gdn_conv_w, gdn_A_log, gdn_dt_bias, gdn_out_norm, ml_i_bias, ml_f_bias, ml_out_norm, ffn_w_up, ffn_conv_w, ffn_w_down):
    B, T, D = x_prompt.shape
    Bs, Ls, _ = x_sample.shape
    depth = w_in.shape[0]
    P = cache_attn_k.shape[2]
    NM = meta_tokens.shape[0]
    Fd = ffn_conv_w.shape[-1]
    W = HEADS * DA_DV
    assert T % CHUNK == 0 and Ls <= CHUNK and NM <= CHUNK and NM % 8 == 0 and Ls % 8 == 0

    tq = _pick(T, ATT_TQ)
    tk = _pick(tq, ATT_TK)
    tkc = _pick(P, (2048, 1024, 512, 256, 128))
    assert tk >= 96 or T == tk, "far prompt tiles must lie past the last distinct relative-position bucket"
    assert tkc >= 96 or P == tkc
    MP = 128

    r = jnp.arange(tq, dtype=jnp.int32)[:, None]
    c = jnp.arange(tk, dtype=jnp.int32)[None, :]
    tiles = []
    for d in range(-1, tq // tk):
        visible = ((c + tk * d) // CHUNK <= r // CHUNK)[None]
        tiles.append(jnp.where(visible, _rel_bias_toeplitz(rel_bias, tq, tk, tk * d), NEG_INF))
    bias_big = jnp.stack(tiles, axis=1)
    bias_far = _rel_bias_vec(rel_bias, jnp.full((1,), -2 * tk, jnp.int32))[:, 0]
    cm = jnp.arange(MP, dtype=jnp.int32)[None, :]
    bm0 = _rel_bias_toeplitz(rel_bias, tq, MP, -NM)
    bm1 = _rel_bias_const(rel_bias, -2 * tq, tq, MP)
    bias_meta_big = jnp.where((cm < NM)[None, None], jnp.stack([bm0, bm1], axis=1), NEG_INF)
    bias_new_s = _rel_bias_toeplitz(rel_bias, Ls, Ls, 0)
    bias_cache = jnp.stack([_rel_bias_const(rel_bias, -2 * tkc, Ls, tkc),
                            _rel_bias_toeplitz(rel_bias, Ls, tkc, -tkc)], axis=1)
    bias_new_m = _rel_bias_toeplitz(rel_bias, NM, NM, 0)

    sizes = (W, W, W, QKV_COLS, HEADS, HEADS, HEADS * HD, QKV_COLS, HEADS, HEADS, HEADS * HD)
    offs = [0]
    for s_ in sizes:
        offs.append(offs[-1] + s_)
    seg = lambda w, i: w[:, offs[i]:offs[i + 1]]

    xm = jnp.broadcast_to(meta_tokens.astype(F32)[None], (B, NM, D)).reshape(B * NM, D)
    xb = x_prompt.reshape(B * T, D)
    xs = x_sample.reshape(Bs * Ls, D)
    p_states, s_states = [], []
    for l in range(depth):
        wl = w_in[l]
        w_main = jnp.concatenate([seg(wl, 0), seg(wl, 1), seg(wl, 2), seg(wl, 3), seg(wl, 7), seg(wl, 6), seg(wl, 10)],
                                 axis=1).astype(BF16)
        w_scal = jnp.concatenate([seg(wl, 4), seg(wl, 5), seg(wl, 8), seg(wl, 9),
                                  jnp.zeros((D, SCAL_COLS - 4 * HEADS), F32)], axis=1).astype(BF16)
        lw = {
            "norm_mix": norm_mix[l], "norm_ffn": norm_ffn[l], "w_in": w_main, "w_in_scal": w_scal,
            "w_out": w_out[l].astype(BF16), "w_up": ffn_w_up[l].astype(BF16), "w_down": ffn_w_down[l].astype(BF16),
            "da_q_norm": da_q_norm[l], "da_k_norm": da_k_norm[l], "gdn_conv_w": gdn_conv_w[l],
            "gdn_neg_a": -jnp.exp(gdn_A_log[l].astype(F32)), "gdn_dt_bias": gdn_dt_bias[l], "gdn_out_norm": gdn_out_norm[l],
            "ml_i_bias": ml_i_bias[l], "ml_f_bias": ml_f_bias[l], "ml_out_norm": ml_out_norm[l],
            "ffn_conv_w": ffn_conv_w[l],
        }
        lam_init = 0.8 - 0.6 * math.exp(-0.3 * l)
        lam = (jnp.exp(jnp.sum(da_lq1[l].astype(F32) * da_lk1[l].astype(F32)))
               - jnp.exp(jnp.sum(da_lq2[l].astype(F32) * da_lk2[l].astype(F32))) + lam_init).reshape(1, 1)
        out_scale = 1.0 - lam_init
        gain_o = da_out_norm[l]

        zero = {"gconv": jnp.zeros((B, GDN_CONV - 1, QKV_COLS), F32), "S": jnp.zeros((B, HEADS, HD, HD), F32),
                "C": jnp.zeros((B, HEADS, HD, HD), F32), "n": jnp.zeros((B, HEADS, HD), F32),
                "m": jnp.zeros((B, HEADS), F32), "fconv": jnp.zeros((B, FFN_CONV - 1, Fd), F32)}
        xm, st_m = _layer(xm, B, NM, NM, lw, zero,
                          lambda q, k, v: attn_small(q, k, v, None, bias_new_m, lam, gain_o, out_scale, B, NM))

        kmeta = jnp.pad(st_m["k16"].reshape(B, NM, W), ((0, 0), (0, MP - NM), (0, 0)))
        kmeta_t = jnp.transpose(kmeta, (0, 2, 1)).reshape(B * W, MP)
        vmeta = jnp.pad(st_m["v16"].reshape(B, NM, W), ((0, 0), (0, MP - NM), (0, 0))).reshape(B * MP, W)
        xb, st_b = _layer(xb, B, T, CHUNK, lw, st_m,
                          lambda q, kt, v: attn_big(q, kt, v, kmeta_t, vmeta, bias_big, bias_far, bias_meta_big, lam, gain_o,
                                                    out_scale, B, T, tq, tk, ATT_RB), want_kt=True)

        init_s = {"gconv": state_gdn_conv[l], "S": state_gdn_S[l].astype(F32), "C": state_mlstm_C[l].astype(F32),
                  "n": state_mlstm_n[l].astype(F32), "m": state_mlstm_m[l].astype(F32), "fconv": state_ffn_conv[l]}
        cache = (cache_attn_k[l].reshape(Bs * P, W).astype(BF16), cache_attn_v[l].reshape(Bs * P, W).astype(BF16),
                 bias_cache, tkc)
        xs, st_s = _layer(xs, Bs, Ls, Ls, lw, init_s,
                          lambda q, k, v: attn_small(q, k, v, cache, bias_new_s, lam, gain_o, out_scale, Bs, Ls))

        names = ("k", "v", "gconv", "S", "C", "n", "m", "fconv")
        p_states.append([jnp.concatenate([st_m[n_], st_b[n_]], axis=1) if n_ in ("k", "v") else st_b[n_]
                         for n_ in names])
        s_states.append([st_s[n_] for n_ in names])

    p_out = [jnp.stack([p_states[l][i] for l in range(depth)]) for i in range(8)]
    s_out = [jnp.stack([s_states[l][i] for l in range(depth)]) for i in range(8)]
    y_prompt = xb.reshape(B, T, D)
    y_sample = xs.reshape(Bs, Ls, D)
    return (y_prompt, y_sample, *p_out, *s_out)
```

```python
import functools
import math

import jax
import jax.numpy as jnp
from jax import lax
from jax.experimental import pallas as pl
from jax.experimental.pallas import tpu as pltpu

F32 = jnp.float32
BF16 = jnp.bfloat16

HEADS = 4
HD = 128
DA_DV = 2 * HD
CHUNK = 64
GDN_CONV = 4
FFN_CONV = 3
NUM_BUCKETS = 32
MAX_DISTANCE = 128
RMS_EPS = 1e-6
NEG_INF = -1e30
HALO = 8

QKV_COLS = 3 * HEADS * HD
Z_COLS = 3 * HEADS * DA_DV + 2 * QKV_COLS + 2 * HEADS * HD
SCAL_COLS = 128

LOG2E = math.log2(math.e)
Q_SCALE = HD ** -0.5 * LOG2E

VMEM_LIMIT = 56 * 1024 * 1024


def _pick(n, cands):
    for c in cands:
        if n % c == 0:
            return c
    raise ValueError(f"no tile for {n} in {cands}")


def _params(sem):
    return pltpu.CompilerParams(dimension_semantics=sem, vmem_limit_bytes=VMEM_LIMIT)


def _sigmoid(x):
    return 0.5 * jnp.tanh(0.5 * x) + 0.5


def _softplus(x):
    return jnp.maximum(x, 0.0) + jnp.log(1.0 + jnp.exp(-jnp.abs(x)))


def _dot_nt(a, b):
    return lax.dot_general(a, b, (((1,), (1,)), ((), ())), preferred_element_type=F32)


def _rms_matmul_kernel(x_ref, g_ref, w_ref, *rest, has_extra):
    if has_extra:
        ws_ref, o_ref, os_ref, xn_ref = rest
    else:
        o_ref, xn_ref = rest

    @pl.when(pl.program_id(1) == 0)
    def _():
        x = x_ref[...]
        ms = jnp.mean(x * x, axis=-1, keepdims=True)
        xn = (x * lax.rsqrt(ms + RMS_EPS) * g_ref[...]).astype(BF16)
        xn_ref[...] = xn
        if has_extra:
            os_ref[...] = jnp.dot(xn, ws_ref[...], preferred_element_type=F32)

    o_ref[...] = jnp.dot(xn_ref[...], w_ref[...], preferred_element_type=F32).astype(o_ref.dtype)


def rms_matmul(x, gain, w, w_extra=None, name="rms_matmul"):
    M, K = x.shape
    N = w.shape[1]
    tm = _pick(M, (1024, 512, 256, 128, 64, 32, 16, 8))
    tn = _pick(N, (1024, 512, 256, 128))
    has_extra = w_extra is not None
    in_specs = [pl.BlockSpec((tm, K), lambda i, j: (i, 0)),
                pl.BlockSpec((1, K), lambda i, j: (0, 0)),
                pl.BlockSpec((K, tn), lambda i, j: (0, j))]
    out_specs = [pl.BlockSpec((tm, tn), lambda i, j: (i, j))]
    out_shape = [jax.ShapeDtypeStruct((M, N), F32)]
    args = [x, gain.reshape(1, K), w]
    if has_extra:
        ne = w_extra.shape[1]
        in_specs.append(pl.BlockSpec((K, ne), lambda i, j: (0, 0)))
        out_specs.append(pl.BlockSpec((tm, ne), lambda i, j: (i, 0)))
        out_shape.append(jax.ShapeDtypeStruct((M, ne), F32))
        args.append(w_extra)
    outs = pl.pallas_call(
        functools.partial(_rms_matmul_kernel, has_extra=has_extra),
        grid=(M // tm, N // tn),
        in_specs=in_specs, out_specs=out_specs, out_shape=out_shape,
        scratch_shapes=[pltpu.VMEM((tm, K), BF16)],
        compiler_params=_params(("parallel", "arbitrary")),
        name=name,
    )(*args)
    return outs if has_extra else outs[0]


def _matmul_res_kernel(*refs, n_a):
    a_refs, w_refs = refs[:n_a], refs[n_a:2 * n_a]
    res_ref, o_ref = refs[2 * n_a], refs[2 * n_a + 1]
    acc = res_ref[...]
    for a_ref, w_ref in zip(a_refs, w_refs):
        acc = acc + jnp.dot(a_ref[...], w_ref[...], preferred_element_type=F32)
    o_ref[...] = acc


def matmul_res(a_list, w, res, name="matmul_res"):
    M, N = res.shape
    ktot = sum(a.shape[1] for a in a_list)
    tm = _pick(M, (1024, 512, 256, 128, 64, 32, 16, 8))
    tn = _pick(N, (1024, 512, 256, 128) if ktot <= 4096 else (512, 256, 128))
    in_specs, off = [], 0
    for a in a_list:
        in_specs.append(pl.BlockSpec((tm, a.shape[1]), lambda i, j: (i, 0)))
    for a in a_list:
        k = a.shape[1]
        assert off % k == 0
        in_specs.append(pl.BlockSpec((k, tn), lambda i, j, _o=off // k: (_o, j)))
        off += k
    in_specs.append(pl.BlockSpec((tm, tn), lambda i, j: (i, j)))
    return pl.pallas_call(
        functools.partial(_matmul_res_kernel, n_a=len(a_list)),
        grid=(M // tm, N // tn),
        in_specs=in_specs,
        out_specs=pl.BlockSpec((tm, tn), lambda i, j: (i, j)),
        out_shape=jax.ShapeDtypeStruct((M, N), F32),
        compiler_params=_params(("parallel", "arbitrary")),
        name=name,
    )(*a_list, *([w] * len(a_list)), res)


def _qk_prep_kernel(q_ref, k_ref, v_ref, gq_ref, gk_ref, *rest, want_kt, direct, n_buf):
    rest = rest[n_buf:]
    qn_ref, kn32_ref, kn16_ref, v16_ref = rest[:4]
    rest = rest[4:]
    knt_ref = rest[0] if want_kt else None
    v32_ref = rest[-1] if direct else None
    gq = gq_ref[...]
    gk = gk_ref[...]
    groups = [slice(g * HD, (g + 1) * HD) for g in range(2 * HEADS)]
    q = [q_ref[:, sl] for sl in groups]
    k = [k_ref[:, sl] for sl in groups]
    q2 = [jnp.mean(x * x, axis=-1, keepdims=True) for x in q]
    k2 = [jnp.mean(x * x, axis=-1, keepdims=True) for x in k]
    for g, (sl, q_, k_, q2_, k2_) in enumerate(zip(groups, q, k, q2, k2)):
        qn = q_ * lax.rsqrt(q2_ + RMS_EPS) * gq
        kn = k_ * lax.rsqrt(k2_ + RMS_EPS) * gk
        qn_ref[:, sl] = (qn * Q_SCALE).astype(BF16)
        if direct:
            kn32_ref[0, 0, :, g // 2, g % 2, :] = kn
        else:
            kn32_ref[:, sl] = kn
        kn16_ref[:, sl] = kn.astype(BF16)
        if want_kt:
            knt_ref[sl, :] = kn.T.astype(BF16)
    v = v_ref[...]
    v16_ref[...] = v.astype(BF16)
    if direct:
        for h in range(HEADS):
            v32_ref[0, 0, :, h, :] = v[:, h * DA_DV:(h + 1) * DA_DV]


def qk_prep(z, gq, gk, B, T, want_kt=False, dst=None):
    M = z.shape[0]
    W = 2 * HEADS * HD
    tm = _pick(T, (512, 256, 128, 64, 32, 16, 8))
    tpb = T // tm
    col = lambda c: pl.BlockSpec((tm, W), lambda i, _c=c: (i, _c))
    vec = pl.BlockSpec((1, HD), lambda i: (0, 0))
    row = pl.BlockSpec((tm, W), lambda i: (i, 0))
    in_specs = [col(0), col(1), col(2), vec, vec]
    args = [z, z, z, gq.reshape(1, HD), gk.reshape(1, HD)]
    out_specs = [row, row, row, row]
    out_shape = [jax.ShapeDtypeStruct((M, W), BF16), jax.ShapeDtypeStruct((M, W), F32),
                 jax.ShapeDtypeStruct((M, W), BF16), jax.ShapeDtypeStruct((M, W), BF16)]
    if want_kt:
        out_specs.append(pl.BlockSpec((W, tm), lambda i: (0, i)))
        out_shape.append(jax.ShapeDtypeStruct((W, M), BF16))
    aliases = {}
    if dst is not None:
        kbuf, vbuf, depth, rows, layer, off = dst
        where = lambda i: (layer, i // tpb, off + (i % tpb) * tm)
        out_specs[1] = pl.BlockSpec(tuple(pl.Element(d) for d in (1, 1, tm, HEADS, 2, HD)),
                                    lambda i: where(i) + (0, 0, 0))
        out_shape[1] = jax.ShapeDtypeStruct((depth, B, rows, HEADS, 2, HD), F32)
        out_specs.append(pl.BlockSpec(tuple(pl.Element(d) for d in (1, 1, tm, HEADS, DA_DV)),
                                      lambda i: where(i) + (0, 0)))
        out_shape.append(jax.ShapeDtypeStruct((depth, B, rows, HEADS, DA_DV), F32))
        if kbuf is not None:
            in_specs += [pl.BlockSpec(memory_space=pl.ANY)] * 2
            args += [kbuf, vbuf]
            aliases = {len(args) - 2: 1, len(args) - 1: len(out_shape) - 1}
    return pl.pallas_call(
        functools.partial(_qk_prep_kernel, want_kt=want_kt, direct=dst is not None, n_buf=len(aliases)),
        grid=(M // tm,),
        in_specs=in_specs,
        out_specs=out_specs,
        out_shape=out_shape,
        input_output_aliases=aliases,
        compiler_params=_params(("parallel",)),
        name="qk_prep",
    )(*args)


def _softmax_update(mi, s, v, m_ref, l_ref, acc_ref):
    m_prev = m_ref[mi]
    m_new = jnp.maximum(m_prev, jnp.max(s, axis=-1, keepdims=True))
    alpha = jnp.exp2(m_prev - m_new)
    p = jnp.exp2(s - m_new)
    l_ref[mi] = alpha * l_ref[mi] + jnp.sum(p, axis=-1, keepdims=True)
    acc_ref[mi] = alpha * acc_ref[mi] + jnp.dot(p.astype(BF16), v, preferred_element_type=F32)
    m_ref[mi] = m_new


def _attn_init(m_ref, l_ref, acc_ref):
    m_ref[...] = jnp.full(m_ref.shape, -jnp.inf, F32)
    l_ref[...] = jnp.zeros(l_ref.shape, F32)
    acc_ref[...] = jnp.zeros(acc_ref.shape, F32)


def _attn_tile(q, k, v, bias, m_ref, l_ref, acc_ref):
    for mi in range(2):
        sl = slice(mi * HD, (mi + 1) * HD)
        s = _dot_nt(q[:, sl], k[:, sl]) + bias
        _softmax_update(mi, s, v, m_ref, l_ref, acc_ref)


def _attn_finish(lam, out_scale, g_ref, o_ref, l_ref, acc_ref):
    o = acc_ref[0] / l_ref[0] - lam * (acc_ref[1] / l_ref[1])
    on = o * lax.rsqrt(jnp.mean(o * o, axis=-1, keepdims=True) + RMS_EPS) * g_ref[...]
    o_ref[...] = (on * out_scale).astype(o_ref.dtype)


def _attn_scratch(tq):
    return [pltpu.VMEM((2, tq, 1), F32), pltpu.VMEM((2, tq, 1), F32), pltpu.VMEM((2, tq, DA_DV), F32)]


LANES = 128
ATT_TQ = (1024, 512, 256, 128, 64)
ATT_TK = (1024, 512, 256, 128, 64)
ATT_RB = 256


def _rowblock_tile(q_ref, kt_ref, v_ref, bias_ref, shift, m_ref, l_ref, acc_ref, rb, diagonal=False):
    tq = q_ref.shape[0]
    tk = kt_ref.shape[1]
    rb = min(rb, tq)
    nrb = tq // rb
    assert rb % CHUNK == 0 or not diagonal
    ncols = lambda r: min(tk, (r + 1) * rb) if diagonal else tk

    def scores(r):
        rows = pl.ds(r * rb, rb)
        out = [jnp.dot(q_ref[rows, mi * HD:(mi + 1) * HD], kt_ref[mi * HD:(mi + 1) * HD, 0:ncols(r)],
                       preferred_element_type=F32) for mi in range(2)]
        if bias_ref is not None:
            bias = bias_ref[0, 0, rows, 0:ncols(r)]
            out = [s + bias for s in out]
        return out

    def update(r, s_pair):
        rows = pl.ds(r * rb, rb)
        nb = ncols(r) // LANES
        v = v_ref[0:ncols(r), :]
        for mi in range(2):
            s = s_pair[mi]
            blocks = [s[:, c * LANES:(c + 1) * LANES] for c in range(nb)]
            smax = blocks[0]
            for blk in blocks[1:]:
                smax = jnp.maximum(smax, blk)
            m_prev = m_ref[mi, rows, :]
            if shift is not None:
                m_prev = m_prev - shift
            m_new = jnp.maximum(m_prev, jnp.max(smax, axis=-1, keepdims=True))
            alpha = jnp.exp2(m_prev - m_new)
            ps = [jnp.exp2(blk - m_new) for blk in blocks]
            psum = ps[0]
            for pb in ps[1:]:
                psum = psum + pb
            l_ref[mi, rows, :] = alpha * l_ref[mi, rows, :] + psum
            m_ref[mi, rows, :] = m_new if shift is None else m_new + shift
            p = jnp.concatenate(ps, axis=1).astype(BF16) if nb > 1 else ps[0].astype(BF16)
            pv = jnp.dot(p, v, preferred_element_type=F32)
            acc_ref[mi, rows, :] = jnp.concatenate([alpha, alpha], axis=1) * acc_ref[mi, rows, :] + pv

    s_next = scores(0)
    for r in range(nrb):
        s_cur = s_next
        if r + 1 < nrb:
            s_next = scores(r + 1)
        update(r, s_cur)


def _attn_big_kernel(qi_ref, kj_ref, lam_ref, far_ref, q_ref, k_ref, v_ref, km_ref, vm_ref, bias_ref, bmeta_ref,
                     g_ref, o_ref, m_ref, l_ref, acc_ref, *, out_scale, ratio, rb):
    step = pl.program_id(2)
    i = qi_ref[step]
    j = kj_ref[step]
    is_far = j - ratio * i < -1

    @pl.when(j == 0)
    def _():
        _attn_init(m_ref, l_ref, acc_ref)
        _rowblock_tile(q_ref, km_ref, vm_ref, bmeta_ref, None, m_ref, l_ref, acc_ref, rb)

    @pl.when(is_far)
    def _():
        _rowblock_tile(q_ref, k_ref, v_ref, None, far_ref[pl.program_id(1)], m_ref, l_ref, acc_ref, rb)

    if ratio == 1:
        @pl.when(j == i - 1)
        def _():
            _rowblock_tile(q_ref, k_ref, v_ref, bias_ref, None, m_ref, l_ref, acc_ref, rb)

        @pl.when(j == i)
        def _():
            _rowblock_tile(q_ref, k_ref, v_ref, bias_ref, None, m_ref, l_ref, acc_ref, rb, diagonal=True)
    else:
        @pl.when(jnp.logical_not(is_far))
        def _():
            _rowblock_tile(q_ref, k_ref, v_ref, bias_ref, None, m_ref, l_ref, acc_ref, rb)

    @pl.when(j == ratio * (i + 1) - 1)
    def _():
        l0 = jnp.sum(l_ref[0], axis=-1, keepdims=True)
        l1 = jnp.sum(l_ref[1], axis=-1, keepdims=True)
        o = acc_ref[0] / l0 - lam_ref[0, 0] * (acc_ref[1] / l1)
        on = o * lax.rsqrt(jnp.mean(o * o, axis=-1, keepdims=True) + RMS_EPS) * g_ref[...]
        o_ref[...] = (on * out_scale).astype(o_ref.dtype)


def attn_big(qn, knt16, v16, kmeta_t, vmeta, bias, far, bmeta, lam, gain, out_scale, B, T, tq, tk, rb):
    nq, nk = T // tq, T // tk
    ratio = tq // tk
    mp = vmeta.shape[0] // B
    pairs = [(i, j) for i in range(nq) for j in range(ratio * (i + 1))]
    qi = jnp.asarray([p[0] for p in pairs], jnp.int32)
    kj = jnp.asarray([p[1] for p in pairs], jnp.int32)
    grid_spec = pltpu.PrefetchScalarGridSpec(
        num_scalar_prefetch=2,
        grid=(B, HEADS, len(pairs)),
        in_specs=[
            pl.BlockSpec(memory_space=pltpu.SMEM),
            pl.BlockSpec(memory_space=pltpu.SMEM),
            pl.BlockSpec((tq, DA_DV), lambda b, h, s, qi, kj: (b * nq + qi[s], h)),
            pl.BlockSpec((DA_DV, tk), lambda b, h, s, qi, kj: (h, b * nk + kj[s])),
            pl.BlockSpec((tk, DA_DV), lambda b, h, s, qi, kj: (b * nk + kj[s], h)),
            pl.BlockSpec((DA_DV, mp), lambda b, h, s, qi, kj: (b * HEADS + h, 0)),
            pl.BlockSpec((mp, DA_DV), lambda b, h, s, qi, kj: (b, h)),
            pl.BlockSpec((1, 1, tq, tk),
                         lambda b, h, s, qi, kj: (h, jnp.maximum(kj[s] - ratio * qi[s] + 1, 0), 0, 0)),
            pl.BlockSpec((1, 1, tq, mp), lambda b, h, s, qi, kj: (h, jnp.minimum(qi[s], 1), 0, 0)),
            pl.BlockSpec((1, DA_DV), lambda b, h, s, qi, kj: (0, 0)),
        ],
        out_specs=pl.BlockSpec((tq, DA_DV), lambda b, h, s, qi, kj: (b * nq + qi[s], h)),
        scratch_shapes=[pltpu.VMEM((2, tq, LANES), F32), pltpu.VMEM((2, tq, LANES), F32),
                        pltpu.VMEM((2, tq, DA_DV), F32)],
    )
    return pl.pallas_call(
        functools.partial(_attn_big_kernel, out_scale=out_scale, ratio=ratio, rb=rb),
        grid_spec=grid_spec,
        out_shape=jax.ShapeDtypeStruct((B * T, HEADS * DA_DV), BF16),
        compiler_params=_params(("parallel", "parallel", "arbitrary")),
        name="attn_prompt",
    )(qi, kj, lam, far, qn, knt16, v16, kmeta_t, vmeta, bias, bmeta, gain.reshape(1, DA_DV))


def _attn_small_kernel(lam_ref, q_ref, *rest, nc, out_scale):
    if nc > 0:
        ck_ref, cv_ref, bc_ref, k_ref, v_ref, bn_ref, g_ref, o_ref, m_ref, l_ref, acc_ref = rest
    else:
        k_ref, v_ref, bn_ref, g_ref, o_ref, m_ref, l_ref, acc_ref = rest
    j = pl.program_id(2)
    q = q_ref[...]

    @pl.when(j == 0)
    def _():
        _attn_init(m_ref, l_ref, acc_ref)

    if nc > 0:
        @pl.when(j < nc)
        def _():
            _attn_tile(q, ck_ref[...].astype(BF16), cv_ref[...].astype(BF16), bc_ref[0, 0], m_ref, l_ref, acc_ref)

    @pl.when(j == nc)
    def _():
        _attn_tile(q, k_ref[...], v_ref[...], bn_ref[0], m_ref, l_ref, acc_ref)
        _attn_finish(lam_ref[0, 0], out_scale, g_ref, o_ref, l_ref, acc_ref)


def attn_small(qn, kn16, v16, cache, bias_new, lam, gain, out_scale, B, L):
    W = HEADS * DA_DV
    blk = pl.BlockSpec((L, DA_DV), lambda b, h, j: (b, h))
    in_specs = [pl.BlockSpec(memory_space=pltpu.SMEM), blk]
    args = [lam, qn]
    nc = 0
    if cache is not None:
        ck, cv, bc, tk = cache
        nc = ck.shape[0] // B // tk
        cspec = pl.BlockSpec((tk, DA_DV), lambda b, h, j: (b * nc + jnp.minimum(j, nc - 1), h))
        in_specs += [cspec, cspec,
                     pl.BlockSpec((1, 1, L, tk), lambda b, h, j: (h, jnp.where(j >= nc - 1, 1, 0), 0, 0))]
        args += [ck, cv, bc]
    in_specs += [blk, blk, pl.BlockSpec((1, L, L), lambda b, h, j: (h, 0, 0)),
                 pl.BlockSpec((1, DA_DV), lambda b, h, j: (0, 0))]
    args += [kn16, v16, bias_new, gain.reshape(1, DA_DV)]
    return pl.pallas_call(
        functools.partial(_attn_small_kernel, nc=nc, out_scale=out_scale),
        grid=(B, HEADS, nc + 1),
        in_specs=in_specs,
        out_specs=blk,
        out_shape=jax.ShapeDtypeStruct((B * L, W), BF16),
        scratch_shapes=_attn_scratch(L),
        compiler_params=_params(("parallel", "parallel", "arbitrary")),
        name="attn_block",
    )(*args)


_NN = (((1,), (0,)), ((), ()))
_NT = (((1,), (1,)), ((), ()))
_TN = (((0,), (0,)), ((), ()))


def _dg(a, b, dn):
    return lax.dot_general(a, b, dn, preferred_element_type=F32)


def _bf16r(a):
    return a.astype(BF16).astype(F32)


def _mm3(a, b, dn=_NN):
    ah = _bf16r(a)
    al = a - ah
    bh = _bf16r(b)
    bl = b - bh
    return _dg(ah, bh, dn) + (_dg(ah, bl, dn) + _dg(al, bh, dn))


def _split3(a):
    hi = _bf16r(a)
    r1 = a - hi
    mid = _bf16r(r1)
    return hi, mid, r1 - mid


def _chunk_masks(L):
    row = lax.broadcasted_iota(jnp.int32, (L, L), 0)
    col = lax.broadcasted_iota(jnp.int32, (L, L), 1)
    return row, col


def _cumsum_cols_rows(x, tril, triu):
    hi, mid, lo = _split3(x)
    cols = _dg(tril, hi, _NN) + (_dg(tril, mid, _NN) + _dg(tril, lo, _NN))
    rows = _dg(hi, triu, _TN) + (_dg(mid, triu, _TN) + _dg(lo, triu, _TN))
    return cols, rows


def _gdn_prep_kernel(x_ref, halo_ref, prev_ref, sc_ref, cw_ref, nega_ref, dtb_ref,
                     w_ref, u_ref, qg_ref, kd_ref, qk_ref, gc_ref, xbuf, *, L, G):
    i = pl.program_id(1)
    tile = L * G
    xbuf[0:HALO, :] = jnp.where(i == 0, prev_ref[0], halo_ref[...])
    xbuf[HALO:HALO + tile, :] = x_ref[...]
    cw = cw_ref[...]
    y = xbuf[HALO - 3:HALO - 3 + tile, :] * cw[0:1, :]
    for t in range(1, GDN_CONV):
        y = y + xbuf[HALO - 3 + t:HALO - 3 + t + tile, :] * cw[t:t + 1, :]
    gcv = y * _sigmoid(y)

    row, col = _chunk_masks(L)
    incl = col <= row
    strict = col < row
    eye = (col == row).astype(F32)
    tril = incl.astype(F32)
    triu = (row <= col).astype(F32)
    sc = sc_ref[...]
    beta_all = _sigmoid(sc)
    g_all = nega_ref[...] * _softplus(sc + dtb_ref[...])
    nqk = HEADS * HD

    ids = [(c, h) for c in range(G) for h in range(HEADS)]
    cum = [_cumsum_cols_rows(g_all[c * L:(c + 1) * L], tril, triu) for c in range(G)]
    for c in range(G):
        gc_ref[c * L:(c + 1) * L, :] = cum[c][0]
    xq = [gcv[c * L:(c + 1) * L, h * HD:(h + 1) * HD] for c, h in ids]
    xk = [gcv[c * L:(c + 1) * L, nqk + h * HD:nqk + (h + 1) * HD] for c, h in ids]
    xv = [gcv[c * L:(c + 1) * L, 2 * nqk + h * HD:2 * nqk + (h + 1) * HD] for c, h in ids]
    q2 = [jnp.sum(x * x, axis=-1, keepdims=True) for x in xq]
    k2 = [jnp.sum(x * x, axis=-1, keepdims=True) for x in xk]
    qn = [x * lax.rsqrt(s2 + RMS_EPS) * (HD ** -0.5) for x, s2 in zip(xq, q2)]
    kn = [x * lax.rsqrt(s2 + RMS_EPS) for x, s2 in zip(xk, k2)]
    beta = [beta_all[c * L:(c + 1) * L, h:h + 1] for c, h in ids]
    gcol = [cum[c][0][:, HEADS + h:HEADS + h + 1] for c, h in ids]
    grow = [cum[c][1][HEADS + h:HEADS + h + 1, :] for c, h in ids]
    dec = [jnp.where(incl, jnp.exp(jnp.where(incl, a - b, 0.0)), 0.0) for a, b in zip(gcol, grow)]
    eg = [jnp.exp(a) for a in gcol]
    st = [dict(rows=slice(c * L, (c + 1) * L), h=h, qn=qn[t], kn=kn[t], bv=beta[t] * xv[t],
               bek=beta[t] * eg[t] * kn[t], beta=beta[t], dec=dec[t], eg=eg[t],
               kd=kn[t] * jnp.exp(gcol[t][L - 1:L, :] - gcol[t])) for t, (c, h) in enumerate(ids)]

    kk = [_mm3(s["kn"], s["kn"], _NT) for s in st]
    nmat = [jnp.where(strict, s["beta"] * k_ * s["dec"], 0.0) for s, k_ in zip(st, kk)]
    tinv = [eye - n for n in nmat]
    pw = [_mm3(n, n) for n in nmat]
    p = 2
    while True:
        tinv = [t + _mm3(t, q) for t, q in zip(tinv, pw)]
        p *= 2
        if p >= L:
            break
        pw = [_mm3(q, q) for q in pw]
    w_mat = [_mm3(t, s["bek"]) for t, s in zip(tinv, st)]
    u_mat = [_mm3(t, s["bv"]) for t, s in zip(tinv, st)]
    qk = [_mm3(s["qn"], s["kn"], _NT) * s["dec"] for s in st]
    for s, w_, u_, qk_ in zip(st, w_mat, u_mat, qk):
        rows, h = s["rows"], s["h"]
        sl = slice(h * HD, (h + 1) * HD)
        w_ref[rows, sl] = w_
        u_ref[rows, sl] = u_
        qg_ref[rows, sl] = s["eg"] * s["qn"]
        kd_ref[rows, sl] = s["kd"]
        qk_ref[rows, h * L:(h + 1) * L] = qk_


def _gdn_scan_kernel(w_ref, u_ref, qg_ref, kd_ref, qk_ref, gc_ref, gate_ref, s0_ref, gn_ref,
                     o_ref, sout_ref, s_scr, *, L, cps, B):
    i = pl.program_id(0)

    @pl.when(i == 0)
    def _():
        s_scr[...] = s0_ref[...]

    chains = [(b, h) for b in range(B) for h in range(HEADS)]
    gn = gn_ref[...]
    for c in range(cps):
        rows = slice(c * L, (c + 1) * L)
        last = (c + 1) * L - 1
        s_old = [s_scr[b, h] for b, h in chains]
        ws = [_mm3(w_ref[b, rows, h * HD:(h + 1) * HD], s_) for (b, h), s_ in zip(chains, s_old)]
        u = [u_ref[b, rows, h * HD:(h + 1) * HD] - w_ for (b, h), w_ in zip(chains, ws)]
        upd = [_mm3(kd_ref[b, rows, h * HD:(h + 1) * HD], u_, _TN) for (b, h), u_ in zip(chains, u)]
        ds = [jnp.exp(gc_ref[b, last:last + 1, HEADS + h:HEADS + h + 1]) for b, h in chains]
        for (b, h), s_, d_, ds_ in zip(chains, s_old, upd, ds):
            s_scr[b, h] = ds_ * s_ + d_
        o1 = [_mm3(qg_ref[b, rows, h * HD:(h + 1) * HD], s_) for (b, h), s_ in zip(chains, s_old)]
        o2 = [_mm3(qk_ref[b, rows, h * L:(h + 1) * L], u_) for (b, h), u_ in zip(chains, u)]
        o = [a_ + b_ for a_, b_ in zip(o1, o2)]
        ms = [jnp.mean(o_ * o_, axis=-1, keepdims=True) for o_ in o]
        for (b, h), o_, ms_ in zip(chains, o, ms):
            sl = slice(h * HD, (h + 1) * HD)
            on = o_ * lax.rsqrt(ms_ + RMS_EPS) * gn
            gt = gate_ref[b, rows, sl]
            o_ref[b, rows, sl] = (on * (gt * _sigmoid(gt))).astype(o_ref.dtype)

    @pl.when(i == pl.num_programs(0) - 1)
    def _():
        sout_ref[...] = s_scr[...]


def gdn(z, sc, prev8, conv_w, neg_a, dt_bias, s0, gn, B, T, L):
    M = B * T
    nchunks = T // L
    G = _pick(nchunks, (2, 1))
    tile = L * G
    nt = T // tile
    hb = tile // HALO
    nega_vec = jnp.zeros((1, SCAL_COLS), F32).at[0, HEADS:2 * HEADS].set(neg_a)
    dtb_vec = jnp.zeros((1, SCAL_COLS), F32).at[0, HEADS:2 * HEADS].set(dt_bias.astype(F32))
    rowblk = lambda w: pl.BlockSpec((tile, w), lambda b, i: (b * nt + i, 0))
    vec = pl.BlockSpec((1, SCAL_COLS), lambda b, i: (0, 0))
    w_mat, u_mat, qg, kd, qk, gc = pl.pallas_call(
        functools.partial(_gdn_prep_kernel, L=L, G=G),
        grid=(B, nt),
        in_specs=[pl.BlockSpec((tile, QKV_COLS), lambda b, i: (b * nt + i, 2)),
                  pl.BlockSpec((HALO, QKV_COLS), lambda b, i: (jnp.maximum((b * nt + i) * hb - 1, 0), 2)),
                  pl.BlockSpec((1, HALO, QKV_COLS), lambda b, i: (b, 0, 0)),
                  rowblk(SCAL_COLS),
                  pl.BlockSpec((GDN_CONV, QKV_COLS), lambda b, i: (0, 0)),
                  vec, vec],
        out_specs=[rowblk(HEADS * HD)] * 4 + [rowblk(HEADS * L), rowblk(SCAL_COLS)],
        out_shape=[jax.ShapeDtypeStruct((M, HEADS * HD), F32)] * 4
                  + [jax.ShapeDtypeStruct((M, HEADS * L), F32), jax.ShapeDtypeStruct((M, SCAL_COLS), F32)],
        scratch_shapes=[pltpu.VMEM((tile + HALO, QKV_COLS), F32)],
        compiler_params=_params(("parallel", "parallel")),
        name="gdn_prep",
    )(z, z, prev8, sc, conv_w, nega_vec, dtb_vec)

    cps = _pick(nchunks, (4, 2, 1))
    rows = cps * L
    r3 = lambda a: a.reshape(B, T, a.shape[-1])
    blk = lambda w, cb=0: pl.BlockSpec((B, rows, w), lambda i, _c=cb: (0, i, _c))
    st = pl.BlockSpec((B, HEADS, HD, HD), lambda i: (0, 0, 0, 0))
    o, s_new = pl.pallas_call(
        functools.partial(_gdn_scan_kernel, L=L, cps=cps, B=B),
        grid=(nchunks // cps,),
        in_specs=[blk(HEADS * HD)] * 4 + [blk(HEADS * L), blk(SCAL_COLS), blk(HEADS * HD, 12), st,
                                          pl.BlockSpec((1, HD), lambda i: (0, 0))],
        out_specs=[blk(HEADS * HD), st],
        out_shape=[jax.ShapeDtypeStruct((B, T, HEADS * HD), BF16), jax.ShapeDtypeStruct((B, HEADS, HD, HD), F32)],
        scratch_shapes=[pltpu.VMEM((B, HEADS, HD, HD), F32)],
        compiler_params=_params(("arbitrary",)),
        name="gdn_scan",
    )(r3(w_mat), r3(u_mat), r3(qg), r3(kd), r3(qk), r3(gc), r3(z), s0, gn.reshape(1, HD))
    return o.reshape(M, HEADS * HD), s_new


def _mlstm_prep_kernel(x_ref, sc_ref, ib_ref, fb_ref, pv_ref, u_ref, n_ref, st_ref, *, L, G):
    row, col = _chunk_masks(L)
    incl = col <= row
    eye = (col == row).astype(F32)
    tril = incl.astype(F32)
    triu = (row <= col).astype(F32)
    sc = sc_ref[...]
    ig_all = sc + ib_ref[...]
    lf_all = -_softplus(-(sc + fb_ref[...]))
    lane = lax.broadcasted_iota(jnp.int32, (L, SCAL_COLS), 1)
    nqk = HEADS * HD
    ids = [(c, h) for c in range(G) for h in range(HEADS)]
    crow = lambda c: slice(c * L, (c + 1) * L)
    cum = [_cumsum_cols_rows(lf_all[crow(c)], tril, triu) for c in range(G)]
    ig3 = [_split3(ig_all[crow(c)]) for c in range(G)]
    ig_rows = [_dg(a, eye, _TN) + (_dg(b, eye, _TN) + _dg(c_, eye, _TN)) for a, b, c_ in ig3]
    bcol = [cum[c][0][:, 3 * HEADS + h:3 * HEADS + h + 1] for c, h in ids]
    brow = [cum[c][1][3 * HEADS + h:3 * HEADS + h + 1, :] for c, h in ids]
    igrow = [ig_rows[c][2 * HEADS + h:2 * HEADS + h + 1, :] for c, h in ids]
    igcol = [ig_all[crow(c), 2 * HEADS + h:2 * HEADS + h + 1] for c, h in ids]
    dmat = [jnp.where(incl, a - b + g_, NEG_INF) for a, b, g_ in zip(bcol, brow, igrow)]
    dmax = [jnp.max(d_, axis=-1, keepdims=True) for d_ in dmat]
    q = [x_ref[crow(c), h * HD:(h + 1) * HD] for c, h in ids]
    k = [x_ref[crow(c), nqk + h * HD:nqk + (h + 1) * HD] * (HD ** -0.5) for c, h in ids]
    v = [x_ref[crow(c), 2 * nqk + h * HD:2 * nqk + (h + 1) * HD] for c, h in ids]
    kw = [jnp.exp(b_[L - 1:L, :] - b_ + g_ - m_[L - 1:L, :]) * k_ for b_, g_, m_, k_ in zip(bcol, igcol, dmax, k)]
    w = [jnp.exp(d_ - m_) for d_, m_ in zip(dmat, dmax)]
    qk = [_mm3(q_, k_, _NT) * w_ for q_, k_, w_ in zip(q, k, w)]
    pv = [_mm3(a_, v_) for a_, v_ in zip(qk, v)]
    um = [_mm3(kw_, v_, _TN) for kw_, v_ in zip(kw, v)]
    rs = [jnp.sum(a_, axis=-1, keepdims=True) for a_ in qk]
    ns = [jnp.sum(kw_, axis=0, keepdims=True) for kw_ in kw]
    for t, (c, h) in enumerate(ids):
        sl = slice(h * HD, (h + 1) * HD)
        pv_ref[crow(c), sl] = pv[t]
        u_ref[c * HD:(c + 1) * HD, sl] = um[t]
        n_ref[c * 8:(c + 1) * 8, sl] = jnp.broadcast_to(ns[t], (8, HD))
    for c in range(G):
        stats = jnp.where(lane >= 3 * HEADS, cum[c][0], 0.0)
        for h in range(HEADS):
            stats = jnp.where(lane == h, dmax[c * HEADS + h], stats)
            stats = jnp.where(lane == HEADS + h, rs[c * HEADS + h], stats)
        st_ref[crow(c), :] = stats


def _mlstm_scan_kernel(x_ref, og_ref, pv_ref, u_ref, n_ref, st_ref, c0_ref, n0_ref, m0_ref, gn_ref,
                       o_ref, cout_ref, nout_ref, mout_ref, c_scr, n_scr, m_scr, *, L, cps, B):
    i = pl.program_id(0)

    @pl.when(i == 0)
    def _():
        c_scr[...] = c0_ref[...]
        n_scr[...] = n0_ref[...]
        m_scr[...] = m0_ref[...]

    chains = [(b, h) for b in range(B) for h in range(HEADS)]
    gn = gn_ref[...]
    for c in range(cps):
        rows = slice(c * L, (c + 1) * L)
        n_ch = range(len(chains))
        hs = [slice(h * HD, (h + 1) * HD) for _, h in chains]
        dmax = [st_ref[b, rows, h:h + 1] for b, h in chains]
        rsum = [st_ref[b, rows, HEADS + h:HEADS + h + 1] for b, h in chains]
        bcol = [st_ref[b, rows, 3 * HEADS + h:3 * HEADS + h + 1] for b, h in chains]
        m_prev = [m_scr[b, h, 0:1, 0:1] for b, h in chains]
        inter = [bcol[t] + m_prev[t] for t in n_ch]
        m_t = [jnp.maximum(inter[t], dmax[t]) for t in n_ch]
        m_new = [m_t[t][L - 1:L, :] for t in n_ch]
        f_new = [jnp.exp(dmax[t][L - 1:L, :] - m_new[t]) for t in n_ch]
        dec0 = [jnp.exp(bcol[t][L - 1:L, :] + m_prev[t] - m_new[t]) for t in n_ch]
        c_old = [c_scr[b, h] for b, h in chains]
        n_old = [n_scr[b, h] for b, h in chains]
        q = [x_ref[b, rows, hs[t]] for t, (b, _) in enumerate(chains)]
        for t, (b, h) in enumerate(chains):
            c_scr[b, h] = dec0[t] * c_old[t] + f_new[t] * u_ref[b, c * HD:(c + 1) * HD, hs[t]]
            n_scr[b, h] = dec0[t] * n_old[t] + f_new[t] * n_ref[b, c * 8:(c + 1) * 8, hs[t]]
            m_scr[b, h] = jnp.broadcast_to(m_new[t], (8, HD))
        qc = [_mm3(q[t], c_old[t]) for t in n_ch]
        w_inter = [jnp.exp(inter[t] - m_t[t]) for t in n_ch]
        e1 = [jnp.exp(dmax[t] - m_t[t]) for t in n_ch]
        qn = [jnp.sum(q[t] * n_old[t][0:1, :], axis=-1, keepdims=True) for t in n_ch]
        den = [w_inter[t] * qn[t] + e1[t] * rsum[t] for t in n_ch]
        scale = [1.0 / jnp.maximum(jnp.abs(den[t]), jnp.exp(-m_t[t])) for t in n_ch]
        hout = [(w_inter[t] * qc[t] + e1[t] * pv_ref[b, rows, hs[t]]) * scale[t] for t, (b, _) in enumerate(chains)]
        ms = [jnp.mean(hout[t] * hout[t], axis=-1, keepdims=True) for t in n_ch]
        for t, (b, h) in enumerate(chains):
            on = hout[t] * lax.rsqrt(ms[t] + RMS_EPS) * gn
            o_ref[b, rows, hs[t]] = (on * _sigmoid(og_ref[b, rows, hs[t]])).astype(o_ref.dtype)

    @pl.when(i == pl.num_programs(0) - 1)
    def _():
        cout_ref[...] = c_scr[...]
        nout_ref[...] = n_scr[...]
        mout_ref[...] = m_scr[...]


def mlstm(z, sc, i_bias, f_bias, c0, n0p, m0p, gn, B, T, L):
    M = B * T
    nchunks = T // L
    G = _pick(nchunks, (4, 2, 1))
    tile = L * G
    nt = T // tile
    ib_vec = jnp.zeros((1, SCAL_COLS), F32).at[0, 2 * HEADS:3 * HEADS].set(i_bias.astype(F32))
    fb_vec = jnp.zeros((1, SCAL_COLS), F32).at[0, 3 * HEADS:4 * HEADS].set(f_bias.astype(F32))
    rowblk = lambda r, w, cb=0: pl.BlockSpec((r, w), lambda b, i, _c=cb: (b * nt + i, _c))
    vec = pl.BlockSpec((1, SCAL_COLS), lambda b, i: (0, 0))
    nc_all = B * nchunks
    pv, um, nm, stats = pl.pallas_call(
        functools.partial(_mlstm_prep_kernel, L=L, G=G),
        grid=(B, nt),
        in_specs=[rowblk(tile, QKV_COLS, 3), rowblk(tile, SCAL_COLS), vec, vec],
        out_specs=[rowblk(tile, HEADS * HD), rowblk(G * HD, HEADS * HD), rowblk(G * 8, HEADS * HD),
                   rowblk(tile, SCAL_COLS)],
        out_shape=[jax.ShapeDtypeStruct((M, HEADS * HD), F32), jax.ShapeDtypeStruct((nc_all * HD, HEADS * HD), F32),
                   jax.ShapeDtypeStruct((nc_all * 8, HEADS * HD), F32), jax.ShapeDtypeStruct((M, SCAL_COLS), F32)],
        compiler_params=_params(("parallel", "parallel")),
        name="mlstm_prep",
    )(z, sc, ib_vec, fb_vec)

    cps = _pick(nchunks, (4, 2, 1))
    r3 = lambda a: a.reshape(B, a.shape[0] // B, a.shape[-1])
    blk = lambda r, w, cb=0: pl.BlockSpec((B, r, w), lambda i, _c=cb: (0, i, _c))
    st = pl.BlockSpec((B, HEADS, HD, HD), lambda i: (0, 0, 0, 0))
    vec8 = pl.BlockSpec((B, HEADS, 8, HD), lambda i: (0, 0, 0, 0))
    o, c_new, n_new, m_new = pl.pallas_call(
        functools.partial(_mlstm_scan_kernel, L=L, cps=cps, B=B),
        grid=(nchunks // cps,),
        in_specs=[blk(cps * L, HEADS * HD, 3 * QKV_COLS // (HEADS * HD)), blk(cps * L, HEADS * HD, 13),
                  blk(cps * L, HEADS * HD), blk(cps * HD, HEADS * HD), blk(cps * 8, HEADS * HD),
                  blk(cps * L, SCAL_COLS), st, vec8, vec8, pl.BlockSpec((1, HD), lambda i: (0, 0))],
        out_specs=[blk(cps * L, HEADS * HD), st, vec8, vec8],
        out_shape=[jax.ShapeDtypeStruct((B, T, HEADS * HD), BF16), jax.ShapeDtypeStruct((B, HEADS, HD, HD), F32),
                   jax.ShapeDtypeStruct((B, HEADS, 8, HD), F32), jax.ShapeDtypeStruct((B, HEADS, 8, HD), F32)],
        scratch_shapes=[pltpu.VMEM((B, HEADS, HD, HD), F32), pltpu.VMEM((B, HEADS, 8, HD), F32),
                        pltpu.VMEM((B, HEADS, 8, HD), F32)],
        compiler_params=_params(("arbitrary",)),
        name="mlstm_scan",
    )(r3(z), r3(z), r3(pv), r3(um), r3(nm), r3(stats), c0, n0p, m0p, gn.reshape(1, HD))
    return o.reshape(M, HEADS * HD), c_new, n_new, m_new


def _ffn_up_act_kernel(x_ref, xh_ref, g_ref, wg_ref, wv_ref, prev_ref, cw_ref, o_ref, tail_ref,
                       xn_ref, xhn_ref, buf, *, tm, tpb):
    i = pl.program_id(0)

    def norm(x):
        ms = jnp.mean(x * x, axis=-1, keepdims=True)
        return (x * lax.rsqrt(ms + RMS_EPS) * g_ref[...]).astype(BF16)

    @pl.when(pl.program_id(1) == 0)
    def _():
        xn_ref[...] = norm(x_ref[...])
        xh = norm(xh_ref[...])
        xhn_ref[...] = jnp.concatenate([xh, xh], axis=0)

    wg = wg_ref[...]
    gate = jnp.dot(xn_ref[...], wg, preferred_element_type=F32)
    val = jnp.dot(xn_ref[...], wv_ref[...], preferred_element_type=F32)
    halo = jnp.dot(xhn_ref[...], wg, preferred_element_type=F32)[0:HALO, :]
    buf[0:HALO, :] = jnp.where(i % tpb == 0, prev_ref[0], halo)
    buf[HALO:HALO + tm, :] = gate
    cwh = 0.5 * cw_ref[...]
    g = buf[...]
    yh = g[HALO:, :] * cwh[FFN_CONV - 1:FFN_CONV, :]
    for t in range(1, FFN_CONV):
        yh = yh + pltpu.roll(g, t, axis=0)[HALO:, :] * cwh[FFN_CONV - 1 - t:FFN_CONV - t, :]
    o_ref[...] = (yh * (jnp.tanh(yh) + 1.0) * val).astype(o_ref.dtype)
    tail_ref[0] = buf[tm:tm + HALO, :]


def ffn_up_act(x, gain, w_up, prev8, conv_w, B, T):
    M, K = x.shape
    Fd = conv_w.shape[1]
    tm = _pick(T, (1024, 512, 256, 128, 64, 32, 16, 8))
    tn = _pick(Fd, (512, 256, 128))
    tpb, nf = T // tm, Fd // tn
    hb = tm // HALO
    hmid, tails = pl.pallas_call(
        functools.partial(_ffn_up_act_kernel, tm=tm, tpb=tpb),
        grid=(M // tm, nf),
        in_specs=[pl.BlockSpec((tm, K), lambda i, j: (i, 0)),
                  pl.BlockSpec((HALO, K), lambda i, j: (jnp.maximum(i * hb - 1, 0), 0)),
                  pl.BlockSpec((1, K), lambda i, j: (0, 0)),
                  pl.BlockSpec((K, tn), lambda i, j: (0, j)),
                  pl.BlockSpec((K, tn), lambda i, j: (0, nf + j)),
                  pl.BlockSpec((1, HALO, tn), lambda i, j: (i // tpb, 0, j)),
                  pl.BlockSpec((FFN_CONV, tn), lambda i, j: (0, j))],
        out_specs=[pl.BlockSpec((tm, tn), lambda i, j: (i, j)),
                   pl.BlockSpec((1, HALO, tn), lambda i, j: (i, 0, j))],
        out_shape=[jax.ShapeDtypeStruct((M, Fd), BF16), jax.ShapeDtypeStruct((M // tm, HALO, Fd), F32)],
        scratch_shapes=[pltpu.VMEM((tm, K), BF16), pltpu.VMEM((2 * HALO, K), BF16),
                        pltpu.VMEM((tm + HALO, tn), F32)],
        compiler_params=_params(("parallel", "arbitrary")),
        name="ffn_up_act",
    )(x, x, gain.reshape(1, K), w_up, w_up, prev8, conv_w)
    return hmid, tails[tpb - 1::tpb]


def _t5_bucket(rel):
    half = NUM_BUCKETS // 2
    exact = half // 2
    n = jnp.abs(rel)
    nf = jnp.maximum(n, 1).astype(F32)
    large = exact + (jnp.log(nf / exact) / math.log(MAX_DISTANCE / exact) * (half - exact)).astype(jnp.int32)
    large = jnp.minimum(large, half - 1)
    return jnp.where(rel > 0, half, 0) + jnp.where(n < exact, n, large)


def _rel_bias_vec(rel_bias, rel):
    bucket = _t5_bucket(rel)
    rb = rel_bias.astype(F32) * LOG2E
    H = rb.shape[1]
    out = jnp.zeros((H,) + rel.shape, F32)
    for kb in range(NUM_BUCKETS):
        out = jnp.where((bucket == kb)[None], rb[kb].reshape((H,) + (1,) * rel.ndim), out)
    return out


def _rel_bias_const(rel_bias, rel, R, C):
    v = _rel_bias_vec(rel_bias, jnp.full((1,), rel, jnp.int32))
    return jnp.broadcast_to(v[:, :, None], (v.shape[0], R, C))


def _rel_bias_toeplitz(rel_bias, R, C, off):
    ar = lambda n: jnp.arange(n, dtype=jnp.int32)
    if R * C <= 256 * 1024 or R % LANES or C % LANES:
        return _rel_bias_vec(rel_bias, ar(C)[None, :] - ar(R)[:, None] + off)
    nr, nc = R // LANES, C // LANES
    deltas = jnp.arange(-(nr - 1), nc, dtype=jnp.int32)
    rel = LANES * deltas[:, None, None] + (ar(LANES)[None, None, :] - ar(LANES)[None, :, None]) + off
    small = _rel_bias_vec(rel_bias, rel)
    rows = [jnp.concatenate([small[:, e - a + nr - 1] for e in range(nc)], axis=-1) for a in range(nr)]
    return jnp.concatenate(rows, axis=-2)


def _pad_rows(a, rows):
    return jnp.pad(a, ((0, 0), (rows - a.shape[1], 0), (0, 0)))


def _layer(x, B, T, L, lw, init, attn_fn, want_kt=False, dst=None):
    z, sc = rms_matmul(x, lw["norm_mix"], lw["w_in"], lw["w_in_scal"], name="in_proj")
    qn, kn32, kn16, v16, *more = qk_prep(z, lw["da_q_norm"], lw["da_k_norm"], B, T, want_kt, dst)
    knt = more[:1] if want_kt else []
    o_da = attn_fn(qn, knt[0] if want_kt else kn16, v16)
    o_g, s_new = gdn(z, sc, _pad_rows(init["gconv"], HALO), lw["gdn_conv_w"], lw["gdn_neg_a"], lw["gdn_dt_bias"],
                     init["S"], lw["gdn_out_norm"], B, T, L)
    n0p = jnp.broadcast_to(init["n"][:, :, None, :], (B, HEADS, 8, HD))
    m0p = jnp.broadcast_to(init["m"][:, :, None, None], (B, HEADS, 8, HD))
    o_m, c_new, n_new, m_new = mlstm(z, sc, lw["ml_i_bias"], lw["ml_f_bias"], init["C"], n0p, m0p,
                                     lw["ml_out_norm"], B, T, L)
    x1 = matmul_res([o_da, o_g, o_m], lw["w_out"], x, name="out_proj")
    hmid, gate_tail = ffn_up_act(x1, lw["norm_ffn"], lw["w_up"], _pad_rows(init["fconv"], HALO), lw["ffn_conv_w"], B, T)
    x2 = matmul_res([hmid], lw["w_down"], x1, name="ffn_down")
    z3 = z.reshape(B, T, Z_COLS)
    st = {
        "k": kn32 if dst is not None else kn32.reshape(B, T, HEADS, 2, HD),
        "v": more[-1] if dst is not None else z3[:, :, 2 * HEADS * DA_DV:3 * HEADS * DA_DV].reshape(B, T, HEADS, DA_DV),
        "gconv": z3[:, T - (GDN_CONV - 1):, 3 * HEADS * DA_DV:3 * HEADS * DA_DV + QKV_COLS],
        "S": s_new, "C": c_new, "n": n_new[:, :, 0, :], "m": m_new[:, :, 0, 0],
        "fconv": gate_tail[:, HALO - (FFN_CONV - 1):, :],
        "k16": kn16, "v16": v16,
    }
    return x2, st


def kernel(x_prompt, x_sample, cache_attn_k, cache_attn_v, state_gdn_conv, state_gdn_S, state_mlstm_C, state_mlstm_n, state_mlstm_m, state_ffn_conv, meta_tokens, rel_bias, norm_mix, norm_ffn, w_in, w_out, da_q_norm, da_k_norm, da_lq1, da_lk1, da_lq2, da_lk2, da_out_norm, gdn_conv_w, gdn_A_log, gdn_dt_bias, gdn_out_norm, ml_i_bias, ml_f_bias, ml_out_norm, ffn_w_up, ffn_conv_w, ffn_w_down):
    B, T, D = x_prompt.shape
    Bs, Ls, _ = x_sample.shape
    depth = w_in.shape[0]
    P = cache_attn_k.shape[2]
    NM = meta_tokens.shape[0]
    Fd = ffn_conv_w.shape[-1]
    W = HEADS * DA_DV
    assert T % CHUNK == 0 and Ls <= CHUNK and NM <= CHUNK and NM % 8 == 0 and Ls % 8 == 0

    tq = _pick(T, ATT_TQ)
    tk = _pick(tq, ATT_TK)
    tkc = _pick(P, (2048, 1024, 512, 256, 128))
    assert tk >= 96 or T == tk, "far prompt tiles must lie past the last distinct relative-position bucket"
    assert tkc >= 96 or P == tkc
    MP = 128

    r = jnp.arange(tq, dtype=jnp.int32)[:, None]
    c = jnp.arange(tk, dtype=jnp.int32)[None, :]
    tiles = []
    for d in range(-1, tq // tk):
        visible = ((c + tk * d) // CHUNK <= r // CHUNK)[None]
        tiles.append(jnp.where(visible, _rel_bias_toeplitz(rel_bias, tq, tk, tk * d), NEG_INF))
    bias_big = jnp.stack(tiles, axis=1)
    bias_far = _rel_bias_vec(rel_bias, jnp.full((1,), -2 * tk, jnp.int32))[:, 0]
    cm = jnp.arange(MP, dtype=jnp.int32)[None, :]
    bm0 = _rel_bias_toeplitz(rel_bias, tq, MP, -NM)
    bm1 = _rel_bias_const(rel_bias, -2 * tq, tq, MP)
    bias_meta_big = jnp.where((cm < NM)[None, None], jnp.stack([bm0, bm1], axis=1), NEG_INF)
    bias_new_s = _rel_bias_toeplitz(rel_bias, Ls, Ls, 0)
    bias_cache = jnp.stack([_rel_bias_const(rel_bias, -2 * tkc, Ls, tkc),
                            _rel_bias_toeplitz(rel_bias, Ls, tkc, -tkc)], axis=1)
    bias_new_m = _rel_bias_toeplitz(rel_bias, NM, NM, 0)

    sizes = (W, W, W, QKV_COLS, HEADS, HEADS, HEADS * HD, QKV_COLS, HEADS, HEADS, HEADS * HD)
    offs = [0]
    for s_ in sizes:
        offs.append(offs[-1] + s_)
    seg = lambda w, i: w[:, offs[i]:offs[i + 1]]

    xm = jnp.broadcast_to(meta_tokens.astype(F32)[None], (B, NM, D)).reshape(B * NM, D)
    xb = x_prompt.reshape(B * T, D)
    xs = x_sample.reshape(Bs * Ls, D)
    p_states, s_states = [], []
    k_prompt = v_prompt = None
    for l in range(depth):
        wl = w_in[l]
        w_main = jnp.concatenate([seg(wl, 0), seg(wl, 1), seg(wl, 2), seg(wl, 3), seg(wl, 7), seg(wl, 6), seg(wl, 10)],
                                 axis=1).astype(BF16)
        w_scal = jnp.concatenate([seg(wl, 4), seg(wl, 5), seg(wl, 8), seg(wl, 9),
                                  jnp.zeros((D, SCAL_COLS - 4 * HEADS), F32)], axis=1).astype(BF16)
        lw = {
            "norm_mix": norm_mix[l], "norm_ffn": norm_ffn[l], "w_in": w_main, "w_in_scal": w_scal,
            "w_out": w_out[l].astype(BF16), "w_up": ffn_w_up[l].astype(BF16), "w_down": ffn_w_down[l].astype(BF16),
            "da_q_norm": da_q_norm[l], "da_k_norm": da_k_norm[l], "gdn_conv_w": gdn_conv_w[l],
            "gdn_neg_a": -jnp.exp(gdn_A_log[l].astype(F32)), "gdn_dt_bias": gdn_dt_bias[l], "gdn_out_norm": gdn_out_norm[l],
            "ml_i_bias": ml_i_bias[l], "ml_f_bias": ml_f_bias[l], "ml_out_norm": ml_out_norm[l],
            "ffn_conv_w": ffn_conv_w[l],
        }
        lam_init = 0.8 - 0.6 * math.exp(-0.3 * l)
        lam = (jnp.exp(jnp.sum(da_lq1[l].astype(F32) * da_lk1[l].astype(F32)))
               - jnp.exp(jnp.sum(da_lq2[l].astype(F32) * da_lk2[l].astype(F32))) + lam_init).reshape(1, 1)
        out_scale = 1.0 - lam_init
        gain_o = da_out_norm[l]

        zero = {"gconv": jnp.zeros((B, GDN_CONV - 1, QKV_COLS), F32), "S": jnp.zeros((B, HEADS, HD, HD), F32),
                "C": jnp.zeros((B, HEADS, HD, HD), F32), "n": jnp.zeros((B, HEADS, HD), F32),
                "m": jnp.zeros((B, HEADS), F32), "fconv": jnp.zeros((B, FFN_CONV - 1, Fd), F32)}
        xm, st_m = _layer(xm, B, NM, NM, lw, zero,
                          lambda q, k, v: attn_small(q, k, v, None, bias_new_m, lam, gain_o, out_scale, B, NM),
                          dst=(k_prompt, v_prompt, depth, NM + T, l, 0))
        k_prompt, v_prompt = st_m["k"], st_m["v"]

        kmeta = jnp.pad(st_m["k16"].reshape(B, NM, W), ((0, 0), (0, MP - NM), (0, 0)))
        kmeta_t = jnp.transpose(kmeta, (0, 2, 1)).reshape(B * W, MP)
        vmeta = jnp.pad(st_m["v16"].reshape(B, NM, W), ((0, 0), (0, MP - NM), (0, 0))).reshape(B * MP, W)
        xb, st_b = _layer(xb, B, T, CHUNK, lw, st_m,
                          lambda q, kt, v: attn_big(q, kt, v, kmeta_t, vmeta, bias_big, bias_far, bias_meta_big, lam, gain_o,
                                                    out_scale, B, T, tq, tk, ATT_RB), want_kt=True,
                          dst=(k_prompt, v_prompt, depth, NM + T, l, NM))
        k_prompt, v_prompt = st_b["k"], st_b["v"]

        init_s = {"gconv": state_gdn_conv[l], "S": state_gdn_S[l].astype(F32), "C": state_mlstm_C[l].astype(F32),
                  "n": state_mlstm_n[l].astype(F32), "m": state_mlstm_m[l].astype(F32), "fconv": state_ffn_conv[l]}
        cache = (cache_attn_k[l].reshape(Bs * P, W).astype(BF16), cache_attn_v[l].reshape(Bs * P, W).astype(BF16),
                 bias_cache, tkc)
        xs, st_s = _layer(xs, Bs, Ls, Ls, lw, init_s,
                          lambda q, k, v: attn_small(q, k, v, cache, bias_new_s, lam, gain_o, out_scale, Bs, Ls))

        names = ("k", "v", "gconv", "S", "C", "n", "m", "fconv")
        p_states.append([st_b[n_] for n_ in names])
        s_states.append([st_s[n_] for n_ in names])

    p_out = [k_prompt, v_prompt] + [jnp.stack([p_states[l][i] for l in range(depth)]) for i in range(2, 8)]
    s_out = [jnp.stack([s_states[l][i] for l in range(depth)]) for i in range(8)]
    y_prompt = xb.reshape(B, T, D)
    y_sample = xs.reshape(Bs, Ls, D)
    return (y_prompt, y_sample, *p_out, *s_out)
```

```python
import functools
import math

import jax
import jax.numpy as jnp
from jax import lax
from jax.experimental import pallas as pl
from jax.experimental.pallas import tpu as pltpu

F32 = jnp.float32
BF16 = jnp.bfloat16

HEADS = 4
HD = 128
DA_DV = 2 * HD
CHUNK = 64
GDN_CONV = 4
FFN_CONV = 3
NUM_BUCKETS = 32
MAX_DISTANCE = 128
RMS_EPS = 1e-6
NEG_INF = -1e30
HALO = 8

QKV_COLS = 3 * HEADS * HD
Z_COLS = 3 * HEADS * DA_DV + 2 * QKV_COLS + 2 * HEADS * HD
SCAL_COLS = 128

LOG2E = math.log2(math.e)
Q_SCALE = HD ** -0.5 * LOG2E

VMEM_LIMIT = 56 * 1024 * 1024


def _pick(n, cands):
    for c in cands:
        if n % c == 0:
            return c
    raise ValueError(f"no tile for {n} in {cands}")


def _params(sem):
    return pltpu.CompilerParams(dimension_semantics=sem, vmem_limit_bytes=VMEM_LIMIT)


def _sigmoid(x):
    return 0.5 * jnp.tanh(0.5 * x) + 0.5


def _softplus(x):
    return jnp.maximum(x, 0.0) + jnp.log(1.0 + jnp.exp(-jnp.abs(x)))


def _dot_nt(a, b):
    return lax.dot_general(a, b, (((1,), (1,)), ((), ())), preferred_element_type=F32)


def _rms_matmul_kernel(x_ref, g_ref, w_ref, *rest, has_extra):
    if has_extra:
        ws_ref, o_ref, os_ref, xn_ref = rest
    else:
        o_ref, xn_ref = rest

    @pl.when(pl.program_id(1) == 0)
    def _():
        x = x_ref[...]
        ms = jnp.mean(x * x, axis=-1, keepdims=True)
        xn = (x * lax.rsqrt(ms + RMS_EPS) * g_ref[...]).astype(BF16)
        xn_ref[...] = xn
        if has_extra:
            os_ref[...] = jnp.dot(xn, ws_ref[...], preferred_element_type=F32)

    o_ref[...] = jnp.dot(xn_ref[...], w_ref[...], preferred_element_type=F32).astype(o_ref.dtype)


def rms_matmul(x, gain, w, w_extra=None, name="rms_matmul"):
    M, K = x.shape
    N = w.shape[1]
    tm = _pick(M, (1024, 512, 256, 128, 64, 32, 16, 8))
    tn = _pick(N, (1024, 512, 256, 128))
    has_extra = w_extra is not None
    in_specs = [pl.BlockSpec((tm, K), lambda i, j: (i, 0)),
                pl.BlockSpec((1, K), lambda i, j: (0, 0)),
                pl.BlockSpec((K, tn), lambda i, j: (0, j))]
    out_specs = [pl.BlockSpec((tm, tn), lambda i, j: (i, j))]
    out_shape = [jax.ShapeDtypeStruct((M, N), F32)]
    args = [x, gain.reshape(1, K), w]
    if has_extra:
        ne = w_extra.shape[1]
        in_specs.append(pl.BlockSpec((K, ne), lambda i, j: (0, 0)))
        out_specs.append(pl.BlockSpec((tm, ne), lambda i, j: (i, 0)))
        out_shape.append(jax.ShapeDtypeStruct((M, ne), F32))
        args.append(w_extra)
    outs = pl.pallas_call(
        functools.partial(_rms_matmul_kernel, has_extra=has_extra),
        grid=(M // tm, N // tn),
        in_specs=in_specs, out_specs=out_specs, out_shape=out_shape,
        scratch_shapes=[pltpu.VMEM((tm, K), BF16)],
        compiler_params=_params(("parallel", "arbitrary")),
        name=name,
    )(*args)
    return outs if has_extra else outs[0]


def _matmul_res_kernel(*refs, n_a):
    a_refs, w_refs = refs[:n_a], refs[n_a:2 * n_a]
    res_ref, o_ref = refs[2 * n_a], refs[2 * n_a + 1]
    acc = res_ref[...]
    for a_ref, w_ref in zip(a_refs, w_refs):
        acc = acc + jnp.dot(a_ref[...], w_ref[...], preferred_element_type=F32)
    o_ref[...] = acc


def matmul_res(a_list, w, res, name="matmul_res"):
    M, N = res.shape
    ktot = sum(a.shape[1] for a in a_list)
    tm = _pick(M, (1024, 512, 256, 128, 64, 32, 16, 8))
    tn = _pick(N, (1024, 512, 256, 128) if ktot <= 4096 else (512, 256, 128))
    in_specs, off = [], 0
    for a in a_list:
        in_specs.append(pl.BlockSpec((tm, a.shape[1]), lambda i, j: (i, 0)))
    for a in a_list:
        k = a.shape[1]
        assert off % k == 0
        in_specs.append(pl.BlockSpec((k, tn), lambda i, j, _o=off // k: (_o, j)))
        off += k
    in_specs.append(pl.BlockSpec((tm, tn), lambda i, j: (i, j)))
    return pl.pallas_call(
        functools.partial(_matmul_res_kernel, n_a=len(a_list)),
        grid=(M // tm, N // tn),
        in_specs=in_specs,
        out_specs=pl.BlockSpec((tm, tn), lambda i, j: (i, j)),
        out_shape=jax.ShapeDtypeStruct((M, N), F32),
        compiler_params=_params(("parallel", "arbitrary")),
        name=name,
    )(*a_list, *([w] * len(a_list)), res)


def _qk_prep_kernel(q_ref, k_ref, v_ref, gq_ref, gk_ref, *rest, want_kt, direct, n_buf):
    rest = rest[n_buf:]
    qn_ref, kn32_ref, kn16_ref, v16_ref = rest[:4]
    rest = rest[4:]
    knt_ref = rest[0] if want_kt else None
    v32_ref = rest[-1] if direct else None
    gq = gq_ref[...]
    gk = gk_ref[...]
    groups = [slice(g * HD, (g + 1) * HD) for g in range(2 * HEADS)]
    q = [q_ref[:, sl] for sl in groups]
    k = [k_ref[:, sl] for sl in groups]
    q2 = [jnp.mean(x * x, axis=-1, keepdims=True) for x in q]
    k2 = [jnp.mean(x * x, axis=-1, keepdims=True) for x in k]
    for g, (sl, q_, k_, q2_, k2_) in enumerate(zip(groups, q, k, q2, k2)):
        qn = q_ * lax.rsqrt(q2_ + RMS_EPS) * gq
        kn = k_ * lax.rsqrt(k2_ + RMS_EPS) * gk
        qn_ref[:, sl] = (qn * Q_SCALE).astype(BF16)
        if direct:
            kn32_ref[0, 0, :, g // 2, g % 2, :] = kn
        else:
            kn32_ref[:, sl] = kn
        kn16_ref[:, sl] = kn.astype(BF16)
        if want_kt:
            knt_ref[sl, :] = kn.T.astype(BF16)
    v = v_ref[...]
    v16_ref[...] = v.astype(BF16)
    if direct:
        for h in range(HEADS):
            v32_ref[0, 0, :, h, :] = v[:, h * DA_DV:(h + 1) * DA_DV]


def qk_prep(z, gq, gk, B, T, want_kt=False, dst=None):
    M = z.shape[0]
    W = 2 * HEADS * HD
    tm = _pick(T, (512, 256, 128, 64, 32, 16, 8))
    tpb = T // tm
    col = lambda c: pl.BlockSpec((tm, W), lambda i, _c=c: (i, _c))
    vec = pl.BlockSpec((1, HD), lambda i: (0, 0))
    row = pl.BlockSpec((tm, W), lambda i: (i, 0))
    in_specs = [col(0), col(1), col(2), vec, vec]
    args = [z, z, z, gq.reshape(1, HD), gk.reshape(1, HD)]
    out_specs = [row, row, row, row]
    out_shape = [jax.ShapeDtypeStruct((M, W), BF16), jax.ShapeDtypeStruct((M, W), F32),
                 jax.ShapeDtypeStruct((M, W), BF16), jax.ShapeDtypeStruct((M, W), BF16)]
    if want_kt:
        out_specs.append(pl.BlockSpec((W, tm), lambda i: (0, i)))
        out_shape.append(jax.ShapeDtypeStruct((W, M), BF16))
    aliases = {}
    if dst is not None:
        kbuf, vbuf, depth, rows, layer, off = dst
        where = lambda i: (layer, i // tpb, off + (i % tpb) * tm)
        out_specs[1] = pl.BlockSpec(tuple(pl.Element(d) for d in (1, 1, tm, HEADS, 2, HD)),
                                    lambda i: where(i) + (0, 0, 0))
        out_shape[1] = jax.ShapeDtypeStruct((depth, B, rows, HEADS, 2, HD), F32)
        out_specs.append(pl.BlockSpec(tuple(pl.Element(d) for d in (1, 1, tm, HEADS, DA_DV)),
                                      lambda i: where(i) + (0, 0)))
        out_shape.append(jax.ShapeDtypeStruct((depth, B, rows, HEADS, DA_DV), F32))
        if kbuf is not None:
            in_specs += [pl.BlockSpec(memory_space=pl.ANY)] * 2
            args += [kbuf, vbuf]
            aliases = {len(args) - 2: 1, len(args) - 1: len(out_shape) - 1}
    return pl.pallas_call(
        functools.partial(_qk_prep_kernel, want_kt=want_kt, direct=dst is not None, n_buf=len(aliases)),
        grid=(M // tm,),
        in_specs=in_specs,
        out_specs=out_specs,
        out_shape=out_shape,
        input_output_aliases=aliases,
        compiler_params=_params(("parallel",)),
        name="qk_prep",
    )(*args)


def _softmax_update(mi, s, v, m_ref, l_ref, acc_ref):
    m_prev = m_ref[mi]
    m_new = jnp.maximum(m_prev, jnp.max(s, axis=-1, keepdims=True))
    alpha = jnp.exp2(m_prev - m_new)
    p = jnp.exp2(s - m_new)
    l_ref[mi] = alpha * l_ref[mi] + jnp.sum(p, axis=-1, keepdims=True)
    acc_ref[mi] = alpha * acc_ref[mi] + jnp.dot(p.astype(BF16), v, preferred_element_type=F32)
    m_ref[mi] = m_new


def _attn_init(m_ref, l_ref, acc_ref):
    m_ref[...] = jnp.full(m_ref.shape, -jnp.inf, F32)
    l_ref[...] = jnp.zeros(l_ref.shape, F32)
    acc_ref[...] = jnp.zeros(acc_ref.shape, F32)


def _attn_tile(q, k, v, bias, m_ref, l_ref, acc_ref):
    for mi in range(2):
        sl = slice(mi * HD, (mi + 1) * HD)
        s = _dot_nt(q[:, sl], k[:, sl]) + bias
        _softmax_update(mi, s, v, m_ref, l_ref, acc_ref)


def _attn_finish(lam, out_scale, g_ref, o_ref, l_ref, acc_ref):
    o = acc_ref[0] / l_ref[0] - lam * (acc_ref[1] / l_ref[1])
    on = o * lax.rsqrt(jnp.mean(o * o, axis=-1, keepdims=True) + RMS_EPS) * g_ref[...]
    o_ref[...] = (on * out_scale).astype(o_ref.dtype)


def _attn_scratch(tq):
    return [pltpu.VMEM((2, tq, 1), F32), pltpu.VMEM((2, tq, 1), F32), pltpu.VMEM((2, tq, DA_DV), F32)]


LANES = 128
ATT_TQ = (1024, 512, 256, 128, 64)
ATT_TK = (1024, 512, 256, 128, 64)
ATT_RB = 256


def _rowblock_tile(q_ref, kt_ref, v_ref, bias_ref, shift, m_ref, l_ref, acc_ref, rb, diagonal=False):
    tq = q_ref.shape[0]
    tk = kt_ref.shape[1]
    rb = min(rb, tq)
    nrb = tq // rb
    assert rb % CHUNK == 0 or not diagonal
    ncols = lambda r: min(tk, (r + 1) * rb) if diagonal else tk

    def scores(r):
        rows = pl.ds(r * rb, rb)
        out = [jnp.dot(q_ref[rows, mi * HD:(mi + 1) * HD], kt_ref[mi * HD:(mi + 1) * HD, 0:ncols(r)],
                       preferred_element_type=F32) for mi in range(2)]
        if bias_ref is not None:
            bias = bias_ref[0, 0, rows, 0:ncols(r)]
            out = [s + bias for s in out]
        return out

    def update(r, s_pair):
        rows = pl.ds(r * rb, rb)
        nb = ncols(r) // LANES
        v = v_ref[0:ncols(r), :]
        for mi in range(2):
            s = s_pair[mi]
            blocks = [s[:, c * LANES:(c + 1) * LANES] for c in range(nb)]
            smax = blocks[0]
            for blk in blocks[1:]:
                smax = jnp.maximum(smax, blk)
            m_prev = m_ref[mi, rows, :]
            if shift is not None:
                m_prev = m_prev - shift
            m_new = jnp.maximum(m_prev, jnp.max(smax, axis=-1, keepdims=True))
            alpha = jnp.exp2(m_prev - m_new)
            ps = [jnp.exp2(blk - m_new) for blk in blocks]
            psum = ps[0]
            for pb in ps[1:]:
                psum = psum + pb
            l_ref[mi, rows, :] = alpha * l_ref[mi, rows, :] + psum
            m_ref[mi, rows, :] = m_new if shift is None else m_new + shift
            p = jnp.concatenate(ps, axis=1).astype(BF16) if nb > 1 else ps[0].astype(BF16)
            pv = jnp.dot(p, v, preferred_element_type=F32)
            acc_ref[mi, rows, :] = jnp.concatenate([alpha, alpha], axis=1) * acc_ref[mi, rows, :] + pv

    s_next = scores(0)
    for r in range(nrb):
        s_cur = s_next
        if r + 1 < nrb:
            s_next = scores(r + 1)
        update(r, s_cur)


def _attn_big_kernel(qi_ref, kj_ref, lam_ref, far_ref, q_ref, k_ref, v_ref, km_ref, vm_ref, bias_ref, bmeta_ref,
                     g_ref, o_ref, m_ref, l_ref, acc_ref, *, out_scale, ratio, rb):
    step = pl.program_id(2)
    i = qi_ref[step]
    j = kj_ref[step]
    is_far = j - ratio * i < -1

    @pl.when(j == 0)
    def _():
        _attn_init(m_ref, l_ref, acc_ref)
        _rowblock_tile(q_ref, km_ref, vm_ref, bmeta_ref, None, m_ref, l_ref, acc_ref, rb)

    @pl.when(is_far)
    def _():
        _rowblock_tile(q_ref, k_ref, v_ref, None, far_ref[pl.program_id(1)], m_ref, l_ref, acc_ref, rb)

    if ratio == 1:
        @pl.when(j == i - 1)
        def _():
            _rowblock_tile(q_ref, k_ref, v_ref, bias_ref, None, m_ref, l_ref, acc_ref, rb)

        @pl.when(j == i)
        def _():
            _rowblock_tile(q_ref, k_ref, v_ref, bias_ref, None, m_ref, l_ref, acc_ref, rb, diagonal=True)
    else:
        @pl.when(jnp.logical_not(is_far))
        def _():
            _rowblock_tile(q_ref, k_ref, v_ref, bias_ref, None, m_ref, l_ref, acc_ref, rb)

    @pl.when(j == ratio * (i + 1) - 1)
    def _():
        l0 = jnp.sum(l_ref[0], axis=-1, keepdims=True)
        l1 = jnp.sum(l_ref[1], axis=-1, keepdims=True)
        o = acc_ref[0] / l0 - lam_ref[0, 0] * (acc_ref[1] / l1)
        on = o * lax.rsqrt(jnp.mean(o * o, axis=-1, keepdims=True) + RMS_EPS) * g_ref[...]
        o_ref[...] = (on * out_scale).astype(o_ref.dtype)


def attn_big(qn, knt16, v16, kmeta_t, vmeta, bias, far, bmeta, lam, gain, out_scale, B, T, tq, tk, rb):
    nq, nk = T // tq, T // tk
    ratio = tq // tk
    mp = vmeta.shape[0] // B
    pairs = [(i, j) for i in range(nq) for j in range(ratio * (i + 1))]
    qi = jnp.asarray([p[0] for p in pairs], jnp.int32)
    kj = jnp.asarray([p[1] for p in pairs], jnp.int32)
    grid_spec = pltpu.PrefetchScalarGridSpec(
        num_scalar_prefetch=2,
        grid=(B, HEADS, len(pairs)),
        in_specs=[
            pl.BlockSpec(memory_space=pltpu.SMEM),
            pl.BlockSpec(memory_space=pltpu.SMEM),
            pl.BlockSpec((tq, DA_DV), lambda b, h, s, qi, kj: (b * nq + qi[s], h)),
            pl.BlockSpec((DA_DV, tk), lambda b, h, s, qi, kj: (h, b * nk + kj[s])),
            pl.BlockSpec((tk, DA_DV), lambda b, h, s, qi, kj: (b * nk + kj[s], h)),
            pl.BlockSpec((DA_DV, mp), lambda b, h, s, qi, kj: (b * HEADS + h, 0)),
            pl.BlockSpec((mp, DA_DV), lambda b, h, s, qi, kj: (b, h)),
            pl.BlockSpec((1, 1, tq, tk),
                         lambda b, h, s, qi, kj: (h, jnp.maximum(kj[s] - ratio * qi[s] + 1, 0), 0, 0)),
            pl.BlockSpec((1, 1, tq, mp), lambda b, h, s, qi, kj: (h, jnp.minimum(qi[s], 1), 0, 0)),
            pl.BlockSpec((1, DA_DV), lambda b, h, s, qi, kj: (0, 0)),
        ],
        out_specs=pl.BlockSpec((tq, DA_DV), lambda b, h, s, qi, kj: (b * nq + qi[s], h)),
        scratch_shapes=[pltpu.VMEM((2, tq, LANES), F32), pltpu.VMEM((2, tq, LANES), F32),
                        pltpu.VMEM((2, tq, DA_DV), F32)],
    )
    return pl.pallas_call(
        functools.partial(_attn_big_kernel, out_scale=out_scale, ratio=ratio, rb=rb),
        grid_spec=grid_spec,
        out_shape=jax.ShapeDtypeStruct((B * T, HEADS * DA_DV), BF16),
        compiler_params=_params(("parallel", "parallel", "arbitrary")),
        name="attn_prompt",
    )(qi, kj, lam, far, qn, knt16, v16, kmeta_t, vmeta, bias, bmeta, gain.reshape(1, DA_DV))


def _attn_small_kernel(lam_ref, q_ref, *rest, nc, out_scale):
    if nc > 0:
        ck_ref, cv_ref, bc_ref, k_ref, v_ref, bn_ref, g_ref, o_ref, m_ref, l_ref, acc_ref = rest
    else:
        k_ref, v_ref, bn_ref, g_ref, o_ref, m_ref, l_ref, acc_ref = rest
    j = pl.program_id(2)
    q = q_ref[...]

    @pl.when(j == 0)
    def _():
        _attn_init(m_ref, l_ref, acc_ref)

    if nc > 0:
        @pl.when(j < nc)
        def _():
            _attn_tile(q, ck_ref[...].astype(BF16), cv_ref[...].astype(BF16), bc_ref[0, 0], m_ref, l_ref, acc_ref)

    @pl.when(j == nc)
    def _():
        _attn_tile(q, k_ref[...], v_ref[...], bn_ref[0], m_ref, l_ref, acc_ref)
        _attn_finish(lam_ref[0, 0], out_scale, g_ref, o_ref, l_ref, acc_ref)


def attn_small(qn, kn16, v16, cache, bias_new, lam, gain, out_scale, B, L):
    W = HEADS * DA_DV
    blk = pl.BlockSpec((L, DA_DV), lambda b, h, j: (b, h))
    in_specs = [pl.BlockSpec(memory_space=pltpu.SMEM), blk]
    args = [lam, qn]
    nc = 0
    if cache is not None:
        ck, cv, bc, tk = cache
        nc = ck.shape[0] // B // tk
        cspec = pl.BlockSpec((tk, DA_DV), lambda b, h, j: (b * nc + jnp.minimum(j, nc - 1), h))
        in_specs += [cspec, cspec,
                     pl.BlockSpec((1, 1, L, tk), lambda b, h, j: (h, jnp.where(j >= nc - 1, 1, 0), 0, 0))]
        args += [ck, cv, bc]
    in_specs += [blk, blk, pl.BlockSpec((1, L, L), lambda b, h, j: (h, 0, 0)),
                 pl.BlockSpec((1, DA_DV), lambda b, h, j: (0, 0))]
    args += [kn16, v16, bias_new, gain.reshape(1, DA_DV)]
    return pl.pallas_call(
        functools.partial(_attn_small_kernel, nc=nc, out_scale=out_scale),
        grid=(B, HEADS, nc + 1),
        in_specs=in_specs,
        out_specs=blk,
        out_shape=jax.ShapeDtypeStruct((B * L, W), BF16),
        scratch_shapes=_attn_scratch(L),
        compiler_params=_params(("parallel", "parallel", "arbitrary")),
        name="attn_block",
    )(*args)


_NN = (((1,), (0,)), ((), ()))
_NT = (((1,), (1,)), ((), ()))
_TN = (((0,), (0,)), ((), ()))


def _dg(a, b, dn):
    return lax.dot_general(a, b, dn, preferred_element_type=F32)


def _bf16r(a):
    return a.astype(BF16).astype(F32)


def _mm3(a, b, dn=_NN):
    ah = _bf16r(a)
    al = a - ah
    bh = _bf16r(b)
    bl = b - bh
    return _dg(ah, bh, dn) + (_dg(ah, bl, dn) + _dg(al, bh, dn))


def _split3(a):
    hi = _bf16r(a)
    r1 = a - hi
    mid = _bf16r(r1)
    return hi, mid, r1 - mid


def _chunk_masks(L):
    row = lax.broadcasted_iota(jnp.int32, (L, L), 0)
    col = lax.broadcasted_iota(jnp.int32, (L, L), 1)
    return row, col


def _cumsum_cols_rows(x, tril, triu):
    hi, mid, lo = _split3(x)
    cols = _dg(tril, hi, _NN) + (_dg(tril, mid, _NN) + _dg(tril, lo, _NN))
    rows = _dg(hi, triu, _TN) + (_dg(mid, triu, _TN) + _dg(lo, triu, _TN))
    return cols, rows


def _gdn_prep_kernel(x_ref, halo_ref, prev_ref, sc_ref, cw_ref, nega_ref, dtb_ref,
                     w_ref, u_ref, qg_ref, kd_ref, qk_ref, gc_ref, xbuf, *, L, G):
    i = pl.program_id(1)
    tile = L * G
    xbuf[0:HALO, :] = jnp.where(i == 0, prev_ref[0], halo_ref[...])
    xbuf[HALO:HALO + tile, :] = x_ref[...]
    cw = cw_ref[...]
    y = xbuf[HALO - 3:HALO - 3 + tile, :] * cw[0:1, :]
    for t in range(1, GDN_CONV):
        y = y + xbuf[HALO - 3 + t:HALO - 3 + t + tile, :] * cw[t:t + 1, :]
    gcv = y * _sigmoid(y)

    row, col = _chunk_masks(L)
    incl = col <= row
    strict = col < row
    eye = (col == row).astype(F32)
    tril = incl.astype(F32)
    triu = (row <= col).astype(F32)
    sc = sc_ref[...]
    beta_all = _sigmoid(sc)
    g_all = nega_ref[...] * _softplus(sc + dtb_ref[...])
    nqk = HEADS * HD

    ids = [(c, h) for c in range(G) for h in range(HEADS)]
    cum = [_cumsum_cols_rows(g_all[c * L:(c + 1) * L], tril, triu) for c in range(G)]
    for c in range(G):
        gc_ref[c * L:(c + 1) * L, :] = cum[c][0]
    xq = [gcv[c * L:(c + 1) * L, h * HD:(h + 1) * HD] for c, h in ids]
    xk = [gcv[c * L:(c + 1) * L, nqk + h * HD:nqk + (h + 1) * HD] for c, h in ids]
    xv = [gcv[c * L:(c + 1) * L, 2 * nqk + h * HD:2 * nqk + (h + 1) * HD] for c, h in ids]
    q2 = [jnp.sum(x * x, axis=-1, keepdims=True) for x in xq]
    k2 = [jnp.sum(x * x, axis=-1, keepdims=True) for x in xk]
    qn = [x * lax.rsqrt(s2 + RMS_EPS) * (HD ** -0.5) for x, s2 in zip(xq, q2)]
    kn = [x * lax.rsqrt(s2 + RMS_EPS) for x, s2 in zip(xk, k2)]
    beta = [beta_all[c * L:(c + 1) * L, h:h + 1] for c, h in ids]
    gcol = [cum[c][0][:, HEADS + h:HEADS + h + 1] for c, h in ids]
    grow = [cum[c][1][HEADS + h:HEADS + h + 1, :] for c, h in ids]
    dec = [jnp.where(incl, jnp.exp(jnp.where(incl, a - b, 0.0)), 0.0) for a, b in zip(gcol, grow)]
    eg = [jnp.exp(a) for a in gcol]
    st = [dict(rows=slice(c * L, (c + 1) * L), h=h, qn=qn[t], kn=kn[t], bv=beta[t] * xv[t],
               bek=beta[t] * eg[t] * kn[t], beta=beta[t], dec=dec[t], eg=eg[t],
               kd=kn[t] * jnp.exp(gcol[t][L - 1:L, :] - gcol[t])) for t, (c, h) in enumerate(ids)]

    kk = [_mm3(s["kn"], s["kn"], _NT) for s in st]
    nmat = [jnp.where(strict, s["beta"] * k_ * s["dec"], 0.0) for s, k_ in zip(st, kk)]
    tinv = [eye - n for n in nmat]
    pw = [_mm3(n, n) for n in nmat]
    p = 2
    while True:
        tinv = [t + _mm3(t, q) for t, q in zip(tinv, pw)]
        p *= 2
        if p >= L:
            break
        pw = [_mm3(q, q) for q in pw]
    w_mat = [_mm3(t, s["bek"]) for t, s in zip(tinv, st)]
    u_mat = [_mm3(t, s["bv"]) for t, s in zip(tinv, st)]
    qk = [_mm3(s["qn"], s["kn"], _NT) * s["dec"] for s in st]
    for s, w_, u_, qk_ in zip(st, w_mat, u_mat, qk):
        rows, h = s["rows"], s["h"]
        sl = slice(h * HD, (h + 1) * HD)
        w_ref[rows, sl] = w_
        u_ref[rows, sl] = u_
        qg_ref[rows, sl] = s["eg"] * s["qn"]
        kd_ref[rows, sl] = s["kd"]
        qk_ref[rows, h * L:(h + 1) * L] = qk_


def _gdn_scan_kernel(w_ref, u_ref, qg_ref, kd_ref, qk_ref, gc_ref, gate_ref, s0_ref, gn_ref,
                     o_ref, sout_ref, s_scr, *, L, cps, B):
    i = pl.program_id(0)

    @pl.when(i == 0)
    def _():
        s_scr[...] = s0_ref[...]

    chains = [(b, h) for b in range(B) for h in range(HEADS)]
    gn = gn_ref[...]
    for c in range(cps):
        rows = slice(c * L, (c + 1) * L)
        last = (c + 1) * L - 1
        s_old = [s_scr[b, h] for b, h in chains]
        ws = [_mm3(w_ref[b, rows, h * HD:(h + 1) * HD], s_) for (b, h), s_ in zip(chains, s_old)]
        u = [u_ref[b, rows, h * HD:(h + 1) * HD] - w_ for (b, h), w_ in zip(chains, ws)]
        upd = [_mm3(kd_ref[b, rows, h * HD:(h + 1) * HD], u_, _TN) for (b, h), u_ in zip(chains, u)]
        ds = [jnp.exp(gc_ref[b, last:last + 1, HEADS + h:HEADS + h + 1]) for b, h in chains]
        for (b, h), s_, d_, ds_ in zip(chains, s_old, upd, ds):
            s_scr[b, h] = ds_ * s_ + d_
        o1 = [_mm3(qg_ref[b, rows, h * HD:(h + 1) * HD], s_) for (b, h), s_ in zip(chains, s_old)]
        o2 = [_mm3(qk_ref[b, rows, h * L:(h + 1) * L], u_) for (b, h), u_ in zip(chains, u)]
        o = [a_ + b_ for a_, b_ in zip(o1, o2)]
        ms = [jnp.mean(o_ * o_, axis=-1, keepdims=True) for o_ in o]
        for (b, h), o_, ms_ in zip(chains, o, ms):
            sl = slice(h * HD, (h + 1) * HD)
            on = o_ * lax.rsqrt(ms_ + RMS_EPS) * gn
            gt = gate_ref[b, rows, sl]
            o_ref[b, rows, sl] = (on * (gt * _sigmoid(gt))).astype(o_ref.dtype)

    @pl.when(i == pl.num_programs(0) - 1)
    def _():
        sout_ref[...] = s_scr[...]


def gdn(z, sc, prev8, conv_w, neg_a, dt_bias, s0, gn, B, T, L):
    M = B * T
    nchunks = T // L
    G = _pick(nchunks, (4, 2, 1))
    tile = L * G
    nt = T // tile
    hb = tile // HALO
    nega_vec = jnp.zeros((1, SCAL_COLS), F32).at[0, HEADS:2 * HEADS].set(neg_a)
    dtb_vec = jnp.zeros((1, SCAL_COLS), F32).at[0, HEADS:2 * HEADS].set(dt_bias.astype(F32))
    rowblk = lambda w: pl.BlockSpec((tile, w), lambda b, i: (b * nt + i, 0))
    vec = pl.BlockSpec((1, SCAL_COLS), lambda b, i: (0, 0))
    w_mat, u_mat, qg, kd, qk, gc = pl.pallas_call(
        functools.partial(_gdn_prep_kernel, L=L, G=G),
        grid=(B, nt),
        in_specs=[pl.BlockSpec((tile, QKV_COLS), lambda b, i: (b * nt + i, 2)),
                  pl.BlockSpec((HALO, QKV_COLS), lambda b, i: (jnp.maximum((b * nt + i) * hb - 1, 0), 2)),
                  pl.BlockSpec((1, HALO, QKV_COLS), lambda b, i: (b, 0, 0)),
                  rowblk(SCAL_COLS),
                  pl.BlockSpec((GDN_CONV, QKV_COLS), lambda b, i: (0, 0)),
                  vec, vec],
        out_specs=[rowblk(HEADS * HD)] * 4 + [rowblk(HEADS * L), rowblk(SCAL_COLS)],
        out_shape=[jax.ShapeDtypeStruct((M, HEADS * HD), F32)] * 4
                  + [jax.ShapeDtypeStruct((M, HEADS * L), F32), jax.ShapeDtypeStruct((M, SCAL_COLS), F32)],
        scratch_shapes=[pltpu.VMEM((tile + HALO, QKV_COLS), F32)],
        compiler_params=_params(("parallel", "parallel")),
        name="gdn_prep",
    )(z, z, prev8, sc, conv_w, nega_vec, dtb_vec)

    cps = _pick(nchunks, (8, 4, 2, 1))
    rows = cps * L
    r3 = lambda a: a.reshape(B, T, a.shape[-1])
    blk = lambda w, cb=0: pl.BlockSpec((B, rows, w), lambda i, _c=cb: (0, i, _c))
    st = pl.BlockSpec((B, HEADS, HD, HD), lambda i: (0, 0, 0, 0))
    o, s_new = pl.pallas_call(
        functools.partial(_gdn_scan_kernel, L=L, cps=cps, B=B),
        grid=(nchunks // cps,),
        in_specs=[blk(HEADS * HD)] * 4 + [blk(HEADS * L), blk(SCAL_COLS), blk(HEADS * HD, 12), st,
                                          pl.BlockSpec((1, HD), lambda i: (0, 0))],
        out_specs=[blk(HEADS * HD), st],
        out_shape=[jax.ShapeDtypeStruct((B, T, HEADS * HD), BF16), jax.ShapeDtypeStruct((B, HEADS, HD, HD), F32)],
        scratch_shapes=[pltpu.VMEM((B, HEADS, HD, HD), F32)],
        compiler_params=_params(("arbitrary",)),
        name="gdn_scan",
    )(r3(w_mat), r3(u_mat), r3(qg), r3(kd), r3(qk), r3(gc), r3(z), s0, gn.reshape(1, HD))
    return o.reshape(M, HEADS * HD), s_new


def _mlstm_prep_kernel(x_ref, sc_ref, ib_ref, fb_ref, pv_ref, u_ref, n_ref, st_ref, *, L, G):
    row, col = _chunk_masks(L)
    incl = col <= row
    eye = (col == row).astype(F32)
    tril = incl.astype(F32)
    triu = (row <= col).astype(F32)
    sc = sc_ref[...]
    ig_all = sc + ib_ref[...]
    lf_all = -_softplus(-(sc + fb_ref[...]))
    lane = lax.broadcasted_iota(jnp.int32, (L, SCAL_COLS), 1)
    nqk = HEADS * HD
    ids = [(c, h) for c in range(G) for h in range(HEADS)]
    crow = lambda c: slice(c * L, (c + 1) * L)
    cum = [_cumsum_cols_rows(lf_all[crow(c)], tril, triu) for c in range(G)]
    ig3 = [_split3(ig_all[crow(c)]) for c in range(G)]
    ig_rows = [_dg(a, eye, _TN) + (_dg(b, eye, _TN) + _dg(c_, eye, _TN)) for a, b, c_ in ig3]
    bcol = [cum[c][0][:, 3 * HEADS + h:3 * HEADS + h + 1] for c, h in ids]
    brow = [cum[c][1][3 * HEADS + h:3 * HEADS + h + 1, :] for c, h in ids]
    igrow = [ig_rows[c][2 * HEADS + h:2 * HEADS + h + 1, :] for c, h in ids]
    igcol = [ig_all[crow(c), 2 * HEADS + h:2 * HEADS + h + 1] for c, h in ids]
    dmat = [jnp.where(incl, a - b + g_, NEG_INF) for a, b, g_ in zip(bcol, brow, igrow)]
    dmax = [jnp.max(d_, axis=-1, keepdims=True) for d_ in dmat]
    q = [x_ref[crow(c), h * HD:(h + 1) * HD] for c, h in ids]
    k = [x_ref[crow(c), nqk + h * HD:nqk + (h + 1) * HD] * (HD ** -0.5) for c, h in ids]
    v = [x_ref[crow(c), 2 * nqk + h * HD:2 * nqk + (h + 1) * HD] for c, h in ids]
    kw = [jnp.exp(b_[L - 1:L, :] - b_ + g_ - m_[L - 1:L, :]) * k_ for b_, g_, m_, k_ in zip(bcol, igcol, dmax, k)]
    w = [jnp.exp(d_ - m_) for d_, m_ in zip(dmat, dmax)]
    qk = [_mm3(q_, k_, _NT) * w_ for q_, k_, w_ in zip(q, k, w)]
    pv = [_mm3(a_, v_) for a_, v_ in zip(qk, v)]
    um = [_mm3(kw_, v_, _TN) for kw_, v_ in zip(kw, v)]
    rs = [jnp.sum(a_, axis=-1, keepdims=True) for a_ in qk]
    ns = [jnp.sum(kw_, axis=0, keepdims=True) for kw_ in kw]
    for t, (c, h) in enumerate(ids):
        sl = slice(h * HD, (h + 1) * HD)
        pv_ref[crow(c), sl] = pv[t]
        u_ref[c * HD:(c + 1) * HD, sl] = um[t]
        n_ref[c * 8:(c + 1) * 8, sl] = jnp.broadcast_to(ns[t], (8, HD))
    for c in range(G):
        stats = jnp.where(lane >= 3 * HEADS, cum[c][0], 0.0)
        for h in range(HEADS):
            stats = jnp.where(lane == h, dmax[c * HEADS + h], stats)
            stats = jnp.where(lane == HEADS + h, rs[c * HEADS + h], stats)
        st_ref[crow(c), :] = stats


def _mlstm_scan_kernel(x_ref, og_ref, pv_ref, u_ref, n_ref, st_ref, c0_ref, n0_ref, m0_ref, gn_ref,
                       o_ref, cout_ref, nout_ref, mout_ref, c_scr, n_scr, m_scr, *, L, cps, B):
    i = pl.program_id(0)

    @pl.when(i == 0)
    def _():
        c_scr[...] = c0_ref[...]
        n_scr[...] = n0_ref[...]
        m_scr[...] = m0_ref[...]

    chains = [(b, h) for b in range(B) for h in range(HEADS)]
    gn = gn_ref[...]
    for c in range(cps):
        rows = slice(c * L, (c + 1) * L)
        n_ch = range(len(chains))
        hs = [slice(h * HD, (h + 1) * HD) for _, h in chains]
        dmax = [st_ref[b, rows, h:h + 1] for b, h in chains]
        rsum = [st_ref[b, rows, HEADS + h:HEADS + h + 1] for b, h in chains]
        bcol = [st_ref[b, rows, 3 * HEADS + h:3 * HEADS + h + 1] for b, h in chains]
        m_prev = [m_scr[b, h, 0:1, 0:1] for b, h in chains]
        inter = [bcol[t] + m_prev[t] for t in n_ch]
        m_t = [jnp.maximum(inter[t], dmax[t]) for t in n_ch]
        m_new = [m_t[t][L - 1:L, :] for t in n_ch]
        f_new = [jnp.exp(dmax[t][L - 1:L, :] - m_new[t]) for t in n_ch]
        dec0 = [jnp.exp(bcol[t][L - 1:L, :] + m_prev[t] - m_new[t]) for t in n_ch]
        c_old = [c_scr[b, h] for b, h in chains]
        n_old = [n_scr[b, h] for b, h in chains]
        q = [x_ref[b, rows, hs[t]] for t, (b, _) in enumerate(chains)]
        for t, (b, h) in enumerate(chains):
            c_scr[b, h] = dec0[t] * c_old[t] + f_new[t] * u_ref[b, c * HD:(c + 1) * HD, hs[t]]
            n_scr[b, h] = dec0[t] * n_old[t] + f_new[t] * n_ref[b, c * 8:(c + 1) * 8, hs[t]]
            m_scr[b, h] = jnp.broadcast_to(m_new[t], (8, HD))
        qc = [_mm3(q[t], c_old[t]) for t in n_ch]
        w_inter = [jnp.exp(inter[t] - m_t[t]) for t in n_ch]
        e1 = [jnp.exp(dmax[t] - m_t[t]) for t in n_ch]
        qn = [jnp.sum(q[t] * n_old[t][0:1, :], axis=-1, keepdims=True) for t in n_ch]
        den = [w_inter[t] * qn[t] + e1[t] * rsum[t] for t in n_ch]
        scale = [1.0 / jnp.maximum(jnp.abs(den[t]), jnp.exp(-m_t[t])) for t in n_ch]
        hout = [(w_inter[t] * qc[t] + e1[t] * pv_ref[b, rows, hs[t]]) * scale[t] for t, (b, _) in enumerate(chains)]
        ms = [jnp.mean(hout[t] * hout[t], axis=-1, keepdims=True) for t in n_ch]
        for t, (b, h) in enumerate(chains):
            on = hout[t] * lax.rsqrt(ms[t] + RMS_EPS) * gn
            o_ref[b, rows, hs[t]] = (on * _sigmoid(og_ref[b, rows, hs[t]])).astype(o_ref.dtype)

    @pl.when(i == pl.num_programs(0) - 1)
    def _():
        cout_ref[...] = c_scr[...]
        nout_ref[...] = n_scr[...]
        mout_ref[...] = m_scr[...]


def mlstm(z, sc, i_bias, f_bias, c0, n0p, m0p, gn, B, T, L):
    M = B * T
    nchunks = T // L
    G = _pick(nchunks, (4, 2, 1))
    tile = L * G
    nt = T // tile
    ib_vec = jnp.zeros((1, SCAL_COLS), F32).at[0, 2 * HEADS:3 * HEADS].set(i_bias.astype(F32))
    fb_vec = jnp.zeros((1, SCAL_COLS), F32).at[0, 3 * HEADS:4 * HEADS].set(f_bias.astype(F32))
    rowblk = lambda r, w, cb=0: pl.BlockSpec((r, w), lambda b, i, _c=cb: (b * nt + i, _c))
    vec = pl.BlockSpec((1, SCAL_COLS), lambda b, i: (0, 0))
    nc_all = B * nchunks
    pv, um, nm, stats = pl.pallas_call(
        functools.partial(_mlstm_prep_kernel, L=L, G=G),
        grid=(B, nt),
        in_specs=[rowblk(tile, QKV_COLS, 3), rowblk(tile, SCAL_COLS), vec, vec],
        out_specs=[rowblk(tile, HEADS * HD), rowblk(G * HD, HEADS * HD), rowblk(G * 8, HEADS * HD),
                   rowblk(tile, SCAL_COLS)],
        out_shape=[jax.ShapeDtypeStruct((M, HEADS * HD), F32), jax.ShapeDtypeStruct((nc_all * HD, HEADS * HD), F32),
                   jax.ShapeDtypeStruct((nc_all * 8, HEADS * HD), F32), jax.ShapeDtypeStruct((M, SCAL_COLS), F32)],
        compiler_params=_params(("parallel", "parallel")),
        name="mlstm_prep",
    )(z, sc, ib_vec, fb_vec)

    cps = _pick(nchunks, (8, 4, 2, 1))
    r3 = lambda a: a.reshape(B, a.shape[0] // B, a.shape[-1])
    blk = lambda r, w, cb=0: pl.BlockSpec((B, r, w), lambda i, _c=cb: (0, i, _c))
    st = pl.BlockSpec((B, HEADS, HD, HD), lambda i: (0, 0, 0, 0))
    vec8 = pl.BlockSpec((B, HEADS, 8, HD), lambda i: (0, 0, 0, 0))
    o, c_new, n_new, m_new = pl.pallas_call(
        functools.partial(_mlstm_scan_kernel, L=L, cps=cps, B=B),
        grid=(nchunks // cps,),
        in_specs=[blk(cps * L, HEADS * HD, 3 * QKV_COLS // (HEADS * HD)), blk(cps * L, HEADS * HD, 13),
                  blk(cps * L, HEADS * HD), blk(cps * HD, HEADS * HD), blk(cps * 8, HEADS * HD),
                  blk(cps * L, SCAL_COLS), st, vec8, vec8, pl.BlockSpec((1, HD), lambda i: (0, 0))],
        out_specs=[blk(cps * L, HEADS * HD), st, vec8, vec8],
        out_shape=[jax.ShapeDtypeStruct((B, T, HEADS * HD), BF16), jax.ShapeDtypeStruct((B, HEADS, HD, HD), F32),
                   jax.ShapeDtypeStruct((B, HEADS, 8, HD), F32), jax.ShapeDtypeStruct((B, HEADS, 8, HD), F32)],
        scratch_shapes=[pltpu.VMEM((B, HEADS, HD, HD), F32), pltpu.VMEM((B, HEADS, 8, HD), F32),
                        pltpu.VMEM((B, HEADS, 8, HD), F32)],
        compiler_params=_params(("arbitrary",)),
        name="mlstm_scan",
    )(r3(z), r3(z), r3(pv), r3(um), r3(nm), r3(stats), c0, n0p, m0p, gn.reshape(1, HD))
    return o.reshape(M, HEADS * HD), c_new, n_new, m_new


def _ffn_up_act_kernel(x_ref, xh_ref, g_ref, wg_ref, wv_ref, prev_ref, cw_ref, o_ref, tail_ref,
                       xn_ref, xhn_ref, buf, *, tm, tpb):
    i = pl.program_id(0)

    def norm(x):
        ms = jnp.mean(x * x, axis=-1, keepdims=True)
        return (x * lax.rsqrt(ms + RMS_EPS) * g_ref[...]).astype(BF16)

    @pl.when(pl.program_id(1) == 0)
    def _():
        xn_ref[...] = norm(x_ref[...])
        xh = norm(xh_ref[...])
        xhn_ref[...] = jnp.concatenate([xh, xh], axis=0)

    wg = wg_ref[...]
    gate = jnp.dot(xn_ref[...], wg, preferred_element_type=F32)
    val = jnp.dot(xn_ref[...], wv_ref[...], preferred_element_type=F32)
    halo = jnp.dot(xhn_ref[...], wg, preferred_element_type=F32)[0:HALO, :]
    buf[0:HALO, :] = jnp.where(i % tpb == 0, prev_ref[0], halo)
    buf[HALO:HALO + tm, :] = gate
    cwh = 0.5 * cw_ref[...]
    g = buf[...]
    yh = g[HALO:, :] * cwh[FFN_CONV - 1:FFN_CONV, :]
    for t in range(1, FFN_CONV):
        yh = yh + pltpu.roll(g, t, axis=0)[HALO:, :] * cwh[FFN_CONV - 1 - t:FFN_CONV - t, :]
    o_ref[...] = (yh * (jnp.tanh(yh) + 1.0) * val).astype(o_ref.dtype)
    tail_ref[0] = buf[tm:tm + HALO, :]


def ffn_up_act(x, gain, w_up, prev8, conv_w, B, T):
    M, K = x.shape
    Fd = conv_w.shape[1]
    tm = _pick(T, (1024, 512, 256, 128, 64, 32, 16, 8))
    tn = _pick(Fd, (512, 256, 128))
    tpb, nf = T // tm, Fd // tn
    hb = tm // HALO
    hmid, tails = pl.pallas_call(
        functools.partial(_ffn_up_act_kernel, tm=tm, tpb=tpb),
        grid=(M // tm, nf),
        in_specs=[pl.BlockSpec((tm, K), lambda i, j: (i, 0)),
                  pl.BlockSpec((HALO, K), lambda i, j: (jnp.maximum(i * hb - 1, 0), 0)),
                  pl.BlockSpec((1, K), lambda i, j: (0, 0)),
                  pl.BlockSpec((K, tn), lambda i, j: (0, j)),
                  pl.BlockSpec((K, tn), lambda i, j: (0, nf + j)),
                  pl.BlockSpec((1, HALO, tn), lambda i, j: (i // tpb, 0, j)),
                  pl.BlockSpec((FFN_CONV, tn), lambda i, j: (0, j))],
        out_specs=[pl.BlockSpec((tm, tn), lambda i, j: (i, j)),
                   pl.BlockSpec((1, HALO, tn), lambda i, j: (i, 0, j))],
        out_shape=[jax.ShapeDtypeStruct((M, Fd), BF16), jax.ShapeDtypeStruct((M // tm, HALO, Fd), F32)],
        scratch_shapes=[pltpu.VMEM((tm, K), BF16), pltpu.VMEM((2 * HALO, K), BF16),
                        pltpu.VMEM((tm + HALO, tn), F32)],
        compiler_params=_params(("parallel", "arbitrary")),
        name="ffn_up_act",
    )(x, x, gain.reshape(1, K), w_up, w_up, prev8, conv_w)
    return hmid, tails[tpb - 1::tpb]


def _t5_bucket(rel):
    half = NUM_BUCKETS // 2
    exact = half // 2
    n = jnp.abs(rel)
    nf = jnp.maximum(n, 1).astype(F32)
    large = exact + (jnp.log(nf / exact) / math.log(MAX_DISTANCE / exact) * (half - exact)).astype(jnp.int32)
    large = jnp.minimum(large, half - 1)
    return jnp.where(rel > 0, half, 0) + jnp.where(n < exact, n, large)


def _rel_bias_vec(rel_bias, rel):
    bucket = _t5_bucket(rel)
    rb = rel_bias.astype(F32) * LOG2E
    H = rb.shape[1]
    out = jnp.zeros((H,) + rel.shape, F32)
    for kb in range(NUM_BUCKETS):
        out = jnp.where((bucket == kb)[None], rb[kb].reshape((H,) + (1,) * rel.ndim), out)
    return out


def _rel_bias_const(rel_bias, rel, R, C):
    v = _rel_bias_vec(rel_bias, jnp.full((1,), rel, jnp.int32))
    return jnp.broadcast_to(v[:, :, None], (v.shape[0], R, C))


def _rel_bias_toeplitz(rel_bias, R, C, off):
    ar = lambda n: jnp.arange(n, dtype=jnp.int32)
    if R * C <= 256 * 1024 or R % LANES or C % LANES:
        return _rel_bias_vec(rel_bias, ar(C)[None, :] - ar(R)[:, None] + off)
    nr, nc = R // LANES, C // LANES
    deltas = jnp.arange(-(nr - 1), nc, dtype=jnp.int32)
    rel = LANES * deltas[:, None, None] + (ar(LANES)[None, None, :] - ar(LANES)[None, :, None]) + off
    small = _rel_bias_vec(rel_bias, rel)
    rows = [jnp.concatenate([small[:, e - a + nr - 1] for e in range(nc)], axis=-1) for a in range(nr)]
    return jnp.concatenate(rows, axis=-2)


def _pad_rows(a, rows):
    return jnp.pad(a, ((0, 0), (rows - a.shape[1], 0), (0, 0)))


def _layer(x, B, T, L, lw, init, attn_fn, want_kt=False, dst=None):
    z, sc = rms_matmul(x, lw["norm_mix"], lw["w_in"], lw["w_in_scal"], name="in_proj")
    qn, kn32, kn16, v16, *more = qk_prep(z, lw["da_q_norm"], lw["da_k_norm"], B, T, want_kt, dst)
    knt = more[:1] if want_kt else []
    o_da = attn_fn(qn, knt[0] if want_kt else kn16, v16)
    o_g, s_new = gdn(z, sc, _pad_rows(init["gconv"], HALO), lw["gdn_conv_w"], lw["gdn_neg_a"], lw["gdn_dt_bias"],
                     init["S"], lw["gdn_out_norm"], B, T, L)
    n0p = jnp.broadcast_to(init["n"][:, :, None, :], (B, HEADS, 8, HD))
    m0p = jnp.broadcast_to(init["m"][:, :, None, None], (B, HEADS, 8, HD))
    o_m, c_new, n_new, m_new = mlstm(z, sc, lw["ml_i_bias"], lw["ml_f_bias"], init["C"], n0p, m0p,
                                     lw["ml_out_norm"], B, T, L)
    x1 = matmul_res([o_da, o_g, o_m], lw["w_out"], x, name="out_proj")
    hmid, gate_tail = ffn_up_act(x1, lw["norm_ffn"], lw["w_up"], _pad_rows(init["fconv"], HALO), lw["ffn_conv_w"], B, T)
    x2 = matmul_res([hmid], lw["w_down"], x1, name="ffn_down")
    z3 = z.reshape(B, T, Z_COLS)
    st = {
        "k": kn32 if dst is not None else kn32.reshape(B, T, HEADS, 2, HD),
        "v": more[-1] if dst is not None else z3[:, :, 2 * HEADS * DA_DV:3 * HEADS * DA_DV].reshape(B, T, HEADS, DA_DV),
        "gconv": z3[:, T - (GDN_CONV - 1):, 3 * HEADS * DA_DV:3 * HEADS * DA_DV + QKV_COLS],
        "S": s_new, "C": c_new, "n": n_new[:, :, 0, :], "m": m_new[:, :, 0, 0],
        "fconv": gate_tail[:, HALO - (FFN_CONV - 1):, :],
        "k16": kn16, "v16": v16,
    }
    return x2, st


def kernel(x_prompt, x_sample, cache_attn_k, cache_attn_v, state_gdn_conv, state_gdn_S, state_mlstm_C, state_mlstm_n, state_mlstm_m, state_ffn_conv, meta_tokens, rel_bias, norm_mix, norm_ffn, w_in, w_out, da_q_norm, da_k_norm, da_lq1, da_lk1, da_lq2, da_lk2, da_out_norm, gdn_conv_w, gdn_A_log, gdn_dt_bias, gdn_out_norm, ml_i_bias, ml_f_bias, ml_out_norm, ffn_w_up, ffn_conv_w, ffn_w_down):
    B, T, D = x_prompt.shape
    Bs, Ls, _ = x_sample.shape
    depth = w_in.shape[0]
    P = cache_attn_k.shape[2]
    NM = meta_tokens.shape[0]
    Fd = ffn_conv_w.shape[-1]
    W = HEADS * DA_DV
    assert T % CHUNK == 0 and Ls <= CHUNK and NM <= CHUNK and NM % 8 == 0 and Ls % 8 == 0

    tq = _pick(T, ATT_TQ)
    tk = _pick(tq, ATT_TK)
    tkc = _pick(P, (2048, 1024, 512, 256, 128))
    assert tk >= 96 or T == tk, "far prompt tiles must lie past the last distinct relative-position bucket"
    assert tkc >= 96 or P == tkc
    MP = 128

    r = jnp.arange(tq, dtype=jnp.int32)[:, None]
    c = jnp.arange(tk, dtype=jnp.int32)[None, :]
    tiles = []
    for d in range(-1, tq // tk):
        visible = ((c + tk * d) // CHUNK <= r // CHUNK)[None]
        tiles.append(jnp.where(visible, _rel_bias_toeplitz(rel_bias, tq, tk, tk * d), NEG_INF))
    bias_big = jnp.stack(tiles, axis=1)
    bias_far = _rel_bias_vec(rel_bias, jnp.full((1,), -2 * tk, jnp.int32))[:, 0]
    cm = jnp.arange(MP, dtype=jnp.int32)[None, :]
    bm0 = _rel_bias_toeplitz(rel_bias, tq, MP, -NM)
    bm1 = _rel_bias_const(rel_bias, -2 * tq, tq, MP)
    bias_meta_big = jnp.where((cm < NM)[None, None], jnp.stack([bm0, bm1], axis=1), NEG_INF)
    bias_new_s = _rel_bias_toeplitz(rel_bias, Ls, Ls, 0)
    bias_cache = jnp.stack([_rel_bias_const(rel_bias, -2 * tkc, Ls, tkc),
                            _rel_bias_toeplitz(rel_bias, Ls, tkc, -tkc)], axis=1)
    bias_new_m = _rel_bias_toeplitz(rel_bias, NM, NM, 0)

    sizes = (W, W, W, QKV_COLS, HEADS, HEADS, HEADS * HD, QKV_COLS, HEADS, HEADS, HEADS * HD)
    offs = [0]
    for s_ in sizes:
        offs.append(offs[-1] + s_)
    seg = lambda w, i: w[:, offs[i]:offs[i + 1]]

    xm = jnp.broadcast_to(meta_tokens.astype(F32)[None], (B, NM, D)).reshape(B * NM, D)
    xb = x_prompt.reshape(B * T, D)
    xs = x_sample.reshape(Bs * Ls, D)
    p_states, s_states = [], []
    k_prompt = v_prompt = None
    for l in range(depth):
        wl = w_in[l]
        w_main = jnp.concatenate([seg(wl, 0), seg(wl, 1), seg(wl, 2), seg(wl, 3), seg(wl, 7), seg(wl, 6), seg(wl, 10)],
                                 axis=1).astype(BF16)
        w_scal = jnp.concatenate([seg(wl, 4), seg(wl, 5), seg(wl, 8), seg(wl, 9),
                                  jnp.zeros((D, SCAL_COLS - 4 * HEADS), F32)], axis=1).astype(BF16)
        lw = {
            "norm_mix": norm_mix[l], "norm_ffn": norm_ffn[l], "w_in": w_main, "w_in_scal": w_scal,
            "w_out": w_out[l].astype(BF16), "w_up": ffn_w_up[l].astype(BF16), "w_down": ffn_w_down[l].astype(BF16),
            "da_q_norm": da_q_norm[l], "da_k_norm": da_k_norm[l], "gdn_conv_w": gdn_conv_w[l],
            "gdn_neg_a": -jnp.exp(gdn_A_log[l].astype(F32)), "gdn_dt_bias": gdn_dt_bias[l], "gdn_out_norm": gdn_out_norm[l],
            "ml_i_bias": ml_i_bias[l], "ml_f_bias": ml_f_bias[l], "ml_out_norm": ml_out_norm[l],
            "ffn_conv_w": ffn_conv_w[l],
        }
        lam_init = 0.8 - 0.6 * math.exp(-0.3 * l)
        lam = (jnp.exp(jnp.sum(da_lq1[l].astype(F32) * da_lk1[l].astype(F32)))
               - jnp.exp(jnp.sum(da_lq2[l].astype(F32) * da_lk2[l].astype(F32))) + lam_init).reshape(1, 1)
        out_scale = 1.0 - lam_init
        gain_o = da_out_norm[l]

        zero = {"gconv": jnp.zeros((B, GDN_CONV - 1, QKV_COLS), F32), "S": jnp.zeros((B, HEADS, HD, HD), F32),
                "C": jnp.zeros((B, HEADS, HD, HD), F32), "n": jnp.zeros((B, HEADS, HD), F32),
                "m": jnp.zeros((B, HEADS), F32), "fconv": jnp.zeros((B, FFN_CONV - 1, Fd), F32)}
        xm, st_m = _layer(xm, B, NM, NM, lw, zero,
                          lambda q, k, v: attn_small(q, k, v, None, bias_new_m, lam, gain_o, out_scale, B, NM),
                          dst=(k_prompt, v_prompt, depth, NM + T, l, 0))
        k_prompt, v_prompt = st_m["k"], st_m["v"]

        kmeta = jnp.pad(st_m["k16"].reshape(B, NM, W), ((0, 0), (0, MP - NM), (0, 0)))
        kmeta_t = jnp.transpose(kmeta, (0, 2, 1)).reshape(B * W, MP)
        vmeta = jnp.pad(st_m["v16"].reshape(B, NM, W), ((0, 0), (0, MP - NM), (0, 0))).reshape(B * MP, W)
        xb, st_b = _layer(xb, B, T, CHUNK, lw, st_m,
                          lambda q, kt, v: attn_big(q, kt, v, kmeta_t, vmeta, bias_big, bias_far, bias_meta_big, lam, gain_o,
                                                    out_scale, B, T, tq, tk, ATT_RB), want_kt=True,
                          dst=(k_prompt, v_prompt, depth, NM + T, l, NM))
        k_prompt, v_prompt = st_b["k"], st_b["v"]

        init_s = {"gconv": state_gdn_conv[l], "S": state_gdn_S[l].astype(F32), "C": state_mlstm_C[l].astype(F32),
                  "n": state_mlstm_n[l].astype(F32), "m": state_mlstm_m[l].astype(F32), "fconv": state_ffn_conv[l]}
        cache = (cache_attn_k[l].reshape(Bs * P, W).astype(BF16), cache_attn_v[l].reshape(Bs * P, W).astype(BF16),
                 bias_cache, tkc)
        xs, st_s = _layer(xs, Bs, Ls, Ls, lw, init_s,
                          lambda q, k, v: attn_small(q, k, v, cache, bias_new_s, lam, gain_o, out_scale, Bs, Ls))

        names = ("k", "v", "gconv", "S", "C", "n", "m", "fconv")
        p_states.append([st_b[n_] for n_ in names])
        s_states.append([st_s[n_] for n_ in names])

    p_out = [k_prompt, v_prompt] + [jnp.stack([p_states[l][i] for l in range(depth)]) for i in range(2, 8)]
    s_out = [jnp.stack([s_states[l][i] for l in range(depth)]) for i in range(8)]
    y_prompt = xb.reshape(B, T, D)
    y_sample = xs.reshape(Bs, Ls, D)
    return (y_prompt, y_sample, *p_out, *s_out)
```
